```python
import math
import jax, jax.numpy as jnp
from jax import lax
import numpy as np

D_MODEL = 1024
BATCH = 4
SEQ = 4096
DEPTH = 2
DEC_BATCH = 128
DEC_SEQ = 4
PAST_LEN = 2048
PAGE_SIZE = 128

CONV_W = 4
EPS = 1e-6
D_LRU = D_MODEL // 2
LRU_BLOCKS = 8
LRU_BLOCK = D_LRU // LRU_BLOCKS
LRU_C = 8.0
HEAD_DIM = 64
D_ATTN = D_MODEL // 2
N_ATTN_HEADS = D_ATTN // HEAD_DIM
DILATIONS = ((128, 1), (512, 4), (2048, 16))
MAX_WINDOW = 2048
ATTN_BLOCK = 128
D_SSD = D_MODEL
SSD_HEAD_DIM = 64
SSD_HEADS = D_SSD // SSD_HEAD_DIM
SSD_GROUPS = 2
SSD_STATE = 128
SSD_CHUNK = 128
D_XBC = D_SSD + 2 * SSD_GROUPS * SSD_STATE
D_MIX = D_LRU + D_ATTN + D_SSD
IN_SPLITS = (D_LRU, D_LRU, D_ATTN, D_ATTN, D_ATTN, D_ATTN, D_SSD, D_XBC, SSD_HEADS)
D_IN = sum(IN_SPLITS)

kernel_name = 'hybrid_lru_dilated_attn_ssd_step'


def rmsnorm(x, g):
    xf = x.astype(jnp.float32)
    y = xf * lax.rsqrt(jnp.mean(xf * xf, axis=-1, keepdims=True) + EPS)
    return (y * g.astype(jnp.float32)).astype(x.dtype)


def causal_conv(x, buf, w, b):
    L = x.shape[1]
    xp = jnp.concatenate([buf.astype(x.dtype), x], axis=1)
    y = b.astype(x.dtype) + xp[:, 0:L] * w[0]
    for tap in range(1, CONV_W):
        y = y + xp[:, tap:tap + L] * w[tap]
    return y, xp[:, L:]


def _lin_comb(left, right):
    a1, b1 = left
    a2, b2 = right
    return a1 * a2, a2 * b1 + b2


def rg_lru(x, h0, w_a, b_a, w_x, b_x, lam):
    f32 = jnp.float32
    Bn, L, _ = x.shape
    xf = x.astype(f32)
    xb = xf.reshape(Bn, L, LRU_BLOCKS, LRU_BLOCK)
    r = jax.nn.sigmoid(jnp.einsum('blki,kij->blkj', xb, w_a.astype(f32)).reshape(Bn, L, D_LRU) + b_a.astype(f32))
    i = jax.nn.sigmoid(jnp.einsum('blki,kij->blkj', xb, w_x.astype(f32)).reshape(Bn, L, D_LRU) + b_x.astype(f32))
    log_a = -LRU_C * r * jax.nn.softplus(-lam.astype(f32))
    a = jnp.exp(log_a)
    bt = jnp.sqrt(-jnp.expm1(2.0 * log_a)) * (i * xf)
    bt = bt.at[:, 0].add(a[:, 0] * h0.astype(f32))
    _, hs = lax.associative_scan(_lin_comb, (a, bt), axis=1)
    return hs.astype(x.dtype), hs[:, -1]


def alibi_slopes():
    return jnp.exp2(-8.0 * jnp.arange(1, N_ATTN_HEADS + 1, dtype=jnp.float32) / N_ATTN_HEADS)


def band_attn_prompt(q, k, v, slopes, window, dil):
    f32 = jnp.float32
    Bn, S, H, Dh = q.shape
    span = window // dil
    unit = dil * ATTN_BLOCK
    Sp = -(-S // unit) * unit
    M = Sp // dil
    nb = M // ATTN_BLOCK

    def residues(t):
        t = jnp.pad(t, ((0, 0), (0, Sp - S), (0, 0), (0, 0)))
        t = t.reshape(Bn, M, dil, H, Dh).transpose(0, 2, 1, 3, 4)
        return t.reshape(Bn, dil, nb, ATTN_BLOCK, H, Dh)

    def with_prev(t):
        prev = jnp.pad(t, ((0, 0), (0, 0), (1, 0), (0, 0), (0, 0), (0, 0)))[:, :, :-1]
        return jnp.concatenate([prev, t], axis=3)

    qb = residues(q)
    kk = with_prev(residues(k))
    vv = with_prev(residues(v))
    s = jnp.einsum('brnqhd,brnkhd->brnhqk', qb, kk, preferred_element_type=f32) * (Dh ** -0.5)
    iq = jnp.arange(ATTN_BLOCK)[:, None]
    ik = jnp.arange(2 * ATTN_BLOCK)[None, :]
    j = ATTN_BLOCK + iq - ik
    mk = jnp.arange(nb)[:, None, None] * ATTN_BLOCK + ik - ATTN_BLOCK
    valid = (j >= 0) & (j <= span) & (mk >= 0)
    bias = -slopes[:, None, None] * (j * dil).astype(f32)
    s = jnp.where(valid[None, None, :, None], s + bias[None, None, None], -jnp.inf)
    m = jnp.max(s, axis=-1)
    p = jnp.exp(s - m[..., None])
    l = jnp.sum(p, axis=-1)
    m = jnp.moveaxis(m, -1, -2)
    l = jnp.moveaxis(l, -1, -2)
    o = jnp.einsum('brnhqk,brnkhd->brnqhd', p.astype(v.dtype), vv, preferred_element_type=f32) / l[..., None]

    def merge(t):
        t = t.reshape((Bn, dil, M) + t.shape[4:])
        t = jnp.moveaxis(t, 1, 2).reshape((Bn, Sp) + t.shape[3:])
        return t[:, :S]

    return merge(o), merge(m), merge(l)


def band_attn_sample(q, k_all, v_all, W, slopes, window, dil):
    f32 = jnp.float32
    L = q.shape[1]
    Dh = q.shape[-1]
    span = window // dil
    jd = jnp.arange(span + 1) * dil
    idx = W + jnp.arange(L)[:, None] - jd[None, :]
    valid = idx >= 0
    idx = jnp.maximum(idx, 0)
    kg = k_all[:, idx]
    vg = v_all[:, idx]
    s = jnp.einsum('blhd,bljhd->blhj', q, kg, preferred_element_type=f32) * (Dh ** -0.5)
    s = s - slopes[:, None] * jd.astype(f32)
    s = jnp.where(valid[None, :, None, :], s, -jnp.inf)
    m = jnp.max(s, axis=-1)
    p = jnp.exp(s - m[..., None])
    l = jnp.sum(p, axis=-1)
    o = jnp.einsum('blhj,bljhd->blhd', p.astype(vg.dtype), vg, preferred_element_type=f32) / l[..., None]
    return o, m, l


def combine_dilations(parts):
    m_all = parts[0][1]
    for _, m, _ in parts[1:]:
        m_all = jnp.maximum(m_all, m)
    o0, m0, l0 = parts[0]
    w0 = l0 * jnp.exp(m0 - m_all)
    num = w0[..., None] * o0
    den = w0
    for o, m, l in parts[1:]:
        wgt = l * jnp.exp(m - m_all)
        num = num + wgt[..., None] * o
        den = den + wgt
    return num / den[..., None]


def ssd_chunked(x, dt, A, Bm, Cm, h0):
    Bn, L, H, P = x.shape
    Q = min(SSD_CHUNK, L)
    nc = -(-L // Q)
    pad = nc * Q - L
    rep = H // SSD_GROUPS
    Bh = jnp.repeat(Bm, rep, axis=2)
    Ch = jnp.repeat(Cm, rep, axis=2)

    def padl(t):
        return jnp.pad(t, ((0, 0), (0, pad)) + ((0, 0),) * (t.ndim - 2))

    x = padl(x).reshape(Bn, nc, Q, H, P)
    dt = padl(dt).reshape(Bn, nc, Q, H)
    Bh = padl(Bh).reshape(Bn, nc, Q, H, SSD_STATE)
    Ch = padl(Ch).reshape(Bn, nc, Q, H, SSD_STATE)
    cum = jnp.cumsum(dt * A, axis=2)
    seg = cum[:, :, :, None, :] - cum[:, :, None, :, :]
    causal = jnp.tril(jnp.ones((Q, Q), dtype=bool))
    decay = jnp.exp(jnp.where(causal[None, None, :, :, None], seg, -jnp.inf))
    xdt = x * dt[..., None]
    scores = jnp.einsum('bcihn,bcjhn->bcijh', Ch, Bh) * decay
    y_diag = jnp.einsum('bcijh,bcjhp->bcihp', scores, xdt)
    to_end = jnp.exp(cum[:, :, -1:, :] - cum)
    chunk_states = jnp.einsum('bcjhn,bcjh,bcjhp->bchpn', Bh, to_end, xdt)
    chunk_decay = jnp.exp(cum[:, :, -1, :])

    def step(hc, inp):
        dec, st = inp
        return dec[:, :, None, None] * hc + st, hc

    h_last, h_starts = lax.scan(step, h0, (jnp.moveaxis(chunk_decay, 1, 0), jnp.moveaxis(chunk_states, 1, 0)))
    h_starts = jnp.moveaxis(h_starts, 0, 1)
    y_off = jnp.einsum('bcihn,bchpn->bcihp', Ch, h_starts) * jnp.exp(cum)[..., None]
    y = (y_diag + y_off).reshape(Bn, nc * Q, H, P)[:, :L]
    return y, h_last


def layer(x, lru_conv0, lru_h0, k_buf, v_buf, ssd_conv0, ssd_h0, params):
    (pre_g, post_g, w_in, lru_conv_w, lru_conv_b, lru_w_a, lru_b_a, lru_w_x, lru_b_x, lru_lambda,
     ssd_conv_w, ssd_conv_b, ssd_dt_bias, ssd_a_log, ssd_d, ssd_norm_g, w_out) = params
    f32 = jnp.float32
    Bn, L, _ = x.shape
    h = rmsnorm(x, pre_g)
    u = h @ w_in
    offs = np.cumsum(IN_SPLITS)[:-1].tolist()
    x_lru, g_lru, q, k, v, g_attn, z, xbc, dt_raw = jnp.split(u, offs, axis=-1)

    xc, lru_conv_new = causal_conv(x_lru, lru_conv0, lru_conv_w, lru_conv_b)
    hs, lru_h_new = rg_lru(xc, lru_h0, lru_w_a, lru_b_a, lru_w_x, lru_b_x, lru_lambda)
    y_lru = hs * jax.nn.silu(g_lru)

    qh = q.reshape(Bn, L, N_ATTN_HEADS, HEAD_DIM)
    kh = k.reshape(Bn, L, N_ATTN_HEADS, HEAD_DIM)
    vh = v.reshape(Bn, L, N_ATTN_HEADS, HEAD_DIM)
    slopes = alibi_slopes()
    if k_buf is None:
        parts = [band_attn_prompt(qh, kh, vh, slopes, wd, dl) for (wd, dl) in DILATIONS]
        win = min(MAX_WINDOW, L)
        k_new = kh[:, L - win:]
        v_new = vh[:, L - win:]
    else:
        W = k_buf.shape[1]
        k_all = jnp.concatenate([k_buf.astype(kh.dtype), kh], axis=1)
        v_all = jnp.concatenate([v_buf.astype(vh.dtype), vh], axis=1)
        parts = [band_attn_sample(qh, k_all, v_all, W, slopes, wd, dl) for (wd, dl) in DILATIONS]
        k_new = kh
        v_new = vh
    o = combine_dilations(parts)
    y_attn = o.reshape(Bn, L, D_ATTN).astype(x.dtype) * jax.nn.silu(g_attn)

    xbc_c, ssd_conv_new = causal_conv(xbc, ssd_conv0, ssd_conv_w, ssd_conv_b)
    xbc_c = jax.nn.silu(xbc_c).astype(f32)
    xs, bm, cm = jnp.split(xbc_c, [D_SSD, D_SSD + SSD_GROUPS * SSD_STATE], axis=-1)
    xs = xs.reshape(Bn, L, SSD_HEADS, SSD_HEAD_DIM)
    bm = bm.reshape(Bn, L, SSD_GROUPS, SSD_STATE)
    cm = cm.reshape(Bn, L, SSD_GROUPS, SSD_STATE)
    dt = jax.nn.softplus(dt_raw.astype(f32) + ssd_dt_bias.astype(f32))
    A = -jnp.exp(ssd_a_log.astype(f32))
    ys, ssd_h_new = ssd_chunked(xs, dt, A, bm, cm, ssd_h0.astype(f32))
    ys = ys + ssd_d.astype(f32)[:, None] * xs
    yg = (ys.reshape(Bn, L, D_SSD) * jax.nn.silu(z.astype(f32))).reshape(Bn, L, SSD_GROUPS, D_SSD // SSD_GROUPS)
    yg = yg * lax.rsqrt(jnp.mean(yg * yg, axis=-1, keepdims=True) + EPS)
    y_ssd = (yg.reshape(Bn, L, D_SSD) * ssd_norm_g.astype(f32)).astype(x.dtype)

    mix = jnp.concatenate([y_lru, y_attn, y_ssd], axis=-1) @ w_out
    out = x + rmsnorm(mix, post_g)
    return out, (lru_conv_new, lru_h_new, k_new, v_new, ssd_conv_new, ssd_h_new)


def setup_inputs(seed: int = 0) -> dict:
    key = jax.random.key(seed)
    ks = list(jax.random.split(key, 32))
    f32 = jnp.float32
    kv_win = min(MAX_WINDOW, PAST_LEN)

    def nrm(i, shape, scale):
        return scale * jax.random.normal(ks[i], shape, f32)

    a8 = jax.random.uniform(ks[20], (DEPTH, D_LRU), f32, 0.9, 0.999)
    sig = a8 ** (1.0 / LRU_C)
    dt0 = jnp.exp(jax.random.uniform(ks[21], (DEPTH, SSD_HEADS), f32, math.log(1e-3), math.log(1e-1)))
    return {
        'x_prompt': nrm(0, (BATCH, SEQ, D_MODEL), 1.0),
        'x_sample': nrm(1, (DEC_BATCH, DEC_SEQ, D_MODEL), 1.0),
        'state_lru_conv': nrm(2, (DEPTH, DEC_BATCH, CONV_W - 1, D_LRU), 1.0),
        'state_lru_h': nrm(3, (DEPTH, DEC_BATCH, D_LRU), 0.5),
        'cache_attn_k': nrm(4, (DEPTH, DEC_BATCH, kv_win, N_ATTN_HEADS, HEAD_DIM), 1.0),
        'cache_attn_v': nrm(5, (DEPTH, DEC_BATCH, kv_win, N_ATTN_HEADS, HEAD_DIM), 1.0),
        'state_ssd_conv': nrm(6, (DEPTH, DEC_BATCH, CONV_W - 1, D_XBC), 1.0),
        'state_ssd_h': nrm(7, (DEPTH, DEC_BATCH, SSD_HEADS, SSD_HEAD_DIM, SSD_STATE), 0.1),
        'pre_norm_g': 1.0 + nrm(8, (DEPTH, D_MODEL), 0.05),
        'post_norm_g': 1.0 + nrm(9, (DEPTH, D_MODEL), 0.05),
        'w_in': nrm(10, (DEPTH, D_MODEL, D_IN), D_MODEL ** -0.5),
        'lru_conv_w': nrm(11, (DEPTH, CONV_W, D_LRU), CONV_W ** -0.5),
        'lru_conv_b': nrm(12, (DEPTH, D_LRU), 0.01),
        'lru_w_a': nrm(13, (DEPTH, LRU_BLOCKS, LRU_BLOCK, LRU_BLOCK), LRU_BLOCK ** -0.5),
        'lru_b_a': nrm(14, (DEPTH, D_LRU), 0.01),
        'lru_w_x': nrm(15, (DEPTH, LRU_BLOCKS, LRU_BLOCK, LRU_BLOCK), LRU_BLOCK ** -0.5),
        'lru_b_x': nrm(16, (DEPTH, D_LRU), 0.01),
        'lru_lambda': jnp.log(sig) - jnp.log1p(-sig),
        'ssd_conv_w': nrm(17, (DEPTH, CONV_W, D_XBC), CONV_W ** -0.5),
        'ssd_conv_b': nrm(18, (DEPTH, D_XBC), 0.01),
        'ssd_dt_bias': dt0 + jnp.log(-jnp.expm1(-dt0)),
        'ssd_a_log': jnp.log(jax.random.uniform(ks[22], (DEPTH, SSD_HEADS), f32, 1.0, 16.0)),
        'ssd_d': 1.0 + nrm(23, (DEPTH, SSD_HEADS), 0.1),
        'ssd_norm_g': 1.0 + nrm(24, (DEPTH, D_SSD), 0.05),
        'w_out': nrm(25, (DEPTH, D_MIX, D_MODEL), D_MIX ** -0.5),
    }


def reference(x_prompt, x_sample, state_lru_conv, state_lru_h, cache_attn_k, cache_attn_v, state_ssd_conv, state_ssd_h,
              pre_norm_g, post_norm_g, w_in, lru_conv_w, lru_conv_b, lru_w_a, lru_b_a, lru_w_x, lru_b_x, lru_lambda,
              ssd_conv_w, ssd_conv_b, ssd_dt_bias, ssd_a_log, ssd_d, ssd_norm_g, w_out):
    f32 = jnp.float32
    yp = x_prompt
    ys = x_sample
    prompt_states = []
    sample_states = []
    for li in range(DEPTH):
        params = (pre_norm_g[li], post_norm_g[li], w_in[li], lru_conv_w[li], lru_conv_b[li], lru_w_a[li], lru_b_a[li],
                  lru_w_x[li], lru_b_x[li], lru_lambda[li], ssd_conv_w[li], ssd_conv_b[li], ssd_dt_bias[li],
                  ssd_a_log[li], ssd_d[li], ssd_norm_g[li], w_out[li])
        bp = yp.shape[0]
        yp, st_p = layer(yp,
                         jnp.zeros((bp, CONV_W - 1, D_LRU), yp.dtype),
                         jnp.zeros((bp, D_LRU), f32),
                         None, None,
                         jnp.zeros((bp, CONV_W - 1, D_XBC), yp.dtype),
                         jnp.zeros((bp, SSD_HEADS, SSD_HEAD_DIM, SSD_STATE), f32),
                         params)
        ys, st_s = layer(ys, state_lru_conv[li], state_lru_h[li], cache_attn_k[li], cache_attn_v[li],
                         state_ssd_conv[li], state_ssd_h[li], params)
        prompt_states.append(st_p)
        sample_states.append(st_s)
    lru_conv_p, lru_h_p, k_p, v_p, ssd_conv_p, ssd_h_p = [jnp.stack(c) for c in zip(*prompt_states)]
    lru_conv_s, lru_h_s, k_s, v_s, ssd_conv_s, ssd_h_s = [jnp.stack(c) for c in zip(*sample_states)]
    return (yp, ys, lru_conv_p, lru_conv_s, lru_h_p, lru_h_s, k_p, k_s, v_p, v_s, ssd_conv_p, ssd_conv_s, ssd_h_p, ssd_h_s)
```

```python
import functools

import numpy as np
import jax
import jax.numpy as jnp
from jax import lax
from jax.experimental import pallas as pl
from jax.experimental.pallas import tpu as pltpu

F32 = jnp.float32
BF16 = jnp.bfloat16

D_MODEL = 1024
CONV_W = 4
EPS = 1e-6
D_LRU = 512
LRU_BLOCKS = 8
LRU_C = 8.0
HEAD_DIM = 64
D_ATTN = 512
N_HEADS = 8
DILATIONS = ((128, 1), (512, 4), (2048, 16))
MAX_WINDOW = 2048
ATTN_BLOCK = 128
D_SSD = 1024
SSD_P = 64
SSD_HEADS = 16
SSD_GROUPS = 2
SSD_N = 128
SSD_CHUNK = 128
D_XBC = D_SSD + 2 * SSD_GROUPS * SSD_N
GROUP_W = D_SSD // SSD_GROUPS

OFF_XLRU, OFF_GLRU, OFF_Q, OFF_K, OFF_V, OFF_GATTN, OFF_Z, OFF_XBC, OFF_DT = (
    0, 512, 1024, 1536, 2048, 2560, 3072, 4096, 5632)
D_IN = 5648
LANES = 128
D_IN_PAD = OFF_DT + LANES

SPAN = 2048
MASKED = 1e33
NEG = -1e30

VMEM_LIMIT = 56 * 1024 * 1024


def _cparams(sem):
    return pltpu.CompilerParams(dimension_semantics=sem, vmem_limit_bytes=VMEM_LIMIT)


def _silu(x):
    return x * jax.nn.sigmoid(x)


def _softplus(x):
    return jnp.maximum(x, 0.0) + jnp.log1p(jnp.exp(-jnp.abs(x)))


def _split3(x):
    h1 = x.astype(BF16)
    r1 = x - h1.astype(F32)
    h2 = r1.astype(BF16)
    h3 = (r1 - h2.astype(F32)).astype(BF16)
    return h1, h2, h3


def _dot3_right(x, m_bf16):
    h1, h2, h3 = _split3(x)
    d = functools.partial(jnp.dot, preferred_element_type=F32)
    return d(h1, m_bf16) + d(h2, m_bf16) + d(h3, m_bf16)


def _dot3_left(m_bf16, x):
    h1, h2, h3 = _split3(x)
    d = functools.partial(jnp.dot, preferred_element_type=F32)
    return d(m_bf16, h1) + d(m_bf16, h2) + d(m_bf16, h3)


def _dot_nt(a, b):
    return lax.dot_general(a, b, (((1,), (1,)), ((), ())), preferred_element_type=F32)


def _dot_tn(a, b):
    return lax.dot_general(a, b, (((0,), (0,)), ((), ())), preferred_element_type=F32)


def _proj_kernel(x_ref, g_ref, w_ref, xlru_ref, glru_ref, q_ref, k_ref, v_ref, gattn_ref, z_ref, xbc_ref, dt_ref):
    x = x_ref[...]
    h = x * lax.rsqrt(jnp.mean(x * x, axis=-1, keepdims=True) + EPS) * g_ref[...]
    hb = h.astype(BF16)

    def mm(off, width):
        return jnp.dot(hb, w_ref[:, off:off + width], preferred_element_type=F32)

    xlru_ref[...] = mm(OFF_XLRU, D_LRU)
    glru_ref[...] = mm(OFF_GLRU, D_LRU)
    q = mm(OFF_Q, D_ATTN) * (HEAD_DIM ** -0.5)
    k = mm(OFF_K, D_ATTN)
    v = mm(OFF_V, D_ATTN)
    for c in range(D_ATTN // LANES):
        q_ref[c] = q[:, c * LANES:(c + 1) * LANES]
        k_ref[c] = k[:, c * LANES:(c + 1) * LANES]
        v_ref[c] = v[:, c * LANES:(c + 1) * LANES]
    gattn_ref[...] = mm(OFF_GATTN, D_ATTN)
    z_ref[...] = mm(OFF_Z, D_SSD)
    xbc_ref[...] = mm(OFF_XBC, D_XBC)
    dt_ref[...] = mm(OFF_DT, LANES)


def _proj(x2d, g, w_bf16, tm):
    T = x2d.shape[0]
    row = lambda width: pl.BlockSpec((tm, width), lambda i: (i, 0))
    pair = pl.BlockSpec((D_ATTN // LANES, tm, LANES), lambda i: (0, i, 0))
    sds = lambda *s: jax.ShapeDtypeStruct(s, F32)
    return pl.pallas_call(
        _proj_kernel,
        grid=(T // tm,),
        in_specs=[row(D_MODEL), pl.BlockSpec((1, D_MODEL), lambda i: (0, 0)),
                  pl.BlockSpec((D_MODEL, D_IN_PAD), lambda i: (0, 0))],
        out_specs=[row(D_LRU), row(D_LRU), pair, pair, pair, row(D_ATTN), row(D_SSD), row(D_XBC), row(LANES)],
        out_shape=[sds(T, D_LRU), sds(T, D_LRU), sds(4, T, LANES), sds(4, T, LANES), sds(4, T, LANES),
                   sds(T, D_ATTN), sds(T, D_SSD), sds(T, D_XBC), sds(T, LANES)],
        compiler_params=_cparams(("arbitrary",)),
        name="in_proj",
    )(x2d, g, w_bf16)


def _lru_gates(xc, wg_ref, bg_ref, lam_ref):
    pre = jnp.dot(xc.astype(BF16), wg_ref[...], preferred_element_type=F32) + bg_ref[...]
    r = jax.nn.sigmoid(pre[:, :D_LRU])
    i = jax.nn.sigmoid(pre[:, D_LRU:])
    log_a = (-LRU_C) * r * _softplus(-lam_ref[...])
    a = jnp.exp(log_a)
    t = jnp.tanh(log_a)
    return a, jnp.sqrt(-2.0 * t / (1.0 - t)) * (i * xc)


def _scan_rows(a, b):
    n = a.shape[0]
    row = lax.broadcasted_iota(jnp.int32, a.shape, 0)
    k = 1
    while k < n:
        keep = row >= k
        a_sh = jnp.where(keep, pltpu.roll(a, k, axis=0), 1.0)
        b_sh = jnp.where(keep, pltpu.roll(b, k, axis=0), 0.0)
        b = a * b_sh + b
        a = a * a_sh
        k *= 2
    return a, b


def _conv_rows(xe_ref, n, w_ref, b_ref):
    y = b_ref[...] + xe_ref[8:8 + n, :] * w_ref[3:4, :]
    for tap in range(CONV_W - 1):
        y = y + xe_ref[5 + tap:5 + tap + n, :] * w_ref[tap:tap + 1, :]
    return y


def _lru_p_kernel(x_ref, g_ref, cw_ref, cb_ref, wg_ref, bg_ref, lam_ref, y_ref, hl_ref, xe_ref, h_ref):
    t = pl.program_id(1)
    n = x_ref.shape[0]

    @pl.when(t == 0)
    def _():
        xe_ref[0:8, :] = jnp.zeros((8, D_LRU), F32)
        h_ref[...] = jnp.zeros_like(h_ref)

    xe_ref[8:8 + n, :] = x_ref[...]
    xc = _conv_rows(xe_ref, n, cw_ref, cb_ref)
    xe_ref[0:8, :] = xe_ref[n:n + 8, :]
    a, b = _lru_gates(xc, wg_ref, bg_ref, lam_ref)
    a_cum, hs = _scan_rows(a, b)
    hs = hs + a_cum * h_ref[0:1, :]
    h_last = hs[n - 1:n, :]
    h_ref[...] = jnp.broadcast_to(h_last, h_ref.shape)
    hl_ref[...] = h_last
    y_ref[...] = hs * _silu(g_ref[...])


def _lru_prompt(x, g, cw, cb, wg, bg, lam, tt):
    B, S, _ = x.shape
    tile = pl.BlockSpec((None, tt, D_LRU), lambda b, t: (b, t, 0))
    const = lambda r, c: pl.BlockSpec((r, c), lambda b, t: (0, 0))
    return pl.pallas_call(
        _lru_p_kernel,
        grid=(B, S // tt),
        in_specs=[tile, tile, const(CONV_W, D_LRU), const(1, D_LRU), const(D_LRU, 2 * D_LRU), const(1, 2 * D_LRU),
                  const(1, D_LRU)],
        out_specs=[tile, pl.BlockSpec((None, 1, D_LRU), lambda b, t: (b, 0, 0))],
        out_shape=[jax.ShapeDtypeStruct((B, S, D_LRU), F32), jax.ShapeDtypeStruct((B, 1, D_LRU), F32)],
        scratch_shapes=[pltpu.VMEM((tt + 8, D_LRU), F32), pltpu.VMEM((8, D_LRU), F32)],
        compiler_params=_cparams(("arbitrary", "arbitrary")),
        name="lru_prompt",
    )(x, g, cw, cb, wg, bg, lam)


def _lru_s_kernel(x_ref, g_ref, cs_ref, h0_ref, cw_ref, cb_ref, wg_ref, bg_ref, lam_ref, y_ref, hl_ref):
    L = x_ref.shape[0]
    db = x_ref.shape[1]
    xs = [cs_ref[i] for i in range(CONV_W - 1)] + [x_ref[i] for i in range(L)]
    xc = []
    for t in range(L):
        y = cb_ref[...] + xs[t] * cw_ref[0:1, :]
        for tap in range(1, CONV_W):
            y = y + xs[t + tap] * cw_ref[tap:tap + 1, :]
        xc.append(y)
    a, b = _lru_gates(jnp.concatenate(xc, axis=0), wg_ref, bg_ref, lam_ref)
    h = h0_ref[...]
    for t in range(L):
        h = a[t * db:(t + 1) * db] * h + b[t * db:(t + 1) * db]
        y_ref[t] = h * _silu(g_ref[t])
    hl_ref[...] = h


def _lru_sample(x_tm, g_tm, cs_tm, h0, cw, cb, wg, bg, lam):
    L, DB, _ = x_tm.shape
    return pl.pallas_call(
        _lru_s_kernel,
        out_shape=[jax.ShapeDtypeStruct((L, DB, D_LRU), F32), jax.ShapeDtypeStruct((DB, D_LRU), F32)],
        compiler_params=pltpu.CompilerParams(vmem_limit_bytes=VMEM_LIMIT),
        name="lru_sample",
    )(x_tm, g_tm, cs_tm, h0, cw, cb, wg, bg, lam)


def _block_order(rho, dil):
    if dil == 16:
        return rho
    if dil == 4:
        return 4 * (rho & 31) + (rho >> 5)
    return 16 * (rho & 7) + (rho >> 3)


def _chunks(dil):
    return {16: (1, 128), 4: (4, 32), 1: (16, 8)}[dil]


def _attn_p_kernel(q_ref, k_ref, v_ref, o_ref, qs, ks, vs, ra, ma, rb, mb, bias_ref):
    c = pl.program_id(1)
    s = pl.program_id(2)
    nres = SPAN // ATTN_BLOCK

    @pl.when(s == 0)
    def _():
        ks[0] = jnp.zeros((SPAN, LANES), F32)
        vs[0] = jnp.zeros((SPAN, LANES), F32)

    @pl.when(s > 0)
    def _():
        ks[0] = ks[1]
        vs[0] = vs[1]

    for r in range(nres):
        rows = pl.ds(r, ATTN_BLOCK, stride=nres)
        dst = slice(r * ATTN_BLOCK, (r + 1) * ATTN_BLOCK)
        qs[dst, :] = q_ref[rows, :]
        ks[1, dst, :] = k_ref[rows, :]
        vs[1, dst, :] = v_ref[rows, :]

    lane_q = lax.broadcasted_iota(jnp.int32, (ATTN_BLOCK, LANES), 1)
    lane_kv = lax.broadcasted_iota(jnp.int32, (2 * ATTN_BLOCK, LANES), 1)
    iq = lax.broadcasted_iota(jnp.int32, (ATTN_BLOCK, 2 * ATTN_BLOCK), 0)
    ik = lax.broadcasted_iota(jnp.int32, (ATTN_BLOCK, 2 * ATTN_BLOCK), 1)
    in_cur = ik >= ATTN_BLOCK
    head_sel = ((lane_q < HEAD_DIM, lane_kv < HEAD_DIM), (lane_q >= HEAD_DIM, lane_kv >= HEAD_DIM))
    states = ((ra, ma), (rb, mb))
    span0 = s == 0

    for window, dil in reversed(DILATIONS):
        nchunk, crow = _chunks(dil)
        j = ATTN_BLOCK + _block_order(iq, dil) - (
            jnp.where(in_cur, ATTN_BLOCK, 0) + _block_order(ik & (ATTN_BLOCK - 1), dil))
        valid = (j >= 0) & (j <= window // dil)
        dist = (j * dil).astype(F32)
        dist_all = jnp.where(valid, dist, MASKED)
        dist_cur = jnp.where(valid & in_cur, dist, MASKED)
        for hh in range(2):
            slope = jnp.exp2(-(jnp.zeros((1, 1), F32) + (2 * c + hh + 1).astype(F32)))
            bias_ref[hh, 0] = -slope * dist_all
            bias_ref[hh, 1] = -slope * dist_cur

        def gather(ref, slot, starts):
            parts = [ref[slot, pl.ds(st, crow), :] if slot is not None else ref[pl.ds(st, crow), :] for st in starts]
            return parts[0] if len(parts) == 1 else jnp.concatenate(parts, axis=0)

        def body(it, carry):
            if dil == 16:
                base = it * ATTN_BLOCK
                cb = 0
                last_off = 0
            elif dil == 4:
                base = (it >> 2) * ATTN_BLOCK
                cb = it & 3
                last_off = 3 * crow
            else:
                base = 0
                cb = it
                last_off = 15 * crow
            stride = {16: 0, 4: 4 * ATTN_BLOCK, 1: ATTN_BLOCK}[dil]
            cur_starts = [pl.multiple_of(base + a * stride + crow * cb, 8) for a in range(nchunk)]
            if dil == 16:
                prev_slot = 0
                prev_starts = cur_starts
                first_in_span = True
            else:
                has_prev = cb > 0
                prev_slot = jnp.where(has_prev, 1, 0)
                prev_off = jnp.where(has_prev, crow * (cb - 1), last_off)
                prev_starts = [pl.multiple_of(base + a * stride + prev_off, 8) for a in range(nchunk)]
                first_in_span = jnp.logical_not(has_prev)
            qf = gather(qs, None, cur_starts)
            kf = jnp.concatenate([gather(ks, prev_slot, prev_starts), gather(ks, 1, cur_starts)], axis=0)
            vf = jnp.concatenate([gather(vs, prev_slot, prev_starts), gather(vs, 1, cur_starts)], axis=0)
            kb = kf.astype(BF16)
            use_cur_only = jnp.logical_and(span0, first_in_span)
            for hh in range(2):
                q_keep, kv_keep = head_sel[hh]
                r_ref, m_ref = states[hh]
                qh = jnp.where(q_keep, qf, 0.0).astype(BF16)
                sc = _dot_nt(qh, kb) + jnp.where(use_cur_only, bias_ref[hh, 1], bias_ref[hh, 0])
                m = jnp.max(sc, axis=1, keepdims=True)
                p = jnp.exp(sc - m)
                vh = jnp.where(kv_keep, vf, 1.0).astype(BF16)
                r_new = jnp.dot(p.astype(BF16), vh, preferred_element_type=F32)
                m_new = jnp.broadcast_to(m, (ATTN_BLOCK, LANES))
                if dil != 16:
                    r_old = gather(r_ref, None, cur_starts)
                    m_old = gather(m_ref, None, cur_starts)
                    m_tot = jnp.maximum(m_old, m_new)
                    r_new = r_old * jnp.exp(m_old - m_tot) + r_new * jnp.exp(m_new - m_tot)
                    m_new = m_tot
                for a, st in enumerate(cur_starts):
                    r_ref[pl.ds(st, crow), :] = r_new[a * crow:(a + 1) * crow, :]
                    m_ref[pl.ds(st, crow), :] = m_new[a * crow:(a + 1) * crow, :]
            return carry

        lax.fori_loop(0, nres, body, 0)

    for r in range(nres):
        src = slice(r * ATTN_BLOCK, (r + 1) * ATTN_BLOCK)
        xa = ra[src, :]
        xb = rb[src, :]
        oa = xa / pltpu.roll(xa, HEAD_DIM, axis=1)
        ob = xb / pltpu.roll(xb, HEAD_DIM, axis=1)
        o_ref[pl.ds(r, ATTN_BLOCK, stride=nres), :] = jnp.where(lane_q < HEAD_DIM, oa, ob)


def _attn_prompt(q4, k4, v4):
    npair, B, S, _ = q4.shape
    spec = pl.BlockSpec((None, None, SPAN, LANES), lambda b, c, s: (c, b, s, 0))
    big = pltpu.VMEM((SPAN, LANES), F32)
    return pl.pallas_call(
        _attn_p_kernel,
        grid=(B, npair, S // SPAN),
        in_specs=[spec, spec, spec],
        out_specs=spec,
        out_shape=jax.ShapeDtypeStruct(q4.shape, F32),
        scratch_shapes=[big, pltpu.VMEM((2, SPAN, LANES), F32), pltpu.VMEM((2, SPAN, LANES), F32),
                        big, big, big, big, pltpu.VMEM((2, 2, ATTN_BLOCK, 2 * ATTN_BLOCK), F32)],
        compiler_params=_cparams(("arbitrary", "arbitrary", "arbitrary")),
        name="attn_prompt",
    )(q4, k4, v4)


def _attn_s_kernel(q_ref, kn_ref, vn_ref, kc_ref, vc_ref, o_ref, dist_ref, mult_ref, *, n_new):
    W = kc_ref.shape[2]
    rows = q_ref.shape[1]

    @pl.when(pl.program_id(0) == 0)
    def _():
        i = lax.broadcasted_iota(jnp.int32, (rows, W), 0)
        pos = lax.broadcasted_iota(jnp.int32, (rows, W), 1)
        d = W + i - pos
        cnt = jnp.zeros((rows, W), F32)
        for window, dil in DILATIONS:
            cnt = cnt + jnp.where(((d & (dil - 1)) == 0) & (d <= window), 1.0, 0.0)
        dist_ref[...] = d.astype(F32)
        mult_ref[...] = cnt

    i = lax.broadcasted_iota(jnp.int32, (rows, rows), 0)
    i2 = lax.broadcasted_iota(jnp.int32, (rows, rows), 1)
    dn = i - i2
    mult_n = jnp.zeros((rows, rows), F32)
    for window, dil in DILATIONS:
        ok = ((dn & (dil - 1)) == 0) & (dn >= 0) & (dn <= window) & (i2 < n_new)
        mult_n = mult_n + jnp.where(ok, 1.0, 0.0)
    dist_n = dn.astype(F32)
    dist_c = dist_ref[...]
    mult_c = mult_ref[...]
    for h in range(N_HEADS):
        slope = 2.0 ** (-(h + 1))
        q = q_ref[h].astype(BF16)
        sc = jnp.dot(q, kc_ref[h].astype(BF16), preferred_element_type=F32)
        sc = jnp.where(mult_c > 0, sc - slope * dist_c, NEG)
        sn = _dot_nt(q, kn_ref[h].astype(BF16))
        sn = jnp.where(mult_n > 0, sn - slope * dist_n, NEG)
        m = jnp.maximum(jnp.max(sc, axis=1, keepdims=True), jnp.max(sn, axis=1, keepdims=True))
        pc = mult_c * jnp.exp(sc - m)
        pn = mult_n * jnp.exp(sn - m)
        l = jnp.sum(pc, axis=1, keepdims=True) + jnp.sum(pn, axis=1, keepdims=True)
        o = _dot_nt(pc.astype(BF16), vc_ref[h].astype(BF16)) + jnp.dot(
            pn.astype(BF16), vn_ref[h].astype(BF16), preferred_element_type=F32)
        o_ref[h] = o / l


def _attn_sample(q, kn, vn, kc_t, vc_t, li, n_new):
    DB = q.shape[0]
    W = kc_t.shape[-1]
    small = pl.BlockSpec((None, N_HEADS, 8, HEAD_DIM), lambda b: (b, 0, 0, 0))
    cache = pl.BlockSpec((None, None, N_HEADS, HEAD_DIM, W), lambda b: (li, b, 0, 0, 0))
    return pl.pallas_call(
        functools.partial(_attn_s_kernel, n_new=n_new),
        grid=(DB,),
        in_specs=[small, small, small, cache, cache],
        out_specs=small,
        out_shape=jax.ShapeDtypeStruct(q.shape, F32),
        scratch_shapes=[pltpu.VMEM((8, W), F32), pltpu.VMEM((8, W), F32)],
        compiler_params=_cparams(("arbitrary",)),
        name="attn_sample",
    )(q, kn, vn, kc_t, vc_t)


def _group_norm_out(y, z, ng_ref):
    yg = y * _silu(z)
    outs = []
    for g in range(SSD_GROUPS):
        part = yg[:, g * GROUP_W:(g + 1) * GROUP_W]
        outs.append(part * lax.rsqrt(jnp.mean(part * part, axis=-1, keepdims=True) + EPS))
    return jnp.concatenate(outs, axis=1) * ng_ref[...]


def _ssd_p_kernel(xbc_ref, dt_ref, z_ref, cw_ref, cb_ref, dtb_ref, alog_ref, dx_ref, ng_ref, e_ref,
                  y_ref, hl_ref, xe_ref, ht_ref):
    t = pl.program_id(1)
    Q = SSD_CHUNK

    @pl.when(t == 0)
    def _():
        xe_ref[0:8, :] = jnp.zeros((8, D_XBC), F32)
        ht_ref[...] = jnp.zeros_like(ht_ref)

    xe_ref[8:8 + Q, :] = xbc_ref[...]
    xc = _silu(_conv_rows(xe_ref, Q, cw_ref, cb_ref))
    xe_ref[0:8, :] = xe_ref[Q:Q + 8, :]
    xs = xc[:, :D_SSD]
    bm = xc[:, D_SSD:D_SSD + SSD_GROUPS * SSD_N]
    cm = xc[:, D_SSD + SSD_GROUPS * SSD_N:]

    dt = _softplus(dt_ref[...] + dtb_ref[...])
    a_neg = -jnp.exp(alog_ref[...])
    ri = lax.broadcasted_iota(jnp.int32, (Q, Q), 0)
    ci = lax.broadcasted_iota(jnp.int32, (Q, Q), 1)
    causal = ri >= ci
    tri = jnp.where(causal, 1.0, 0.0).astype(BF16)
    cum = _dot3_left(tri, dt * a_neg)
    cum_t = cum.T
    e = e_ref[...]
    cum_x = _dot3_right(cum, e)
    dt_x = _dot3_right(dt, e)
    xdt = xs * dt_x
    ecum = jnp.exp(cum_x)
    cl = cum_x[Q - 1:Q, :]
    w_end = xdt * jnp.exp(cl - cum_x)
    dec = jnp.exp(cl)
    lane = lax.broadcasted_iota(jnp.int32, (Q, LANES), 1)

    y_parts = []
    for g in range(SSD_GROUPS):
        cg = cm[:, g * SSD_N:(g + 1) * SSD_N].astype(BF16)
        bg = bm[:, g * SSD_N:(g + 1) * SSD_N]
        cb = _dot_nt(cg, bg.astype(BF16))
        gl = slice(g * GROUP_W, (g + 1) * GROUP_W)
        h_old = ht_ref[g]
        y_off = jnp.dot(cg, h_old.astype(BF16), preferred_element_type=F32) * ecum[:, gl]
        ht_ref[g] = h_old * dec[:, gl] + jnp.dot(bg.T.astype(BF16), w_end[:, gl].astype(BF16),
                                                  preferred_element_type=F32)
        for pp in range(GROUP_W // LANES):
            pl_ = slice(g * GROUP_W + pp * LANES, g * GROUP_W + (pp + 1) * LANES)
            xdt_pair = xdt[:, pl_].astype(BF16)
            halves = []
            for hh in range(2):
                h = (g * GROUP_W + pp * LANES) // SSD_P + hh
                seg = jnp.broadcast_to(cum[:, h:h + 1], (Q, Q)) - jnp.broadcast_to(cum_t[h:h + 1, :], (Q, Q))
                scores = cb * jnp.exp(jnp.where(causal, seg, NEG))
                halves.append(jnp.dot(scores.astype(BF16), xdt_pair, preferred_element_type=F32))
            y_diag = jnp.where(lane < SSD_P, halves[0], halves[1])
            y_parts.append(y_diag + y_off[:, pp * LANES:(pp + 1) * LANES])
    y = jnp.concatenate(y_parts, axis=1) + dx_ref[...] * xs
    y_ref[...] = _group_norm_out(y, z_ref[...], ng_ref)

    @pl.when(t == pl.num_programs(1) - 1)
    def _():
        for g in range(SSD_GROUPS):
            hl_ref[g * (SSD_HEADS // SSD_GROUPS):(g + 1) * (SSD_HEADS // SSD_GROUPS)] = (
                ht_ref[g].T.reshape(SSD_HEADS // SSD_GROUPS, SSD_P, SSD_N))


def _ssd_prompt(xbc, dt, z, cw, cb, dtb, alog, dx, ng, e):
    B, S, _ = xbc.shape
    Q = SSD_CHUNK
    tile = lambda w: pl.BlockSpec((None, Q, w), lambda b, t: (b, t, 0))
    const = lambda r, c: pl.BlockSpec((r, c), lambda b, t: (0, 0))
    return pl.pallas_call(
        _ssd_p_kernel,
        grid=(B, S // Q),
        in_specs=[tile(D_XBC), tile(LANES), tile(D_SSD), const(CONV_W, D_XBC), const(1, D_XBC), const(1, LANES),
                  const(1, LANES), const(1, D_SSD), const(1, D_SSD), const(LANES, D_SSD)],
        out_specs=[tile(D_SSD), pl.BlockSpec((None, SSD_HEADS, SSD_P, SSD_N), lambda b, t: (b, 0, 0, 0))],
        out_shape=[jax.ShapeDtypeStruct((B, S, D_SSD), F32), jax.ShapeDtypeStruct((B, SSD_HEADS, SSD_P, SSD_N), F32)],
        scratch_shapes=[pltpu.VMEM((Q + 8, D_XBC), F32), pltpu.VMEM((SSD_GROUPS, SSD_N, GROUP_W), F32)],
        compiler_params=_cparams(("arbitrary", "arbitrary")),
        name="ssd_prompt",
    )(xbc, dt, z, cw, cb, dtb, alog, dx, ng, e)


def _ssd_s_kernel(xbc_ref, cs_ref, dt_ref, z_ref, h_ref, cw_ref, cb_ref, dtb_ref, alog_ref, dx_ref, ng_ref, e_ref,
                  y_ref, hn_ref, *, n_new):
    R = 8
    row = lax.broadcasted_iota(jnp.int32, (R, LANES), 0)
    real = row < n_new
    xe = jnp.concatenate([cs_ref[...], xbc_ref[...]], axis=0)
    y = cb_ref[...] + xe[8:16, :] * cw_ref[3:4, :]
    for tap in range(CONV_W - 1):
        y = y + xe[5 + tap:13 + tap, :] * cw_ref[tap:tap + 1, :]
    xc = _silu(y)
    xs = xc[:, :D_SSD]
    bm = xc[:, D_SSD:D_SSD + SSD_GROUPS * SSD_N]
    cm = xc[:, D_SSD + SSD_GROUPS * SSD_N:]

    dt = jnp.where(real, _softplus(dt_ref[...] + dtb_ref[...]), 0.0)
    da = dt * (-jnp.exp(alog_ref[...]))
    cum = da
    k = 1
    while k < R:
        cum = cum + jnp.where(row >= k, pltpu.roll(cum, k, axis=0), 0.0)
        k *= 2
    e = e_ref[...]
    cum_x = _dot3_right(cum, e)
    dt_x = _dot3_right(dt, e)
    xdt = xs * dt_x
    ecum = jnp.exp(cum_x)
    cl = cum_x[R - 1:R, :]
    w_end = xdt * jnp.exp(cl - cum_x)
    dec3 = _split3(jnp.exp(cl))
    ones = jnp.ones((R, SSD_N), BF16)
    zrow = jnp.zeros((R - 3, GROUP_W), BF16)

    y_parts = []
    for g in range(SSD_GROUPS):
        gl = slice(g * GROUP_W, (g + 1) * GROUP_W)
        cg = cm[:, g * SSD_N:(g + 1) * SSD_N]
        bg = bm[:, g * SSD_N:(g + 1) * SSD_N]
        hs = slice(g * (SSD_HEADS // SSD_GROUPS), (g + 1) * (SSD_HEADS // SSD_GROUPS))
        h_old = h_ref[hs].reshape(GROUP_W, SSD_N)
        y_g = _dot_nt(cg.astype(BF16), h_old.astype(BF16)) * ecum[:, gl]
        for j in range(n_new):
            cbj = jnp.sum(cg * bg[j:j + 1, :], axis=1, keepdims=True)
            term = cbj * jnp.exp(cum_x[:, gl] - cum_x[j:j + 1, gl]) * xdt[j:j + 1, gl]
            y_g = y_g + jnp.where(row[:, 0:1] >= j, term, 0.0)
        y_parts.append(y_g)
        dec_col = _dot_tn(jnp.concatenate([p[:, gl] for p in dec3] + [zrow], axis=0), ones)
        upd = _dot_tn(w_end[:, gl].astype(BF16), bg.astype(BF16))
        hn_ref[hs] = (h_old * dec_col + upd).reshape(SSD_HEADS // SSD_GROUPS, SSD_P, SSD_N)
    y = jnp.concatenate(y_parts, axis=1) + dx_ref[...] * xs
    y_ref[...] = _group_norm_out(y, z_ref[...], ng_ref)


def _ssd_sample(xbc, cs, dt, z, h0, cw, cb, dtb, alog, dx, ng, e, n_new):
    DB = xbc.shape[0]
    tile = lambda w: pl.BlockSpec((None, 8, w), lambda b: (b, 0, 0))
    const = lambda r, c: pl.BlockSpec((r, c), lambda b: (0, 0))
    state = pl.BlockSpec((None, SSD_HEADS, SSD_P, SSD_N), lambda b: (b, 0, 0, 0))
    return pl.pallas_call(
        functools.partial(_ssd_s_kernel, n_new=n_new),
        grid=(DB,),
        in_specs=[tile(D_XBC), tile(D_XBC), tile(LANES), tile(D_SSD), state, const(CONV_W, D_XBC), const(1, D_XBC),
                  const(1, LANES), const(1, LANES), const(1, D_SSD), const(1, D_SSD), const(LANES, D_SSD)],
        out_specs=[tile(D_SSD), state],
        out_shape=[jax.ShapeDtypeStruct((DB, 8, D_SSD), F32), jax.ShapeDtypeStruct(h0.shape, F32)],
        compiler_params=_cparams(("arbitrary",)),
        name="ssd_sample",
    )(xbc, cs, dt, z, h0, cw, cb, dtb, alog, dx, ng, e)


def _out_kernel(x_ref, ylru_ref, o_ref, gattn_ref, yssd_ref, w_ref, g_ref, out_ref):
    o = jnp.concatenate([o_ref[c] for c in range(D_ATTN // LANES)], axis=1)
    y_attn = o * _silu(gattn_ref[...])
    mix_in = jnp.concatenate([ylru_ref[...], y_attn, yssd_ref[...]], axis=1).astype(BF16)
    mix = jnp.dot(mix_in, w_ref[...], preferred_element_type=F32)
    out_ref[...] = x_ref[...] + mix * lax.rsqrt(jnp.mean(mix * mix, axis=-1, keepdims=True) + EPS) * g_ref[...]


def _out_proj(x2d, y_lru, o4, g_attn, y_ssd, w_bf16, g, tm):
    T = x2d.shape[0]
    row = lambda width: pl.BlockSpec((tm, width), lambda i: (i, 0))
    return pl.pallas_call(
        _out_kernel,
        grid=(T // tm,),
        in_specs=[row(D_MODEL), row(D_LRU), pl.BlockSpec((D_ATTN // LANES, tm, LANES), lambda i: (0, i, 0)),
                  row(D_ATTN), row(D_SSD), pl.BlockSpec((D_LRU + D_ATTN + D_SSD, D_MODEL), lambda i: (0, 0)),
                  pl.BlockSpec((1, D_MODEL), lambda i: (0, 0))],
        out_specs=row(D_MODEL),
        out_shape=jax.ShapeDtypeStruct((T, D_MODEL), F32),
        compiler_params=_cparams(("arbitrary",)),
        name="out_proj",
    )(x2d, y_lru, o4, g_attn, y_ssd, w_bf16, g)


def _block_diag(w):
    k, n, _ = w.shape
    return jnp.einsum("kij,kl->kilj", w, jnp.eye(k, dtype=w.dtype)).reshape(k * n, k * n)


def _pad_rows(a, rows):
    return jnp.pad(a, ((0, 0), (0, rows - a.shape[1]), (0, 0)))


def _pairs_to_heads(a4, nb, L):
    return a4.reshape(4, nb, L, 2, HEAD_DIM).transpose(1, 2, 0, 3, 4).reshape(nb, L, N_HEADS, HEAD_DIM)


def kernel(x_prompt, x_sample, state_lru_conv, state_lru_h, cache_attn_k, cache_attn_v, state_ssd_conv, state_ssd_h, pre_norm_g, post_norm_g, w_in, lru_conv_w, lru_conv_b, lru_w_a, lru_b_a, lru_w_x, lru_b_x, lru_lambda, ssd_conv_w, ssd_conv_b, ssd_dt_bias, ssd_a_log, ssd_d, ssd_norm_g, w_out):
    depth = w_in.shape[0]
    B, S, _ = x_prompt.shape
    DB, L, _ = x_sample.shape
    assert S % SPAN == 0 and L <= 8 and L >= CONV_W - 1
    win = min(MAX_WINDOW, S)

    lane_pad = lambda a: jnp.pad(a.reshape(1, -1), ((0, 0), (0, LANES - a.shape[-1])))
    expand = jnp.asarray(np.repeat(np.eye(LANES, SSD_HEADS, dtype=np.float32), SSD_P, axis=1), BF16)
    kc_t = jnp.transpose(cache_attn_k, (0, 1, 3, 4, 2))
    vc_t = jnp.transpose(cache_attn_v, (0, 1, 3, 4, 2))

    yp = x_prompt.reshape(B * S, D_MODEL)
    ys = x_sample.reshape(DB * L, D_MODEL)
    outs_p = []
    outs_s = []
    for li in range(depth):
        w_in_b = jnp.pad(w_in[li], ((0, 0), (0, D_IN_PAD - D_IN))).astype(BF16)
        w_out_b = w_out[li].astype(BF16)
        pre_g = pre_norm_g[li].reshape(1, -1)
        post_g = post_norm_g[li].reshape(1, -1)
        wg = jnp.concatenate([_block_diag(lru_w_a[li]), _block_diag(lru_w_x[li])], axis=1).astype(BF16)
        bg = jnp.concatenate([lru_b_a[li], lru_b_x[li]]).reshape(1, -1)
        lcw, lcb, lam = lru_conv_w[li], lru_conv_b[li].reshape(1, -1), lru_lambda[li].reshape(1, -1)
        scw, scb = ssd_conv_w[li], ssd_conv_b[li].reshape(1, -1)
        dtb, alog = lane_pad(ssd_dt_bias[li]), lane_pad(ssd_a_log[li])
        dx = jnp.repeat(ssd_d[li], SSD_P).reshape(1, -1)
        ng = ssd_norm_g[li].reshape(1, -1)

        xlru, glru, q4, k4, v4, gattn, z, xbc, dtr = _proj(yp, pre_g, w_in_b, 256)
        xlru3 = xlru.reshape(B, S, D_LRU)
        y_lru, lru_h_p = _lru_prompt(xlru3, glru.reshape(B, S, D_LRU), lcw, lcb, wg, bg, lam, 512)
        o4 = _attn_prompt(q4.reshape(4, B, S, LANES), k4.reshape(4, B, S, LANES), v4.reshape(4, B, S, LANES))
        xbc3 = xbc.reshape(B, S, D_XBC)
        y_ssd, ssd_h_p = _ssd_prompt(xbc3, dtr.reshape(B, S, LANES), z.reshape(B, S, D_SSD), scw, scb, dtb, alog, dx,
                                     ng, expand)
        yp = _out_proj(yp, y_lru.reshape(B * S, D_LRU), o4.reshape(4, B * S, LANES), gattn,
                       y_ssd.reshape(B * S, D_SSD), w_out_b, post_g, 512)
        k_p = _pairs_to_heads(k4, B, S)[:, S - win:]
        v_p = _pairs_to_heads(v4, B, S)[:, S - win:]
        outs_p.append((xlru3[:, S - (CONV_W - 1):], lru_h_p.reshape(B, D_LRU), k_p, v_p,
                       xbc3[:, S - (CONV_W - 1):], ssd_h_p))

        xlru, glru, q4, k4, v4, gattn, z, xbc, dtr = _proj(ys, pre_g, w_in_b, min(256, DB * L))
        xlru3 = xlru.reshape(DB, L, D_LRU)
        y_lru_tm, lru_h_s = _lru_sample(
            xlru3.transpose(1, 0, 2), glru.reshape(DB, L, D_LRU).transpose(1, 0, 2),
            state_lru_conv[li].transpose(1, 0, 2), state_lru_h[li], lcw, lcb, wg, bg, lam)
        y_lru = y_lru_tm.transpose(1, 0, 2).reshape(DB * L, D_LRU)

        def heads_first(a4):
            a = _pairs_to_heads(a4, DB, L).transpose(0, 2, 1, 3)
            return jnp.pad(a, ((0, 0), (0, 0), (0, 8 - L), (0, 0)))

        o_s = _attn_sample(heads_first(q4), heads_first(k4), heads_first(v4), kc_t, vc_t, li, L)
        o_s = o_s[:, :, :L].transpose(0, 2, 1, 3).reshape(DB * L, 4, LANES).transpose(1, 0, 2)

        xbc3 = xbc.reshape(DB, L, D_XBC)
        cs8 = jnp.pad(state_ssd_conv[li], ((0, 0), (8 - (CONV_W - 1), 0), (0, 0)))
        y_ssd8, ssd_h_s = _ssd_sample(_pad_rows(xbc3, 8), cs8, _pad_rows(dtr.reshape(DB, L, LANES), 8),
                                      _pad_rows(z.reshape(DB, L, D_SSD), 8), state_ssd_h[li], scw, scb, dtb, alog,
                                      dx, ng, expand, L)
        y_ssd = y_ssd8[:, :L].reshape(DB * L, D_SSD)
        ys = _out_proj(ys, y_lru, o_s, gattn, y_ssd, w_out_b, post_g, DB * L)
        outs_s.append((xlru3[:, L - (CONV_W - 1):], lru_h_s, _pairs_to_heads(k4, DB, L), _pairs_to_heads(v4, DB, L),
                       xbc3[:, L - (CONV_W - 1):], ssd_h_s))

    lru_conv_p, lru_h_p, k_p, v_p, ssd_conv_p, ssd_h_p = [jnp.stack(c) for c in zip(*outs_p)]
    lru_conv_s, lru_h_s, k_s, v_s, ssd_conv_s, ssd_h_s = [jnp.stack(c) for c in zip(*outs_s)]
    return (yp.reshape(B, S, D_MODEL), ys.reshape(DB, L, D_MODEL), lru_conv_p, lru_conv_s, lru_h_p, lru_h_s,
            k_p, k_s, v_p, v_s, ssd_conv_p, ssd_conv_s, ssd_h_p, ssd_h_s)
```

```python
import functools

import numpy as np
import jax
import jax.numpy as jnp
from jax import lax
from jax.experimental import pallas as pl
from jax.experimental.pallas import tpu as pltpu

F32 = jnp.float32
BF16 = jnp.bfloat16

D_MODEL = 1024
CONV_W = 4
EPS = 1e-6
D_LRU = 512
LRU_BLOCKS = 8
LRU_C = 8.0
HEAD_DIM = 64
D_ATTN = 512
N_HEADS = 8
DILATIONS = ((128, 1), (512, 4), (2048, 16))
MAX_WINDOW = 2048
ATTN_BLOCK = 128
D_SSD = 1024
SSD_P = 64
SSD_HEADS = 16
SSD_GROUPS = 2
SSD_N = 128
SSD_CHUNK = 128
D_XBC = D_SSD + 2 * SSD_GROUPS * SSD_N
GROUP_W = D_SSD // SSD_GROUPS

OFF_XLRU, OFF_GLRU, OFF_Q, OFF_K, OFF_V, OFF_GATTN, OFF_Z, OFF_XBC, OFF_DT = (
    0, 512, 1024, 1536, 2048, 2560, 3072, 4096, 5632)
D_IN = 5648
LANES = 128
D_IN_PAD = OFF_DT + LANES

SPAN = 2048
NRES = SPAN // ATTN_BLOCK
MASKED = 1e33
NEG = -1e30

VMEM_LIMIT = 56 * 1024 * 1024


def _cparams(sem):
    return pltpu.CompilerParams(dimension_semantics=sem, vmem_limit_bytes=VMEM_LIMIT)


def _silu(x):
    return x * jax.nn.sigmoid(x)


def _softplus(x):
    return jnp.maximum(x, 0.0) + jnp.log1p(jnp.exp(-jnp.abs(x)))


def _split3(x):
    h1 = x.astype(BF16)
    r1 = x - h1.astype(F32)
    h2 = r1.astype(BF16)
    h3 = (r1 - h2.astype(F32)).astype(BF16)
    return h1, h2, h3


def _dot3_right(x, m_bf16):
    h1, h2, h3 = _split3(x)
    d = functools.partial(jnp.dot, preferred_element_type=F32)
    return d(h1, m_bf16) + d(h2, m_bf16) + d(h3, m_bf16)


def _dot3_left(m_bf16, x):
    h1, h2, h3 = _split3(x)
    d = functools.partial(jnp.dot, preferred_element_type=F32)
    return d(m_bf16, h1) + d(m_bf16, h2) + d(m_bf16, h3)


def _dot_nt(a, b):
    return lax.dot_general(a, b, (((1,), (1,)), ((), ())), preferred_element_type=F32)


def _dot_tn(a, b):
    return lax.dot_general(a, b, (((0,), (0,)), ((), ())), preferred_element_type=F32)


def _proj_kernel(x_ref, g_ref, w_ref, *rest, first_win_tile, n_alias):
    rest = rest[n_alias:]
    xlru_ref, glru_ref, q_ref, k_ref, v_ref, gattn_ref, z_ref, xbc_ref, dt_ref = rest[:9]
    x = x_ref[...]
    h = x * lax.rsqrt(jnp.mean(x * x, axis=-1, keepdims=True) + EPS) * g_ref[...]
    hb = h.astype(BF16)

    def mm(off, width):
        return jnp.dot(hb, w_ref[:, off:off + width], preferred_element_type=F32)

    xlru_ref[...] = mm(OFF_XLRU, D_LRU)
    glru_ref[...] = mm(OFF_GLRU, D_LRU)
    q = mm(OFF_Q, D_ATTN) * (HEAD_DIM ** -0.5)
    k = mm(OFF_K, D_ATTN)
    v = mm(OFF_V, D_ATTN)
    for c in range(D_ATTN // LANES):
        q_ref[c] = q[:, c * LANES:(c + 1) * LANES]
        k_ref[c] = k[:, c * LANES:(c + 1) * LANES]
        v_ref[c] = v[:, c * LANES:(c + 1) * LANES]
    if len(rest) > 9:
        kt_ref, vt_ref = rest[9:]

        @pl.when(pl.program_id(1) >= first_win_tile)
        def _():
            kt_ref[...] = k.T
            vt_ref[...] = v.T

    gattn_ref[...] = mm(OFF_GATTN, D_ATTN)
    z_ref[...] = mm(OFF_Z, D_SSD)
    xbc_ref[...] = mm(OFF_XBC, D_XBC)
    dt_ref[...] = mm(OFF_DT, LANES)


def _proj(x3, g, w_bf16, tm, cache=None):
    B, S, _ = x3.shape
    row = lambda width: pl.BlockSpec((None, tm, width), lambda b, t: (b, t, 0))
    pair = pl.BlockSpec((D_ATTN // LANES, None, tm, LANES), lambda b, t: (0, b, t, 0))
    pair_shape = (4, B, S, LANES)
    sds = lambda *s: jax.ShapeDtypeStruct(s, F32)
    in_specs = [row(D_MODEL), pl.BlockSpec((1, D_MODEL), lambda b, t: (0, 0)),
                pl.BlockSpec((D_MODEL, D_IN_PAD), lambda b, t: (0, 0))]
    args = [x3, g, w_bf16]
    out_specs = [row(D_LRU), row(D_LRU), pair, pair, pair, row(D_ATTN), row(D_SSD), row(D_XBC), row(LANES)]
    out_shape = [sds(B, S, D_LRU), sds(B, S, D_LRU), sds(*pair_shape), sds(*pair_shape), sds(*pair_shape),
                 sds(B, S, D_ATTN), sds(B, S, D_SSD), sds(B, S, D_XBC), sds(B, S, LANES)]
    first_win_tile, n_alias, aliases = 0, 0, {}
    if cache is not None:
        li, depth, win, prev_kt, prev_vt = cache
        first_win_tile = (S - win) // tm
        cspec = pl.BlockSpec((None, None, D_ATTN, tm),
                             lambda b, t: (li, b, 0, jnp.maximum(t - first_win_tile, 0)))
        out_specs += [cspec, cspec]
        out_shape += [sds(depth, B, D_ATTN, win), sds(depth, B, D_ATTN, win)]
        if prev_kt is not None:
            n_alias = 2
            in_specs += [pl.BlockSpec(memory_space=pl.ANY)] * 2
            args += [prev_kt, prev_vt]
            aliases = {3: 9, 4: 10}
    return pl.pallas_call(
        functools.partial(_proj_kernel, first_win_tile=first_win_tile, n_alias=n_alias),
        grid=(B, S // tm),
        in_specs=in_specs,
        out_specs=out_specs,
        out_shape=out_shape,
        input_output_aliases=aliases,
        compiler_params=_cparams(("arbitrary", "arbitrary")),
        name="in_proj",
    )(*args)


def _lru_gates(xc, wg_ref, bg_ref, lam_ref):
    pre = jnp.dot(xc.astype(BF16), wg_ref[...], preferred_element_type=F32) + bg_ref[...]
    r = jax.nn.sigmoid(pre[:, :D_LRU])
    i = jax.nn.sigmoid(pre[:, D_LRU:])
    log_a = (-LRU_C) * r * _softplus(-lam_ref[...])
    a = jnp.exp(log_a)
    t = jnp.tanh(log_a)
    return a, jnp.sqrt(-2.0 * t / (1.0 - t)) * (i * xc)


def _scan_rows(a, b):
    n = a.shape[0]
    row = lax.broadcasted_iota(jnp.int32, a.shape, 0)
    k = 1
    while k < n:
        keep = row >= k
        a_sh = jnp.where(keep, pltpu.roll(a, k, axis=0), 1.0)
        b_sh = jnp.where(keep, pltpu.roll(b, k, axis=0), 0.0)
        b = a * b_sh + b
        a = a * a_sh
        k *= 2
    return a, b


def _conv_rows(xe_ref, n, w_ref, b_ref):
    y = b_ref[...] + xe_ref[8:8 + n, :] * w_ref[3:4, :]
    for tap in range(CONV_W - 1):
        y = y + xe_ref[5 + tap:5 + tap + n, :] * w_ref[tap:tap + 1, :]
    return y


def _lru_p_kernel(x_ref, g_ref, cw_ref, cb_ref, wg_ref, bg_ref, lam_ref, y_ref, hl_ref, xe_ref, h_ref):
    t = pl.program_id(1)
    n = x_ref.shape[0]

    @pl.when(t == 0)
    def _():
        xe_ref[0:8, :] = jnp.zeros((8, D_LRU), F32)
        h_ref[...] = jnp.zeros_like(h_ref)

    xe_ref[8:8 + n, :] = x_ref[...]
    xc = _conv_rows(xe_ref, n, cw_ref, cb_ref)
    xe_ref[0:8, :] = xe_ref[n:n + 8, :]
    a, b = _lru_gates(xc, wg_ref, bg_ref, lam_ref)
    a_cum, hs = _scan_rows(a, b)
    hs = hs + a_cum * h_ref[0:1, :]
    h_last = hs[n - 1:n, :]
    h_ref[...] = jnp.broadcast_to(h_last, h_ref.shape)
    hl_ref[...] = h_last
    y_ref[...] = hs * _silu(g_ref[...])


def _lru_prompt(x, g, cw, cb, wg, bg, lam, tt):
    B, S, _ = x.shape
    tile = pl.BlockSpec((None, tt, D_LRU), lambda b, t: (b, t, 0))
    const = lambda r, c: pl.BlockSpec((r, c), lambda b, t: (0, 0))
    return pl.pallas_call(
        _lru_p_kernel,
        grid=(B, S // tt),
        in_specs=[tile, tile, const(CONV_W, D_LRU), const(1, D_LRU), const(D_LRU, 2 * D_LRU), const(1, 2 * D_LRU),
                  const(1, D_LRU)],
        out_specs=[tile, pl.BlockSpec((None, 1, D_LRU), lambda b, t: (b, 0, 0))],
        out_shape=[jax.ShapeDtypeStruct((B, S, D_LRU), F32), jax.ShapeDtypeStruct((B, 1, D_LRU), F32)],
        scratch_shapes=[pltpu.VMEM((tt + 8, D_LRU), F32), pltpu.VMEM((8, D_LRU), F32)],
        compiler_params=_cparams(("arbitrary", "arbitrary")),
        name="lru_prompt",
    )(x, g, cw, cb, wg, bg, lam)


def _lru_s_kernel(x_ref, g_ref, cs_ref, h0_ref, cw_ref, cb_ref, wg_ref, bg_ref, lam_ref, y_ref, hl_ref):
    L = x_ref.shape[0]
    db = x_ref.shape[1]
    xs = [cs_ref[i] for i in range(CONV_W - 1)] + [x_ref[i] for i in range(L)]
    xc = []
    for t in range(L):
        y = cb_ref[...] + xs[t] * cw_ref[0:1, :]
        for tap in range(1, CONV_W):
            y = y + xs[t + tap] * cw_ref[tap:tap + 1, :]
        xc.append(y)
    a, b = _lru_gates(jnp.concatenate(xc, axis=0), wg_ref, bg_ref, lam_ref)
    h = h0_ref[...]
    for t in range(L):
        h = a[t * db:(t + 1) * db] * h + b[t * db:(t + 1) * db]
        y_ref[t] = h * _silu(g_ref[t])
    hl_ref[...] = h


def _lru_sample(x_tm, g_tm, cs_tm, h0, cw, cb, wg, bg, lam):
    L, DB, _ = x_tm.shape
    return pl.pallas_call(
        _lru_s_kernel,
        out_shape=[jax.ShapeDtypeStruct((L, DB, D_LRU), F32), jax.ShapeDtypeStruct((DB, D_LRU), F32)],
        compiler_params=pltpu.CompilerParams(vmem_limit_bytes=VMEM_LIMIT),
        name="lru_sample",
    )(x_tm, g_tm, cs_tm, h0, cw, cb, wg, bg, lam)


def _block_order(rho, dil):
    if dil == 16:
        return rho
    if dil == 4:
        return 4 * (rho & 31) + (rho >> 5)
    return 16 * (rho & 7) + (rho >> 3)


def _chunks(dil):
    return {16: (1, 128), 4: (4, 32), 1: (16, 8)}[dil]


def _attn_p_kernel(q_ref, k_ref, v_ref, o_ref, qs, ks, vs, r_st, m_st, bias_ref, *, unroll):
    c = pl.program_id(1)
    s = pl.program_id(2)
    nres = SPAN // ATTN_BLOCK
    lane_q = lax.broadcasted_iota(jnp.int32, (ATTN_BLOCK, LANES), 1)
    first_half = lane_q < HEAD_DIM

    @pl.when(s == 0)
    def _():
        ks[0] = jnp.zeros((SPAN, LANES), F32)
        vs[0, 0] = jnp.zeros((SPAN, LANES), F32)
        vs[1, 0] = jnp.zeros((SPAN, LANES), F32)

    @pl.when(s > 0)
    def _():
        ks[0] = ks[1]
        vs[0, 0] = vs[0, 1]
        vs[1, 0] = vs[1, 1]

    for r in range(nres):
        rows = pl.ds(r, ATTN_BLOCK, stride=nres)
        dst = slice(r * ATTN_BLOCK, (r + 1) * ATTN_BLOCK)
        q = q_ref[rows, :]
        v = v_ref[rows, :]
        qs[0, dst, :] = jnp.where(first_half, q, 0.0)
        qs[1, dst, :] = jnp.where(first_half, 0.0, q)
        ks[1, dst, :] = k_ref[rows, :]
        vs[0, 1, dst, :] = jnp.where(first_half, v, 1.0)
        vs[1, 1, dst, :] = jnp.where(first_half, 1.0, v)

    iq = lax.broadcasted_iota(jnp.int32, (ATTN_BLOCK, 2 * ATTN_BLOCK), 0)
    ik = lax.broadcasted_iota(jnp.int32, (ATTN_BLOCK, 2 * ATTN_BLOCK), 1)
    in_cur = ik >= ATTN_BLOCK
    span0 = s == 0

    for di, (window, dil) in enumerate(DILATIONS):
        j = ATTN_BLOCK + _block_order(iq, dil) - (
            jnp.where(in_cur, ATTN_BLOCK, 0) + _block_order(ik & (ATTN_BLOCK - 1), dil))
        valid = (j >= 0) & (j <= window // dil)
        dist = (j * dil).astype(F32)
        dist_all = jnp.where(valid, dist, MASKED)
        dist_cur = jnp.where(valid & in_cur, dist, MASKED)
        for hh in range(2):
            slope = jnp.exp2(-(jnp.zeros((1, 1), F32) + (2 * c + hh + 1).astype(F32)))
            bias_ref[di, hh, 0] = -slope * dist_all
            bias_ref[di, hh, 1] = -slope * dist_cur

    for di, (window, dil) in enumerate(DILATIONS):
        nchunk, crow = _chunks(dil)

        def gather(ref, lead, starts):
            parts = [ref[lead + (pl.ds(st, crow), slice(None))] for st in starts]
            return parts[0] if len(parts) == 1 else jnp.concatenate(parts, axis=0)

        def body(it, carry):
            if dil == 16:
                base = it * ATTN_BLOCK
                cb = 0
                last_off = 0
            elif dil == 4:
                base = (it >> 2) * ATTN_BLOCK
                cb = it & 3
                last_off = 3 * crow
            else:
                base = 0
                cb = it
                last_off = 15 * crow
            stride = {16: 0, 4: 4 * ATTN_BLOCK, 1: ATTN_BLOCK}[dil]
            cur_starts = [pl.multiple_of(base + a * stride + crow * cb, 8) for a in range(nchunk)]
            if dil == 16:
                prev_slot = 0
                prev_starts = cur_starts
                first_in_span = True
            else:
                has_prev = cb > 0
                prev_slot = jnp.where(has_prev, 1, 0)
                prev_off = jnp.where(has_prev, crow * (cb - 1), last_off)
                prev_starts = [pl.multiple_of(base + a * stride + prev_off, 8) for a in range(nchunk)]
                first_in_span = jnp.logical_not(has_prev)
            kb = jnp.concatenate([gather(ks, (prev_slot,), prev_starts), gather(ks, (1,), cur_starts)],
                                 axis=0).astype(BF16)
            variant = jnp.where(jnp.logical_and(span0, first_in_span), 1, 0)
            for hh in range(2):
                qh = gather(qs, (hh,), cur_starts).astype(BF16)
                vh = jnp.concatenate([gather(vs, (hh, prev_slot), prev_starts), gather(vs, (hh, 1), cur_starts)],
                                     axis=0).astype(BF16)
                sc = _dot_nt(qh, kb) + bias_ref[di, hh, variant]
                m = jnp.max(sc, axis=1, keepdims=True)
                p = jnp.exp(sc - m)
                r_new = jnp.dot(p.astype(BF16), vh, preferred_element_type=F32)
                m_new = jnp.broadcast_to(m, (ATTN_BLOCK, LANES))
                for a, st in enumerate(cur_starts):
                    r_st[di, hh, pl.ds(st, crow), :] = r_new[a * crow:(a + 1) * crow, :]
                    m_st[di, hh, pl.ds(st, crow), :] = m_new[a * crow:(a + 1) * crow, :]
            return carry

        lax.fori_loop(0, nres, body, 0, unroll=unroll)

    ndil = len(DILATIONS)
    for r in range(nres):
        src = slice(r * ATTN_BLOCK, (r + 1) * ATTN_BLOCK)
        outs = []
        for hh in range(2):
            ms = [m_st[di, hh, src, :] for di in range(ndil)]
            m_tot = functools.reduce(jnp.maximum, ms)
            tot = sum(r_st[di, hh, src, :] * jnp.exp(ms[di] - m_tot) for di in range(ndil))
            outs.append(tot / pltpu.roll(tot, HEAD_DIM, axis=1))
        o_ref[pl.ds(r, ATTN_BLOCK, stride=nres), :] = jnp.where(first_half, outs[0], outs[1])


def _attn_prompt(q4, k4, v4, unroll=8):
    npair, B, S, _ = q4.shape
    ndil = len(DILATIONS)
    spec = pl.BlockSpec((None, None, SPAN, LANES), lambda b, c, s: (c, b, s, 0))
    return pl.pallas_call(
        functools.partial(_attn_p_kernel, unroll=unroll),
        grid=(B, npair, S // SPAN),
        in_specs=[spec, spec, spec],
        out_specs=spec,
        out_shape=jax.ShapeDtypeStruct(q4.shape, F32),
        scratch_shapes=[pltpu.VMEM((2, SPAN, LANES), F32), pltpu.VMEM((2, SPAN, LANES), F32),
                        pltpu.VMEM((2, 2, SPAN, LANES), F32),
                        pltpu.VMEM((ndil, 2, SPAN, LANES), F32), pltpu.VMEM((ndil, 2, SPAN, LANES), F32),
                        pltpu.VMEM((ndil, 2, 2, ATTN_BLOCK, 2 * ATTN_BLOCK), F32)],
        compiler_params=_cparams(("arbitrary", "arbitrary", "arbitrary")),
        name="attn_prompt",
    )(q4, k4, v4)


def _attn_s_kernel(q_ref, kn_ref, vn_ref, kc_ref, vc_ref, o_ref, dist_ref, mult_ref, *, n_new):
    W = kc_ref.shape[2]
    rows = q_ref.shape[1]

    @pl.when(pl.program_id(0) == 0)
    def _():
        i = lax.broadcasted_iota(jnp.int32, (rows, W), 0)
        pos = lax.broadcasted_iota(jnp.int32, (rows, W), 1)
        d = W + i - pos
        cnt = jnp.zeros((rows, W), F32)
        for window, dil in DILATIONS:
            cnt = cnt + jnp.where(((d & (dil - 1)) == 0) & (d <= window), 1.0, 0.0)
        dist_ref[...] = d.astype(F32)
        mult_ref[...] = cnt

    i = lax.broadcasted_iota(jnp.int32, (rows, rows), 0)
    i2 = lax.broadcasted_iota(jnp.int32, (rows, rows), 1)
    dn = i - i2
    mult_n = jnp.zeros((rows, rows), F32)
    for window, dil in DILATIONS:
        ok = ((dn & (dil - 1)) == 0) & (dn >= 0) & (dn <= window) & (i2 < n_new)
        mult_n = mult_n + jnp.where(ok, 1.0, 0.0)
    dist_n = dn.astype(F32)
    dist_c = dist_ref[...]
    mult_c = mult_ref[...]
    for h in range(N_HEADS):
        slope = 2.0 ** (-(h + 1))
        q = q_ref[h].astype(BF16)
        sc = jnp.dot(q, kc_ref[h].astype(BF16), preferred_element_type=F32)
        sc = jnp.where(mult_c > 0, sc - slope * dist_c, NEG)
        sn = _dot_nt(q, kn_ref[h].astype(BF16))
        sn = jnp.where(mult_n > 0, sn - slope * dist_n, NEG)
        m = jnp.maximum(jnp.max(sc, axis=1, keepdims=True), jnp.max(sn, axis=1, keepdims=True))
        pc = mult_c * jnp.exp(sc - m)
        pn = mult_n * jnp.exp(sn - m)
        l = jnp.sum(pc, axis=1, keepdims=True) + jnp.sum(pn, axis=1, keepdims=True)
        o = _dot_nt(pc.astype(BF16), vc_ref[h].astype(BF16)) + jnp.dot(
            pn.astype(BF16), vn_ref[h].astype(BF16), preferred_element_type=F32)
        o_ref[h] = o / l


def _attn_sample(q, kn, vn, kc_t, vc_t, li, n_new):
    DB = q.shape[0]
    W = kc_t.shape[-1]
    small = pl.BlockSpec((None, N_HEADS, 8, HEAD_DIM), lambda b: (b, 0, 0, 0))
    cache = pl.BlockSpec((None, None, N_HEADS, HEAD_DIM, W), lambda b: (li, b, 0, 0, 0))
    return pl.pallas_call(
        functools.partial(_attn_s_kernel, n_new=n_new),
        grid=(DB,),
        in_specs=[small, small, small, cache, cache],
        out_specs=small,
        out_shape=jax.ShapeDtypeStruct(q.shape, F32),
        scratch_shapes=[pltpu.VMEM((8, W), F32), pltpu.VMEM((8, W), F32)],
        compiler_params=_cparams(("arbitrary",)),
        name="attn_sample",
    )(q, kn, vn, kc_t, vc_t)


def _group_norm_out(y, z, ng_ref):
    yg = y * _silu(z)
    outs = []
    for g in range(SSD_GROUPS):
        part = yg[:, g * GROUP_W:(g + 1) * GROUP_W]
        outs.append(part * lax.rsqrt(jnp.mean(part * part, axis=-1, keepdims=True) + EPS))
    return jnp.concatenate(outs, axis=1) * ng_ref[...]


def _ssd_p_kernel(xbc_ref, dt_ref, z_ref, cw_ref, cb_ref, dtb_ref, alog_ref, dx_ref, ng_ref, e_ref,
                  y_ref, hl_ref, xe_ref, ht_ref):
    t = pl.program_id(1)
    Q = SSD_CHUNK

    @pl.when(t == 0)
    def _():
        xe_ref[0:8, :] = jnp.zeros((8, D_XBC), F32)
        ht_ref[...] = jnp.zeros_like(ht_ref)

    xe_ref[8:8 + Q, :] = xbc_ref[...]
    xc = _silu(_conv_rows(xe_ref, Q, cw_ref, cb_ref))
    xe_ref[0:8, :] = xe_ref[Q:Q + 8, :]
    xs = xc[:, :D_SSD]
    bm = xc[:, D_SSD:D_SSD + SSD_GROUPS * SSD_N]
    cm = xc[:, D_SSD + SSD_GROUPS * SSD_N:]

    dt = _softplus(dt_ref[...] + dtb_ref[...])
    a_neg = -jnp.exp(alog_ref[...])
    ri = lax.broadcasted_iota(jnp.int32, (Q, Q), 0)
    ci = lax.broadcasted_iota(jnp.int32, (Q, Q), 1)
    causal = ri >= ci
    tri = jnp.where(causal, 1.0, 0.0).astype(BF16)
    cum = _dot3_left(tri, dt * a_neg)
    cum_t = cum.T
    e = e_ref[...]
    cum_x = _dot3_right(cum, e)
    dt_x = _dot3_right(dt, e)
    xdt = xs * dt_x
    ecum = jnp.exp(cum_x)
    cl = cum_x[Q - 1:Q, :]
    w_end = xdt * jnp.exp(cl - cum_x)
    dec = jnp.exp(cl)
    lane = lax.broadcasted_iota(jnp.int32, (Q, LANES), 1)

    y_parts = []
    for g in range(SSD_GROUPS):
        cg = cm[:, g * SSD_N:(g + 1) * SSD_N].astype(BF16)
        bg = bm[:, g * SSD_N:(g + 1) * SSD_N]
        cb = _dot_nt(cg, bg.astype(BF16))
        gl = slice(g * GROUP_W, (g + 1) * GROUP_W)
        h_old = ht_ref[g]
        y_off = jnp.dot(cg, h_old.astype(BF16), preferred_element_type=F32) * ecum[:, gl]
        ht_ref[g] = h_old * dec[:, gl] + jnp.dot(bg.T.astype(BF16), w_end[:, gl].astype(BF16),
                                                  preferred_element_type=F32)
        for pp in range(GROUP_W // LANES):
            pl_ = slice(g * GROUP_W + pp * LANES, g * GROUP_W + (pp + 1) * LANES)
            xdt_pair = xdt[:, pl_].astype(BF16)
            halves = []
            for hh in range(2):
                h = (g * GROUP_W + pp * LANES) // SSD_P + hh
                seg = jnp.broadcast_to(cum[:, h:h + 1], (Q, Q)) - jnp.broadcast_to(cum_t[h:h + 1, :], (Q, Q))
                scores = cb * jnp.exp(jnp.where(causal, seg, NEG))
                halves.append(jnp.dot(scores.astype(BF16), xdt_pair, preferred_element_type=F32))
            y_diag = jnp.where(lane < SSD_P, halves[0], halves[1])
            y_parts.append(y_diag + y_off[:, pp * LANES:(pp + 1) * LANES])
    y = jnp.concatenate(y_parts, axis=1) + dx_ref[...] * xs
    y_ref[...] = _group_norm_out(y, z_ref[...], ng_ref)

    @pl.when(t == pl.num_programs(1) - 1)
    def _():
        for g in range(SSD_GROUPS):
            hl_ref[g * (SSD_HEADS // SSD_GROUPS):(g + 1) * (SSD_HEADS // SSD_GROUPS)] = (
                ht_ref[g].T.reshape(SSD_HEADS // SSD_GROUPS, SSD_P, SSD_N))


def _ssd_prompt(xbc, dt, z, cw, cb, dtb, alog, dx, ng, e):
    B, S, _ = xbc.shape
    Q = SSD_CHUNK
    tile = lambda w: pl.BlockSpec((None, Q, w), lambda b, t: (b, t, 0))
    const = lambda r, c: pl.BlockSpec((r, c), lambda b, t: (0, 0))
    return pl.pallas_call(
        _ssd_p_kernel,
        grid=(B, S // Q),
        in_specs=[tile(D_XBC), tile(LANES), tile(D_SSD), const(CONV_W, D_XBC), const(1, D_XBC), const(1, LANES),
                  const(1, LANES), const(1, D_SSD), const(1, D_SSD), const(LANES, D_SSD)],
        out_specs=[tile(D_SSD), pl.BlockSpec((None, SSD_HEADS, SSD_P, SSD_N), lambda b, t: (b, 0, 0, 0))],
        out_shape=[jax.ShapeDtypeStruct((B, S, D_SSD), F32), jax.ShapeDtypeStruct((B, SSD_HEADS, SSD_P, SSD_N), F32)],
        scratch_shapes=[pltpu.VMEM((Q + 8, D_XBC), F32), pltpu.VMEM((SSD_GROUPS, SSD_N, GROUP_W), F32)],
        compiler_params=_cparams(("arbitrary", "arbitrary")),
        name="ssd_prompt",
    )(xbc, dt, z, cw, cb, dtb, alog, dx, ng, e)


def _ssd_s_kernel(xbc_ref, cs_ref, dt_ref, z_ref, h_ref, cw_ref, cb_ref, dtb_ref, alog_ref, dx_ref, ng_ref, e_ref,
                  *rest, n_new):
    y_ref, hn_ref = rest[-2:]
    R = 8
    row = lax.broadcasted_iota(jnp.int32, (R, LANES), 0)
    real = row < n_new
    xe = jnp.concatenate([cs_ref[...], xbc_ref[...]], axis=0)
    y = cb_ref[...] + xe[8:16, :] * cw_ref[3:4, :]
    for tap in range(CONV_W - 1):
        y = y + xe[5 + tap:13 + tap, :] * cw_ref[tap:tap + 1, :]
    xc = _silu(y)
    xs = xc[:, :D_SSD]
    bm = xc[:, D_SSD:D_SSD + SSD_GROUPS * SSD_N]
    cm = xc[:, D_SSD + SSD_GROUPS * SSD_N:]

    dt = jnp.where(real, _softplus(dt_ref[...] + dtb_ref[...]), 0.0)
    da = dt * (-jnp.exp(alog_ref[...]))
    cum = da
    k = 1
    while k < R:
        cum = cum + jnp.where(row >= k, pltpu.roll(cum, k, axis=0), 0.0)
        k *= 2
    e = e_ref[...]
    cum_x = _dot3_right(cum, e)
    dt_x = _dot3_right(dt, e)
    xdt = xs * dt_x
    ecum = jnp.exp(cum_x)
    cl = cum_x[R - 1:R, :]
    w_end = xdt * jnp.exp(cl - cum_x)
    dec3 = _split3(jnp.exp(cl))
    ones = jnp.ones((R, SSD_N), BF16)
    zrow = jnp.zeros((R - 3, GROUP_W), BF16)

    y_parts = []
    for g in range(SSD_GROUPS):
        gl = slice(g * GROUP_W, (g + 1) * GROUP_W)
        cg = cm[:, g * SSD_N:(g + 1) * SSD_N]
        bg = bm[:, g * SSD_N:(g + 1) * SSD_N]
        hs = slice(g * (SSD_HEADS // SSD_GROUPS), (g + 1) * (SSD_HEADS // SSD_GROUPS))
        h_old = h_ref[hs].reshape(GROUP_W, SSD_N)
        y_g = _dot_nt(cg.astype(BF16), h_old.astype(BF16)) * ecum[:, gl]
        for j in range(n_new):
            cbj = jnp.sum(cg * bg[j:j + 1, :], axis=1, keepdims=True)
            term = cbj * jnp.exp(cum_x[:, gl] - cum_x[j:j + 1, gl]) * xdt[j:j + 1, gl]
            y_g = y_g + jnp.where(row[:, 0:1] >= j, term, 0.0)
        y_parts.append(y_g)
        dec_col = _dot_tn(jnp.concatenate([p[:, gl] for p in dec3] + [zrow], axis=0), ones)
        upd = _dot_tn(w_end[:, gl].astype(BF16), bg.astype(BF16))
        hn_ref[hs] = (h_old * dec_col + upd).reshape(SSD_HEADS // SSD_GROUPS, SSD_P, SSD_N)
    y = jnp.concatenate(y_parts, axis=1) + dx_ref[...] * xs
    y_ref[...] = _group_norm_out(y, z_ref[...], ng_ref)


def _ssd_sample(xbc, cs, dt, z, h_all, li, h_new_prev, cw, cb, dtb, alog, dx, ng, e, n_new):
    DB = xbc.shape[0]
    tile = lambda w: pl.BlockSpec((None, 8, w), lambda b: (b, 0, 0))
    const = lambda r, c: pl.BlockSpec((r, c), lambda b: (0, 0))
    state = pl.BlockSpec((None, None, SSD_HEADS, SSD_P, SSD_N), lambda b: (li, b, 0, 0, 0))
    in_specs = [tile(D_XBC), tile(D_XBC), tile(LANES), tile(D_SSD), state, const(CONV_W, D_XBC), const(1, D_XBC),
                const(1, LANES), const(1, LANES), const(1, D_SSD), const(1, D_SSD), const(LANES, D_SSD)]
    args = [xbc, cs, dt, z, h_all, cw, cb, dtb, alog, dx, ng, e]
    aliases = {}
    if h_new_prev is not None:
        in_specs.append(pl.BlockSpec(memory_space=pl.ANY))
        args.append(h_new_prev)
        aliases = {len(args) - 1: 1}
    return pl.pallas_call(
        functools.partial(_ssd_s_kernel, n_new=n_new),
        grid=(DB,),
        in_specs=in_specs,
        out_specs=[tile(D_SSD), state],
        out_shape=[jax.ShapeDtypeStruct((DB, 8, D_SSD), F32), jax.ShapeDtypeStruct(h_all.shape, F32)],
        input_output_aliases=aliases,
        compiler_params=_cparams(("arbitrary",)),
        name="ssd_sample",
    )(*args)


def _out_kernel(x_ref, ylru_ref, o_ref, gattn_ref, yssd_ref, w_ref, g_ref, out_ref):
    o = jnp.concatenate([o_ref[c] for c in range(D_ATTN // LANES)], axis=1)
    y_attn = o * _silu(gattn_ref[...])
    mix_in = jnp.concatenate([ylru_ref[...], y_attn, yssd_ref[...]], axis=1).astype(BF16)
    mix = jnp.dot(mix_in, w_ref[...], preferred_element_type=F32)
    out_ref[...] = x_ref[...] + mix * lax.rsqrt(jnp.mean(mix * mix, axis=-1, keepdims=True) + EPS) * g_ref[...]


def _out_proj(x2d, y_lru, o4, g_attn, y_ssd, w_bf16, g, tm):
    T = x2d.shape[0]
    row = lambda width: pl.BlockSpec((tm, width), lambda i: (i, 0))
    o_spec = pl.BlockSpec((D_ATTN // LANES, tm, LANES), lambda i: (0, i, 0))
    return pl.pallas_call(
        _out_kernel,
        grid=(T // tm,),
        in_specs=[row(D_MODEL), row(D_LRU), o_spec,
                  row(D_ATTN), row(D_SSD), pl.BlockSpec((D_LRU + D_ATTN + D_SSD, D_MODEL), lambda i: (0, 0)),
                  pl.BlockSpec((1, D_MODEL), lambda i: (0, 0))],
        out_specs=row(D_MODEL),
        out_shape=jax.ShapeDtypeStruct((T, D_MODEL), F32),
        compiler_params=_cparams(("arbitrary",)),
        name="out_proj",
    )(x2d, y_lru, o4, g_attn, y_ssd, w_bf16, g)


def _block_diag(w):
    k, n, _ = w.shape
    return jnp.einsum("kij,kl->kilj", w, jnp.eye(k, dtype=w.dtype)).reshape(k * n, k * n)


def _pad_rows(a, rows):
    return jnp.pad(a, ((0, 0), (0, rows - a.shape[1]), (0, 0)))


def _pairs_to_heads(a4, nb, L):
    return a4.reshape(4, nb, L, 2, HEAD_DIM).transpose(1, 2, 0, 3, 4).reshape(nb, L, N_HEADS, HEAD_DIM)


def kernel(x_prompt, x_sample, state_lru_conv, state_lru_h, cache_attn_k, cache_attn_v, state_ssd_conv, state_ssd_h, pre_norm_g, post_norm_g, w_in, lru_conv_w, lru_conv_b, lru_w_a, lru_b_a, lru_w_x, lru_b_x, lru_lambda, ssd_conv_w, ssd_conv_b, ssd_dt_bias, ssd_a_log, ssd_d, ssd_norm_g, w_out):
    depth = w_in.shape[0]
    B, S, _ = x_prompt.shape
    DB, L, _ = x_sample.shape
    assert S % SPAN == 0 and L <= 8 and L >= CONV_W - 1
    win = min(MAX_WINDOW, S)

    lane_pad = lambda a: jnp.pad(a.reshape(1, -1), ((0, 0), (0, LANES - a.shape[-1])))
    expand = jnp.asarray(np.repeat(np.eye(LANES, SSD_HEADS, dtype=np.float32), SSD_P, axis=1), BF16)
    kc_t = jnp.transpose(cache_attn_k, (0, 1, 3, 4, 2))
    vc_t = jnp.transpose(cache_attn_v, (0, 1, 3, 4, 2))

    yp = x_prompt.reshape(B * S, D_MODEL)
    ys = x_sample.reshape(DB * L, D_MODEL)
    outs_p = []
    outs_s = []
    kt_p = vt_p = ssd_h_s = None
    for li in range(depth):
        w_in_b = jnp.pad(w_in[li], ((0, 0), (0, D_IN_PAD - D_IN))).astype(BF16)
        w_out_b = w_out[li].astype(BF16)
        pre_g = pre_norm_g[li].reshape(1, -1)
        post_g = post_norm_g[li].reshape(1, -1)
        wg = jnp.concatenate([_block_diag(lru_w_a[li]), _block_diag(lru_w_x[li])], axis=1).astype(BF16)
        bg = jnp.concatenate([lru_b_a[li], lru_b_x[li]]).reshape(1, -1)
        lcw, lcb, lam = lru_conv_w[li], lru_conv_b[li].reshape(1, -1), lru_lambda[li].reshape(1, -1)
        scw, scb = ssd_conv_w[li], ssd_conv_b[li].reshape(1, -1)
        dtb, alog = lane_pad(ssd_dt_bias[li]), lane_pad(ssd_a_log[li])
        dx = jnp.repeat(ssd_d[li], SSD_P).reshape(1, -1)
        ng = ssd_norm_g[li].reshape(1, -1)

        xlru3, glru, q4, k4, v4, gattn, z, xbc3, dtr, kt_p, vt_p = _proj(
            yp.reshape(B, S, D_MODEL), pre_g, w_in_b, 256, cache=(li, depth, win, kt_p, vt_p))
        y_lru, lru_h_p = _lru_prompt(xlru3, glru, lcw, lcb, wg, bg, lam, 512)
        o4 = _attn_prompt(q4, k4, v4)
        y_ssd, ssd_h_p = _ssd_prompt(xbc3, dtr, z, scw, scb, dtb, alog, dx, ng, expand)
        yp = _out_proj(yp, y_lru.reshape(B * S, D_LRU), o4.reshape(4, B * S, LANES), gattn.reshape(B * S, D_ATTN),
                       y_ssd.reshape(B * S, D_SSD), w_out_b, post_g, 512)
        outs_p.append((xlru3[:, S - (CONV_W - 1):], lru_h_p.reshape(B, D_LRU), xbc3[:, S - (CONV_W - 1):], ssd_h_p))

        xlru, glru, q4, k4, v4, gattn, z, xbc, dtr = [
            a.reshape(a.shape[:-3] + (DB * L, a.shape[-1]))
            for a in _proj(ys.reshape(1, DB * L, D_MODEL), pre_g, w_in_b, min(256, DB * L))]
        xlru3 = xlru.reshape(DB, L, D_LRU)
        y_lru_tm, lru_h_s = _lru_sample(
            xlru3.transpose(1, 0, 2), glru.reshape(DB, L, D_LRU).transpose(1, 0, 2),
            state_lru_conv[li].transpose(1, 0, 2), state_lru_h[li], lcw, lcb, wg, bg, lam)
        y_lru = y_lru_tm.transpose(1, 0, 2).reshape(DB * L, D_LRU)

        def heads_first(a4):
            a = _pairs_to_heads(a4, DB, L).transpose(0, 2, 1, 3)
            return jnp.pad(a, ((0, 0), (0, 0), (0, 8 - L), (0, 0)))

        o_s = _attn_sample(heads_first(q4), heads_first(k4), heads_first(v4), kc_t, vc_t, li, L)
        o_s = o_s[:, :, :L].transpose(0, 2, 1, 3).reshape(DB * L, 4, LANES).transpose(1, 0, 2)

        xbc3 = xbc.reshape(DB, L, D_XBC)
        cs8 = jnp.pad(state_ssd_conv[li], ((0, 0), (8 - (CONV_W - 1), 0), (0, 0)))
        y_ssd8, ssd_h_s = _ssd_sample(_pad_rows(xbc3, 8), cs8, _pad_rows(dtr.reshape(DB, L, LANES), 8),
                                      _pad_rows(z.reshape(DB, L, D_SSD), 8), state_ssd_h, li, ssd_h_s, scw, scb, dtb,
                                      alog, dx, ng, expand, L)
        y_ssd = y_ssd8[:, :L].reshape(DB * L, D_SSD)
        ys = _out_proj(ys, y_lru, o_s, gattn, y_ssd, w_out_b, post_g, DB * L)
        outs_s.append((xlru3[:, L - (CONV_W - 1):], lru_h_s, _pairs_to_heads(k4, DB, L), _pairs_to_heads(v4, DB, L),
                       xbc3[:, L - (CONV_W - 1):]))

    lru_conv_p, lru_h_p, ssd_conv_p, ssd_h_p = [jnp.stack(c) for c in zip(*outs_p)]
    lru_conv_s, lru_h_s, k_s, v_s, ssd_conv_s = [jnp.stack(c) for c in zip(*outs_s)]
    k_p = kt_p.reshape(depth, B, N_HEADS, HEAD_DIM, win).transpose(0, 1, 4, 2, 3)
    v_p = vt_p.reshape(depth, B, N_HEADS, HEAD_DIM, win).transpose(0, 1, 4, 2, 3)
    return (yp.reshape(B, S, D_MODEL), ys.reshape(DB, L, D_MODEL), lru_conv_p, lru_conv_s, lru_h_p, lru_h_s,
            k_p, k_s, v_p, v_s, ssd_conv_p, ssd_conv_s, ssd_h_p, ssd_h_s)
```

```python
import functools

import numpy as np
import jax
import jax.numpy as jnp
from jax import lax
from jax.experimental import pallas as pl
from jax.experimental.pallas import tpu as pltpu

F32 = jnp.float32
BF16 = jnp.bfloat16

D_MODEL = 1024
CONV_W = 4
EPS = 1e-6
D_LRU = 512
LRU_BLOCKS = 8
LRU_C = 8.0
HEAD_DIM = 64
D_ATTN = 512
N_HEADS = 8
DILATIONS = ((128, 1), (512, 4), (2048, 16))
MAX_WINDOW = 2048
ATTN_BLOCK = 128
D_SSD = 1024
SSD_P = 64
SSD_HEADS = 16
SSD_GROUPS = 2
SSD_N = 128
SSD_CHUNK = 128
D_XBC = D_SSD + 2 * SSD_GROUPS * SSD_N
GROUP_W = D_SSD // SSD_GROUPS

OFF_XLRU, OFF_GLRU, OFF_Q, OFF_K, OFF_V, OFF_GATTN, OFF_Z, OFF_XBC, OFF_DT = (
    0, 512, 1024, 1536, 2048, 2560, 3072, 4096, 5632)
D_IN = 5648
LANES = 128
D_IN_PAD = OFF_DT + LANES

SPAN = 2048
NRES = SPAN // ATTN_BLOCK
SSD_PROMPT_CHUNKS = 2
SSD_SAMPLE_SEQS = 4
ATTN_SAMPLE_SEQS = 2
MASKED = 1e33
NEG = -1e30

VMEM_LIMIT = 56 * 1024 * 1024


def _cparams(sem):
    return pltpu.CompilerParams(dimension_semantics=sem, vmem_limit_bytes=VMEM_LIMIT)


def _silu(x):
    return x * jax.nn.sigmoid(x)


def _softplus(x):
    return jnp.maximum(x, 0.0) + jnp.log1p(jnp.exp(-jnp.abs(x)))


def _split3(x):
    h1 = x.astype(BF16)
    r1 = x - h1.astype(F32)
    h2 = r1.astype(BF16)
    h3 = (r1 - h2.astype(F32)).astype(BF16)
    return h1, h2, h3


def _dot3_right(x, m3_bf16):
    return jnp.dot(jnp.concatenate(_split3(x), axis=1), m3_bf16, preferred_element_type=F32)


def _dot3_left(m_bf16, x):
    return jnp.dot(jnp.concatenate([m_bf16] * 3, axis=1), jnp.concatenate(_split3(x), axis=0),
                   preferred_element_type=F32)


def _dot_nt(a, b):
    return lax.dot_general(a, b, (((1,), (1,)), ((), ())), preferred_element_type=F32)


def _dot_tn(a, b):
    return lax.dot_general(a, b, (((0,), (0,)), ((), ())), preferred_element_type=F32)


def _proj_kernel(x_ref, g_ref, w_ref, *rest, first_win_tile, n_alias):
    rest = rest[n_alias:]
    xlru_ref, glru_ref, q_ref, k_ref, v_ref, gattn_ref, z_ref, xbc_ref, dt_ref = rest[:9]
    x = x_ref[...]
    h = x * lax.rsqrt(jnp.mean(x * x, axis=-1, keepdims=True) + EPS) * g_ref[...]
    hb = h.astype(BF16)

    def mm(off, width):
        return jnp.dot(hb, w_ref[:, off:off + width], preferred_element_type=F32)

    xlru_ref[...] = mm(OFF_XLRU, D_LRU)
    glru_ref[...] = mm(OFF_GLRU, D_LRU)
    q = mm(OFF_Q, D_ATTN) * (HEAD_DIM ** -0.5)
    k = mm(OFF_K, D_ATTN)
    v = mm(OFF_V, D_ATTN)
    for c in range(D_ATTN // LANES):
        q_ref[c] = q[:, c * LANES:(c + 1) * LANES]
        k_ref[c] = k[:, c * LANES:(c + 1) * LANES]
        v_ref[c] = v[:, c * LANES:(c + 1) * LANES]
    if len(rest) > 9:
        kt_ref, vt_ref = rest[9:]

        @pl.when(pl.program_id(1) >= first_win_tile)
        def _():
            kt_ref[...] = k.T
            vt_ref[...] = v.T

    gattn_ref[...] = mm(OFF_GATTN, D_ATTN)
    z_ref[...] = mm(OFF_Z, D_SSD)
    xbc_ref[...] = mm(OFF_XBC, D_XBC)
    dt_ref[...] = mm(OFF_DT, LANES)


def _proj(x3, g, w_bf16, tm, cache=None):
    B, S, _ = x3.shape
    row = lambda width: pl.BlockSpec((None, tm, width), lambda b, t: (b, t, 0))
    pair = pl.BlockSpec((D_ATTN // LANES, None, tm, LANES), lambda b, t: (0, b, t, 0))
    pair_shape = (4, B, S, LANES)
    sds = lambda *s: jax.ShapeDtypeStruct(s, F32)
    in_specs = [row(D_MODEL), pl.BlockSpec((1, D_MODEL), lambda b, t: (0, 0)),
                pl.BlockSpec((D_MODEL, D_IN_PAD), lambda b, t: (0, 0))]
    args = [x3, g, w_bf16]
    out_specs = [row(D_LRU), row(D_LRU), pair, pair, pair, row(D_ATTN), row(D_SSD), row(D_XBC), row(LANES)]
    out_shape = [sds(B, S, D_LRU), sds(B, S, D_LRU), sds(*pair_shape), sds(*pair_shape), sds(*pair_shape),
                 sds(B, S, D_ATTN), sds(B, S, D_SSD), sds(B, S, D_XBC), sds(B, S, LANES)]
    first_win_tile, n_alias, aliases = 0, 0, {}
    if cache is not None:
        li, depth, win, prev_kt, prev_vt = cache
        first_win_tile = (S - win) // tm
        cspec = pl.BlockSpec((None, None, D_ATTN, tm),
                             lambda b, t: (li, b, 0, jnp.maximum(t - first_win_tile, 0)))
        out_specs += [cspec, cspec]
        out_shape += [sds(depth, B, D_ATTN, win), sds(depth, B, D_ATTN, win)]
        if prev_kt is not None:
            n_alias = 2
            in_specs += [pl.BlockSpec(memory_space=pl.ANY)] * 2
            args += [prev_kt, prev_vt]
            aliases = {3: 9, 4: 10}
    return pl.pallas_call(
        functools.partial(_proj_kernel, first_win_tile=first_win_tile, n_alias=n_alias),
        grid=(B, S // tm),
        in_specs=in_specs,
        out_specs=out_specs,
        out_shape=out_shape,
        input_output_aliases=aliases,
        compiler_params=_cparams(("arbitrary", "arbitrary")),
        name="in_proj",
    )(*args)


def _lru_gates(xc, wg_ref, bg_ref, lam_ref):
    pre = jnp.dot(xc.astype(BF16), wg_ref[...], preferred_element_type=F32) + bg_ref[...]
    r = jax.nn.sigmoid(pre[:, :D_LRU])
    i = jax.nn.sigmoid(pre[:, D_LRU:])
    log_a = (-LRU_C) * r * _softplus(-lam_ref[...])
    a = jnp.exp(log_a)
    t = jnp.tanh(log_a)
    return a, jnp.sqrt(-2.0 * t / (1.0 - t)) * (i * xc)


def _scan_rows(a, b, h0):
    n, width = a.shape
    a = a.reshape(n // 8, 8, width)
    b = b.reshape(n // 8, 8, width)
    row = lax.broadcasted_iota(jnp.int32, a.shape, 1)
    k = 1
    while k < 8:
        keep = row >= k
        a_sh = jnp.where(keep, pltpu.roll(a, k, axis=1), 1.0)
        b_sh = jnp.where(keep, pltpu.roll(b, k, axis=1), 0.0)
        b = a * b_sh + b
        a = a * a_sh
        k *= 2
    blocks = []
    h = h0
    for j in range(n // 8):
        blk = b[j] + a[j] * h
        blocks.append(blk)
        h = blk[7:8, :]
    return jnp.concatenate(blocks, axis=0), h


def _conv_rows(xe_ref, n, w_ref, b_ref):
    y = b_ref[...] + xe_ref[8:8 + n, :] * w_ref[3:4, :]
    for tap in range(CONV_W - 1):
        y = y + xe_ref[5 + tap:5 + tap + n, :] * w_ref[tap:tap + 1, :]
    return y


def _lru_p_kernel(x_ref, g_ref, cw_ref, cb_ref, wg_ref, bg_ref, lam_ref, y_ref, hl_ref, xe_ref, h_ref):
    t = pl.program_id(1)
    n = x_ref.shape[0]

    @pl.when(t == 0)
    def _():
        xe_ref[0:8, :] = jnp.zeros((8, D_LRU), F32)
        h_ref[...] = jnp.zeros_like(h_ref)

    xe_ref[8:8 + n, :] = x_ref[...]
    xc = _conv_rows(xe_ref, n, cw_ref, cb_ref)
    xe_ref[0:8, :] = xe_ref[n:n + 8, :]
    a, b = _lru_gates(xc, wg_ref, bg_ref, lam_ref)
    hs, h_last = _scan_rows(a, b, h_ref[0:1, :])
    h_ref[...] = jnp.broadcast_to(h_last, h_ref.shape)
    hl_ref[...] = h_last
    y_ref[...] = (hs * _silu(g_ref[...])).astype(y_ref.dtype)


def _lru_prompt(x, g, cw, cb, wg, bg, lam, tt):
    B, S, _ = x.shape
    tile = pl.BlockSpec((None, tt, D_LRU), lambda b, t: (b, t, 0))
    const = lambda r, c: pl.BlockSpec((r, c), lambda b, t: (0, 0))
    return pl.pallas_call(
        _lru_p_kernel,
        grid=(B, S // tt),
        in_specs=[tile, tile, const(CONV_W, D_LRU), const(1, D_LRU), const(D_LRU, 2 * D_LRU), const(1, 2 * D_LRU),
                  const(1, D_LRU)],
        out_specs=[tile, pl.BlockSpec((None, 1, D_LRU), lambda b, t: (b, 0, 0))],
        out_shape=[jax.ShapeDtypeStruct((B, S, D_LRU), BF16), jax.ShapeDtypeStruct((B, 1, D_LRU), F32)],
        scratch_shapes=[pltpu.VMEM((tt + 8, D_LRU), F32), pltpu.VMEM((8, D_LRU), F32)],
        compiler_params=_cparams(("arbitrary", "arbitrary")),
        name="lru_prompt",
    )(x, g, cw, cb, wg, bg, lam)


def _lru_s_kernel(x_ref, g_ref, cs_ref, h0_ref, cw_ref, cb_ref, wg_ref, bg_ref, lam_ref, y_ref, hl_ref):
    L = x_ref.shape[0]
    db = x_ref.shape[1]
    xs = [cs_ref[i] for i in range(CONV_W - 1)] + [x_ref[i] for i in range(L)]
    xc = []
    for t in range(L):
        y = cb_ref[...] + xs[t] * cw_ref[0:1, :]
        for tap in range(1, CONV_W):
            y = y + xs[t + tap] * cw_ref[tap:tap + 1, :]
        xc.append(y)
    a, b = _lru_gates(jnp.concatenate(xc, axis=0), wg_ref, bg_ref, lam_ref)
    h = h0_ref[...]
    for t in range(L):
        h = a[t * db:(t + 1) * db] * h + b[t * db:(t + 1) * db]
        y_ref[t] = h * _silu(g_ref[t])
    hl_ref[...] = h


def _lru_sample(x_tm, g_tm, cs_tm, h0, cw, cb, wg, bg, lam):
    L, DB, _ = x_tm.shape
    return pl.pallas_call(
        _lru_s_kernel,
        out_shape=[jax.ShapeDtypeStruct((L, DB, D_LRU), F32), jax.ShapeDtypeStruct((DB, D_LRU), F32)],
        compiler_params=pltpu.CompilerParams(vmem_limit_bytes=VMEM_LIMIT),
        name="lru_sample",
    )(x_tm, g_tm, cs_tm, h0, cw, cb, wg, bg, lam)


def _block_order(rho, dil):
    if dil == 16:
        return rho
    if dil == 4:
        return 4 * (rho & 31) + (rho >> 5)
    return 16 * (rho & 7) + (rho >> 3)


def _chunks(dil):
    return {16: (1, 128), 4: (4, 32), 1: (16, 8)}[dil]


def _attn_p_kernel(q_ref, k_ref, v_ref, o_ref, qs, ks, vs, r_st, m_st, bias_ref, *, unroll):
    c = pl.program_id(1)
    s = pl.program_id(2)
    nres = SPAN // ATTN_BLOCK
    lane_q = lax.broadcasted_iota(jnp.int32, (ATTN_BLOCK, LANES), 1)
    first_half = lane_q < HEAD_DIM

    cur = s & 1
    prev = 1 - cur

    @pl.when(s == 0)
    def _():
        ks[prev] = jnp.zeros((SPAN, LANES), F32)
        vs[0, prev] = jnp.zeros((SPAN, LANES), F32)
        vs[1, prev] = jnp.zeros((SPAN, LANES), F32)

    for r in range(nres):
        rows = pl.ds(r, ATTN_BLOCK, stride=nres)
        dst = slice(r * ATTN_BLOCK, (r + 1) * ATTN_BLOCK)
        q = q_ref[rows, :]
        v = v_ref[rows, :]
        qs[0, dst, :] = jnp.where(first_half, q, 0.0)
        qs[1, dst, :] = jnp.where(first_half, 0.0, q)
        ks[cur, dst, :] = k_ref[rows, :]
        vs[0, cur, dst, :] = jnp.where(first_half, v, 1.0)
        vs[1, cur, dst, :] = jnp.where(first_half, 1.0, v)

    span0 = s == 0

    @pl.when(span0)
    def _():
        iq = lax.broadcasted_iota(jnp.int32, (ATTN_BLOCK, 2 * ATTN_BLOCK), 0)
        ik = lax.broadcasted_iota(jnp.int32, (ATTN_BLOCK, 2 * ATTN_BLOCK), 1)
        in_cur = ik >= ATTN_BLOCK
        for di, (window, dil) in enumerate(DILATIONS):
            j = ATTN_BLOCK + _block_order(iq, dil) - (
                jnp.where(in_cur, ATTN_BLOCK, 0) + _block_order(ik & (ATTN_BLOCK - 1), dil))
            valid = (j >= 0) & (j <= window // dil)
            dist = (j * dil).astype(F32)
            dist_all = jnp.where(valid, dist, MASKED)
            dist_cur = jnp.where(valid & in_cur, dist, MASKED)
            for hh in range(2):
                slope = jnp.exp2(-(jnp.zeros((1, 1), F32) + (2 * c + hh + 1).astype(F32)))
                bias_ref[di, hh, 0] = -slope * dist_all
                bias_ref[di, hh, 1] = -slope * dist_cur

    for di, (window, dil) in enumerate(DILATIONS):
        nchunk, crow = _chunks(dil)

        def gather(ref, lead, starts):
            parts = [ref[lead + (pl.ds(st, crow), slice(None))] for st in starts]
            return parts[0] if len(parts) == 1 else jnp.concatenate(parts, axis=0)

        def body(it, carry):
            if dil == 16:
                base = it * ATTN_BLOCK
                cb = 0
                last_off = 0
            elif dil == 4:
                base = (it >> 2) * ATTN_BLOCK
                cb = it & 3
                last_off = 3 * crow
            else:
                base = 0
                cb = it
                last_off = 15 * crow
            stride = {16: 0, 4: 4 * ATTN_BLOCK, 1: ATTN_BLOCK}[dil]
            cur_starts = [pl.multiple_of(base + a * stride + crow * cb, 8) for a in range(nchunk)]
            if dil == 16:
                prev_slot = prev
                prev_starts = cur_starts
                first_in_span = True
            else:
                has_prev = cb > 0
                prev_slot = jnp.where(has_prev, cur, prev)
                prev_off = jnp.where(has_prev, crow * (cb - 1), last_off)
                prev_starts = [pl.multiple_of(base + a * stride + prev_off, 8) for a in range(nchunk)]
                first_in_span = jnp.logical_not(has_prev)
            kb = jnp.concatenate([gather(ks, (prev_slot,), prev_starts), gather(ks, (cur,), cur_starts)],
                                 axis=0).astype(BF16)
            variant = jnp.where(jnp.logical_and(span0, first_in_span), 1, 0)
            for hh in range(2):
                qh = gather(qs, (hh,), cur_starts).astype(BF16)
                vh = jnp.concatenate([gather(vs, (hh, prev_slot), prev_starts), gather(vs, (hh, cur), cur_starts)],
                                     axis=0).astype(BF16)
                sc = _dot_nt(qh, kb) + bias_ref[di, hh, variant]
                m = jnp.max(sc, axis=1, keepdims=True)
                p = jnp.exp(sc - m)
                r_new = jnp.dot(p.astype(BF16), vh, preferred_element_type=F32)
                m_new = jnp.broadcast_to(m, (ATTN_BLOCK, LANES))
                for a, st in enumerate(cur_starts):
                    r_st[di, hh, pl.ds(st, crow), :] = r_new[a * crow:(a + 1) * crow, :]
                    m_st[di, hh, pl.ds(st, crow), :] = m_new[a * crow:(a + 1) * crow, :]
            return carry

        lax.fori_loop(0, nres, body, 0, unroll=unroll)

    ndil = len(DILATIONS)
    for r in range(nres):
        src = slice(r * ATTN_BLOCK, (r + 1) * ATTN_BLOCK)
        outs = []
        for hh in range(2):
            ms = [m_st[di, hh, src, :] for di in range(ndil)]
            m_tot = functools.reduce(jnp.maximum, ms)
            tot = sum(r_st[di, hh, src, :] * jnp.exp(ms[di] - m_tot) for di in range(ndil))
            outs.append(tot / pltpu.roll(tot, HEAD_DIM, axis=1))
        o_ref[pl.ds(r, ATTN_BLOCK, stride=nres), :] = jnp.where(first_half, outs[0], outs[1])


def _attn_prompt(q4, k4, v4, unroll=8):
    npair, B, S, _ = q4.shape
    ndil = len(DILATIONS)
    spec = pl.BlockSpec((None, None, SPAN, LANES), lambda b, c, s: (c, b, s, 0))
    return pl.pallas_call(
        functools.partial(_attn_p_kernel, unroll=unroll),
        grid=(B, npair, S // SPAN),
        in_specs=[spec, spec, spec],
        out_specs=spec,
        out_shape=jax.ShapeDtypeStruct(q4.shape, F32),
        scratch_shapes=[pltpu.VMEM((2, SPAN, LANES), F32), pltpu.VMEM((2, SPAN, LANES), F32),
                        pltpu.VMEM((2, 2, SPAN, LANES), F32),
                        pltpu.VMEM((ndil, 2, SPAN, LANES), F32), pltpu.VMEM((ndil, 2, SPAN, LANES), F32),
                        pltpu.VMEM((ndil, 2, 2, ATTN_BLOCK, 2 * ATTN_BLOCK), F32)],
        compiler_params=_cparams(("arbitrary", "arbitrary", "arbitrary")),
        name="attn_prompt",
    )(q4, k4, v4)


def _attn_s_kernel(q_ref, kn_ref, vn_ref, kc_ref, vc_ref, o_ref, dist_ref, mult_ref, *, n_new):
    W = kc_ref.shape[-1]
    rows = q_ref.shape[-2]

    @pl.when(pl.program_id(0) == 0)
    def _():
        i = lax.broadcasted_iota(jnp.int32, (rows, W), 0)
        pos = lax.broadcasted_iota(jnp.int32, (rows, W), 1)
        d = W + i - pos
        cnt = jnp.zeros((rows, W), F32)
        for window, dil in DILATIONS:
            cnt = cnt + jnp.where(((d & (dil - 1)) == 0) & (d <= window), 1.0, 0.0)
        dist_ref[...] = d.astype(F32)
        mult_ref[...] = cnt

    i = lax.broadcasted_iota(jnp.int32, (rows, rows), 0)
    i2 = lax.broadcasted_iota(jnp.int32, (rows, rows), 1)
    dn = i - i2
    mult_n = jnp.zeros((rows, rows), F32)
    for window, dil in DILATIONS:
        ok = ((dn & (dil - 1)) == 0) & (dn >= 0) & (dn <= window) & (i2 < n_new)
        mult_n = mult_n + jnp.where(ok, 1.0, 0.0)
    dist_n = dn.astype(F32)
    dist_c = dist_ref[...]
    mult_c = mult_ref[...]
    for b, h in [(b, h) for b in range(q_ref.shape[0]) for h in range(N_HEADS)]:
        slope = 2.0 ** (-(h + 1))
        q = q_ref[b, h].astype(BF16)
        sc = jnp.dot(q, kc_ref[b, h].astype(BF16), preferred_element_type=F32)
        sc = jnp.where(mult_c > 0, sc - slope * dist_c, NEG)
        sn = _dot_nt(q, kn_ref[b, h].astype(BF16))
        sn = jnp.where(mult_n > 0, sn - slope * dist_n, NEG)
        m = jnp.maximum(jnp.max(sc, axis=1, keepdims=True), jnp.max(sn, axis=1, keepdims=True))
        pc = mult_c * jnp.exp(sc - m)
        pn = mult_n * jnp.exp(sn - m)
        l = jnp.sum(pc, axis=1, keepdims=True) + jnp.sum(pn, axis=1, keepdims=True)
        o = _dot_nt(pc.astype(BF16), vc_ref[b, h].astype(BF16)) + jnp.dot(
            pn.astype(BF16), vn_ref[b, h].astype(BF16), preferred_element_type=F32)
        o_ref[b, h] = o / l


def _attn_sample(q, kn, vn, kc_t, vc_t, li, n_new):
    DB = q.shape[0]
    W = kc_t.shape[-1]
    bb = ATTN_SAMPLE_SEQS if DB % ATTN_SAMPLE_SEQS == 0 else 1
    small = pl.BlockSpec((bb, N_HEADS, 8, HEAD_DIM), lambda b: (b, 0, 0, 0))
    cache = pl.BlockSpec((None, bb, N_HEADS, HEAD_DIM, W), lambda b: (li, b, 0, 0, 0))
    return pl.pallas_call(
        functools.partial(_attn_s_kernel, n_new=n_new),
        grid=(DB // bb,),
        in_specs=[small, small, small, cache, cache],
        out_specs=small,
        out_shape=jax.ShapeDtypeStruct(q.shape, F32),
        scratch_shapes=[pltpu.VMEM((8, W), F32), pltpu.VMEM((8, W), F32)],
        compiler_params=_cparams(("arbitrary",)),
        name="attn_sample",
    )(q, kn, vn, kc_t, vc_t)


def _group_norm_out(y, z, ng_ref):
    yg = y * _silu(z)
    outs = []
    for g in range(SSD_GROUPS):
        part = yg[:, g * GROUP_W:(g + 1) * GROUP_W]
        outs.append(part * lax.rsqrt(jnp.mean(part * part, axis=-1, keepdims=True) + EPS))
    return jnp.concatenate(outs, axis=1) * ng_ref[...]


def _ssd_p_kernel(xbc_ref, dt_ref, z_ref, cw_ref, cb_ref, dtb_ref, alog_ref, dx_ref, ng_ref, e_ref,
                  y_ref, hl_ref, xe_ref, ht_ref):
    t = pl.program_id(1)
    Q = SSD_CHUNK
    n = xbc_ref.shape[0]

    @pl.when(t == 0)
    def _():
        xe_ref[0:8, :] = jnp.zeros((8, D_XBC), F32)
        ht_ref[...] = jnp.zeros_like(ht_ref)

    xe_ref[8:8 + n, :] = xbc_ref[...]
    xc_all = _silu(_conv_rows(xe_ref, n, cw_ref, cb_ref))
    xe_ref[0:8, :] = xe_ref[n:n + 8, :]
    dt_all = _softplus(dt_ref[...] + dtb_ref[...])
    a_neg = -jnp.exp(alog_ref[...])
    ri = lax.broadcasted_iota(jnp.int32, (Q, Q), 0)
    ci = lax.broadcasted_iota(jnp.int32, (Q, Q), 1)
    causal = ri >= ci
    tri = jnp.where(causal, 1.0, 0.0).astype(BF16)
    lane = lax.broadcasted_iota(jnp.int32, (Q, LANES), 1)
    for ck in range(n // Q):
        rows = slice(ck * Q, (ck + 1) * Q)
        y = _ssd_chunk(xc_all[rows, :], dt_all[rows, :], a_neg, causal, tri, lane, dx_ref, e_ref, ht_ref)
        y_ref[rows, :] = _group_norm_out(y, z_ref[rows, :], ng_ref).astype(y_ref.dtype)

    @pl.when(t == pl.num_programs(1) - 1)
    def _():
        for g in range(SSD_GROUPS):
            hl_ref[g * (SSD_HEADS // SSD_GROUPS):(g + 1) * (SSD_HEADS // SSD_GROUPS)] = (
                ht_ref[g].T.reshape(SSD_HEADS // SSD_GROUPS, SSD_P, SSD_N))


def _ssd_chunk(xc, dt, a_neg, causal, tri, lane, dx_ref, e_ref, ht_ref):
    Q = SSD_CHUNK
    xs = xc[:, :D_SSD]
    bm = xc[:, D_SSD:D_SSD + SSD_GROUPS * SSD_N]
    cm = xc[:, D_SSD + SSD_GROUPS * SSD_N:]
    cum = _dot3_left(tri, dt * a_neg)
    cum_t = cum.T
    both_x = _dot3_right(jnp.concatenate([cum, dt], axis=0), e_ref[...])
    cum_x = both_x[:Q]
    dt_x = both_x[Q:]
    xdt = xs * dt_x
    ecum = jnp.exp(cum_x)
    cl = cum_x[Q - 1:Q, :]
    w_end = xdt * jnp.exp(cl - cum_x)
    dec = jnp.exp(cl)

    y_parts = []
    for g in range(SSD_GROUPS):
        cg = cm[:, g * SSD_N:(g + 1) * SSD_N].astype(BF16)
        bg = bm[:, g * SSD_N:(g + 1) * SSD_N]
        cb = _dot_nt(cg, bg.astype(BF16))
        gl = slice(g * GROUP_W, (g + 1) * GROUP_W)
        h_old = ht_ref[g]
        y_off = jnp.dot(cg, h_old.astype(BF16), preferred_element_type=F32) * ecum[:, gl]
        ht_ref[g] = h_old * dec[:, gl] + jnp.dot(bg.T.astype(BF16), w_end[:, gl].astype(BF16),
                                                  preferred_element_type=F32)
        for pp in range(GROUP_W // LANES):
            pl_ = slice(g * GROUP_W + pp * LANES, g * GROUP_W + (pp + 1) * LANES)
            xdt_pair = xdt[:, pl_].astype(BF16)
            halves = []
            for hh in range(2):
                h = (g * GROUP_W + pp * LANES) // SSD_P + hh
                seg = jnp.broadcast_to(cum[:, h:h + 1], (Q, Q)) - jnp.broadcast_to(cum_t[h:h + 1, :], (Q, Q))
                scores = cb * jnp.exp(jnp.where(causal, seg, NEG))
                halves.append(jnp.dot(scores.astype(BF16), xdt_pair, preferred_element_type=F32))
            y_diag = jnp.where(lane < SSD_P, halves[0], halves[1])
            y_parts.append(y_diag + y_off[:, pp * LANES:(pp + 1) * LANES])
    return jnp.concatenate(y_parts, axis=1) + dx_ref[...] * xs


def _ssd_prompt(xbc, dt, z, cw, cb, dtb, alog, dx, ng, e):
    B, S, _ = xbc.shape
    Q = SSD_PROMPT_CHUNKS * SSD_CHUNK
    tile = lambda w: pl.BlockSpec((None, Q, w), lambda b, t: (b, t, 0))
    const = lambda r, c: pl.BlockSpec((r, c), lambda b, t: (0, 0))
    return pl.pallas_call(
        _ssd_p_kernel,
        grid=(B, S // Q),
        in_specs=[tile(D_XBC), tile(LANES), tile(D_SSD), const(CONV_W, D_XBC), const(1, D_XBC), const(1, LANES),
                  const(1, LANES), const(1, D_SSD), const(1, D_SSD), const(3 * LANES, D_SSD)],
        out_specs=[tile(D_SSD), pl.BlockSpec((None, SSD_HEADS, SSD_P, SSD_N), lambda b, t: (b, 0, 0, 0))],
        out_shape=[jax.ShapeDtypeStruct((B, S, D_SSD), BF16), jax.ShapeDtypeStruct((B, SSD_HEADS, SSD_P, SSD_N), F32)],
        scratch_shapes=[pltpu.VMEM((Q + 8, D_XBC), F32), pltpu.VMEM((SSD_GROUPS, SSD_N, GROUP_W), F32)],
        compiler_params=_cparams(("arbitrary", "arbitrary")),
        name="ssd_prompt",
    )(xbc, dt, z, cw, cb, dtb, alog, dx, ng, e)


def _ssd_s_kernel(xbc_ref, cs_ref, dt_ref, z_ref, h_ref, cw_ref, cb_ref, dtb_ref, alog_ref, dx_ref, ng_ref, e_ref,
                  *rest, n_new):
    y_ref, hn_ref = rest[-2:]
    for i in range(xbc_ref.shape[0]):
        _ssd_s_one(xbc_ref.at[i], cs_ref.at[i], dt_ref.at[i], z_ref.at[i], h_ref.at[i], cw_ref, cb_ref, dtb_ref,
                   alog_ref, dx_ref, ng_ref, e_ref, y_ref.at[i], hn_ref.at[i], n_new)


def _ssd_s_one(xbc_ref, cs_ref, dt_ref, z_ref, h_ref, cw_ref, cb_ref, dtb_ref, alog_ref, dx_ref, ng_ref, e_ref,
               y_ref, hn_ref, n_new):
    R = 8
    row = lax.broadcasted_iota(jnp.int32, (R, LANES), 0)
    real = row < n_new
    xe = jnp.concatenate([cs_ref[...], xbc_ref[...]], axis=0)
    y = cb_ref[...] + xe[8:16, :] * cw_ref[3:4, :]
    for tap in range(CONV_W - 1):
        y = y + xe[5 + tap:13 + tap, :] * cw_ref[tap:tap + 1, :]
    xc = _silu(y)
    xs = xc[:, :D_SSD]
    bm = xc[:, D_SSD:D_SSD + SSD_GROUPS * SSD_N]
    cm = xc[:, D_SSD + SSD_GROUPS * SSD_N:]

    dt = jnp.where(real, _softplus(dt_ref[...] + dtb_ref[...]), 0.0)
    da = dt * (-jnp.exp(alog_ref[...]))
    cum = da
    k = 1
    while k < R:
        cum = cum + jnp.where(row >= k, pltpu.roll(cum, k, axis=0), 0.0)
        k *= 2
    e = e_ref[...]
    cum_x = _dot3_right(cum, e)
    dt_x = _dot3_right(dt, e)
    xdt = xs * dt_x
    ecum = jnp.exp(cum_x)
    cl = cum_x[R - 1:R, :]
    w_end = xdt * jnp.exp(cl - cum_x)
    dec3 = _split3(jnp.exp(cl))
    ones = jnp.ones((R, SSD_N), BF16)
    zrow = jnp.zeros((R - 3, GROUP_W), BF16)

    y_parts = []
    for g in range(SSD_GROUPS):
        gl = slice(g * GROUP_W, (g + 1) * GROUP_W)
        cg = cm[:, g * SSD_N:(g + 1) * SSD_N]
        bg = bm[:, g * SSD_N:(g + 1) * SSD_N]
        hs = slice(g * (SSD_HEADS // SSD_GROUPS), (g + 1) * (SSD_HEADS // SSD_GROUPS))
        h_old = h_ref[hs].reshape(GROUP_W, SSD_N)
        y_g = _dot_nt(cg.astype(BF16), h_old.astype(BF16)) * ecum[:, gl]
        for j in range(n_new):
            cbj = jnp.sum(cg * bg[j:j + 1, :], axis=1, keepdims=True)
            term = cbj * jnp.exp(cum_x[:, gl] - cum_x[j:j + 1, gl]) * xdt[j:j + 1, gl]
            y_g = y_g + jnp.where(row[:, 0:1] >= j, term, 0.0)
        y_parts.append(y_g)
        dec_col = _dot_tn(jnp.concatenate([p[:, gl] for p in dec3] + [zrow], axis=0), ones)
        upd = _dot_tn(w_end[:, gl].astype(BF16), bg.astype(BF16))
        hn_ref[hs] = (h_old * dec_col + upd).reshape(SSD_HEADS // SSD_GROUPS, SSD_P, SSD_N)
    y = jnp.concatenate(y_parts, axis=1) + dx_ref[...] * xs
    y_ref[...] = _group_norm_out(y, z_ref[...], ng_ref)


def _ssd_sample(xbc, cs, dt, z, h_all, li, h_new_prev, cw, cb, dtb, alog, dx, ng, e, n_new):
    DB = xbc.shape[0]
    bb = SSD_SAMPLE_SEQS if DB % SSD_SAMPLE_SEQS == 0 else 1
    tile = lambda w: pl.BlockSpec((bb, 8, w), lambda b: (b, 0, 0))
    const = lambda r, c: pl.BlockSpec((r, c), lambda b: (0, 0))
    state = pl.BlockSpec((None, bb, SSD_HEADS, SSD_P, SSD_N), lambda b: (li, b, 0, 0, 0))
    in_specs = [tile(D_XBC), tile(D_XBC), tile(LANES), tile(D_SSD), state, const(CONV_W, D_XBC), const(1, D_XBC),
                const(1, LANES), const(1, LANES), const(1, D_SSD), const(1, D_SSD), const(3 * LANES, D_SSD)]
    args = [xbc, cs, dt, z, h_all, cw, cb, dtb, alog, dx, ng, e]
    aliases = {}
    if h_new_prev is not None:
        in_specs.append(pl.BlockSpec(memory_space=pl.ANY))
        args.append(h_new_prev)
        aliases = {len(args) - 1: 1}
    return pl.pallas_call(
        functools.partial(_ssd_s_kernel, n_new=n_new),
        grid=(DB // bb,),
        in_specs=in_specs,
        out_specs=[tile(D_SSD), state],
        out_shape=[jax.ShapeDtypeStruct((DB, 8, D_SSD), F32), jax.ShapeDtypeStruct(h_all.shape, F32)],
        input_output_aliases=aliases,
        compiler_params=_cparams(("arbitrary",)),
        name="ssd_sample",
    )(*args)


def _out_kernel(x_ref, ylru_ref, o_ref, gattn_ref, yssd_ref, w_ref, g_ref, out_ref):
    o = jnp.concatenate([o_ref[c] for c in range(D_ATTN // LANES)], axis=1)
    y_attn = o * _silu(gattn_ref[...])
    mix_in = jnp.concatenate([ylru_ref[...].astype(BF16), y_attn.astype(BF16), yssd_ref[...].astype(BF16)], axis=1)
    mix = jnp.dot(mix_in, w_ref[...], preferred_element_type=F32)
    out_ref[...] = x_ref[...] + mix * lax.rsqrt(jnp.mean(mix * mix, axis=-1, keepdims=True) + EPS) * g_ref[...]


def _out_proj(x2d, y_lru, o4, g_attn, y_ssd, w_bf16, g, tm):
    T = x2d.shape[0]
    row = lambda width: pl.BlockSpec((tm, width), lambda i: (i, 0))
    o_spec = pl.BlockSpec((D_ATTN // LANES, tm, LANES), lambda i: (0, i, 0))
    return pl.pallas_call(
        _out_kernel,
        grid=(T // tm,),
        in_specs=[row(D_MODEL), row(D_LRU), o_spec,
                  row(D_ATTN), row(D_SSD), pl.BlockSpec((D_LRU + D_ATTN + D_SSD, D_MODEL), lambda i: (0, 0)),
                  pl.BlockSpec((1, D_MODEL), lambda i: (0, 0))],
        out_specs=row(D_MODEL),
        out_shape=jax.ShapeDtypeStruct((T, D_MODEL), F32),
        compiler_params=_cparams(("arbitrary",)),
        name="out_proj",
    )(x2d, y_lru, o4, g_attn, y_ssd, w_bf16, g)


def _block_diag(w):
    k, n, _ = w.shape
    return jnp.einsum("kij,kl->kilj", w, jnp.eye(k, dtype=w.dtype)).reshape(k * n, k * n)


def _pad_rows(a, rows):
    return jnp.pad(a, ((0, 0), (0, rows - a.shape[1]), (0, 0)))


def _pairs_to_heads(a4, nb, L):
    return a4.reshape(4, nb, L, 2, HEAD_DIM).transpose(1, 2, 0, 3, 4).reshape(nb, L, N_HEADS, HEAD_DIM)


def kernel(x_prompt, x_sample, state_lru_conv, state_lru_h, cache_attn_k, cache_attn_v, state_ssd_conv, state_ssd_h, pre_norm_g, post_norm_g, w_in, lru_conv_w, lru_conv_b, lru_w_a, lru_b_a, lru_w_x, lru_b_x, lru_lambda, ssd_conv_w, ssd_conv_b, ssd_dt_bias, ssd_a_log, ssd_d, ssd_norm_g, w_out):
    depth = w_in.shape[0]
    B, S, _ = x_prompt.shape
    DB, L, _ = x_sample.shape
    assert S % SPAN == 0 and L <= 8 and L >= CONV_W - 1
    win = min(MAX_WINDOW, S)

    lane_pad = lambda a: jnp.pad(a.reshape(1, -1), ((0, 0), (0, LANES - a.shape[-1])))
    expand = jnp.asarray(np.tile(np.repeat(np.eye(LANES, SSD_HEADS, dtype=np.float32), SSD_P, axis=1), (3, 1)), BF16)
    kc_t = jnp.transpose(cache_attn_k, (0, 1, 3, 4, 2))
    vc_t = jnp.transpose(cache_attn_v, (0, 1, 3, 4, 2))

    yp = x_prompt.reshape(B * S, D_MODEL)
    ys = x_sample.reshape(DB * L, D_MODEL)
    outs_p = []
    outs_s = []
    kt_p = vt_p = ssd_h_s = None
    for li in range(depth):
        w_in_b = jnp.pad(w_in[li], ((0, 0), (0, D_IN_PAD - D_IN))).astype(BF16)
        w_out_b = w_out[li].astype(BF16)
        pre_g = pre_norm_g[li].reshape(1, -1)
        post_g = post_norm_g[li].reshape(1, -1)
        wg = jnp.concatenate([_block_diag(lru_w_a[li]), _block_diag(lru_w_x[li])], axis=1).astype(BF16)
        bg = jnp.concatenate([lru_b_a[li], lru_b_x[li]]).reshape(1, -1)
        lcw, lcb, lam = lru_conv_w[li], lru_conv_b[li].reshape(1, -1), lru_lambda[li].reshape(1, -1)
        scw, scb = ssd_conv_w[li], ssd_conv_b[li].reshape(1, -1)
        dtb, alog = lane_pad(ssd_dt_bias[li]), lane_pad(ssd_a_log[li])
        dx = jnp.repeat(ssd_d[li], SSD_P).reshape(1, -1)
        ng = ssd_norm_g[li].reshape(1, -1)

        xlru3, glru, q4, k4, v4, gattn, z, xbc3, dtr, kt_p, vt_p = _proj(
            yp.reshape(B, S, D_MODEL), pre_g, w_in_b, 256, cache=(li, depth, win, kt_p, vt_p))
        y_lru, lru_h_p = _lru_prompt(xlru3, glru, lcw, lcb, wg, bg, lam, 512)
        o4 = _attn_prompt(q4, k4, v4)
        y_ssd, ssd_h_p = _ssd_prompt(xbc3, dtr, z, scw, scb, dtb, alog, dx, ng, expand)
        yp = _out_proj(yp, y_lru.reshape(B * S, D_LRU), o4.reshape(4, B * S, LANES), gattn.reshape(B * S, D_ATTN),
                       y_ssd.reshape(B * S, D_SSD), w_out_b, post_g, 512)
        outs_p.append((xlru3[:, S - (CONV_W - 1):], lru_h_p.reshape(B, D_LRU), xbc3[:, S - (CONV_W - 1):], ssd_h_p))

        xlru, glru, q4, k4, v4, gattn, z, xbc, dtr = [
            a.reshape(a.shape[:-3] + (DB * L, a.shape[-1]))
            for a in _proj(ys.reshape(1, DB * L, D_MODEL), pre_g, w_in_b, min(256, DB * L))]
        xlru3 = xlru.reshape(DB, L, D_LRU)
        y_lru_tm, lru_h_s = _lru_sample(
            xlru3.transpose(1, 0, 2), glru.reshape(DB, L, D_LRU).transpose(1, 0, 2),
            state_lru_conv[li].transpose(1, 0, 2), state_lru_h[li], lcw, lcb, wg, bg, lam)
        y_lru = y_lru_tm.transpose(1, 0, 2).reshape(DB * L, D_LRU)

        def heads_first(a4):
            a = _pairs_to_heads(a4, DB, L).transpose(0, 2, 1, 3)
            return jnp.pad(a, ((0, 0), (0, 0), (0, 8 - L), (0, 0)))

        o_s = _attn_sample(heads_first(q4), heads_first(k4), heads_first(v4), kc_t, vc_t, li, L)
        o_s = o_s[:, :, :L].transpose(0, 2, 1, 3).reshape(DB * L, 4, LANES).transpose(1, 0, 2)

        xbc3 = xbc.reshape(DB, L, D_XBC)
        cs8 = jnp.pad(state_ssd_conv[li], ((0, 0), (8 - (CONV_W - 1), 0), (0, 0)))
        y_ssd8, ssd_h_s = _ssd_sample(_pad_rows(xbc3, 8), cs8, _pad_rows(dtr.reshape(DB, L, LANES), 8),
                                      _pad_rows(z.reshape(DB, L, D_SSD), 8), state_ssd_h, li, ssd_h_s, scw, scb, dtb,
                                      alog, dx, ng, expand, L)
        y_ssd = y_ssd8[:, :L].reshape(DB * L, D_SSD)
        ys = _out_proj(ys, y_lru, o_s, gattn, y_ssd, w_out_b, post_g, DB * L)
        outs_s.append((xlru3[:, L - (CONV_W - 1):], lru_h_s, _pairs_to_heads(k4, DB, L), _pairs_to_heads(v4, DB, L),
                       xbc3[:, L - (CONV_W - 1):]))

    lru_conv_p, lru_h_p, ssd_conv_p, ssd_h_p = [jnp.stack(c) for c in zip(*outs_p)]
    lru_conv_s, lru_h_s, k_s, v_s, ssd_conv_s = [jnp.stack(c) for c in zip(*outs_s)]
    k_p = kt_p.reshape(depth, B, N_HEADS, HEAD_DIM, win).transpose(0, 1, 4, 2, 3)
    v_p = vt_p.reshape(depth, B, N_HEADS, HEAD_DIM, win).transpose(0, 1, 4, 2, 3)
    return (yp.reshape(B, S, D_MODEL), ys.reshape(DB, L, D_MODEL), lru_conv_p, lru_conv_s, lru_h_p, lru_h_s,
            k_p, k_s, v_p, v_s, ssd_conv_p, ssd_conv_s, ssd_h_p, ssd_h_s)
```

```python
import functools

import numpy as np
import jax
import jax.numpy as jnp
from jax import lax
from jax.experimental import pallas as pl
from jax.experimental.pallas import tpu as pltpu

F32 = jnp.float32
BF16 = jnp.bfloat16

D_MODEL = 1024
CONV_W = 4
EPS = 1e-6
D_LRU = 512
LRU_BLOCKS = 8
LRU_C = 8.0
HEAD_DIM = 64
D_ATTN = 512
N_HEADS = 8
DILATIONS = ((128, 1), (512, 4), (2048, 16))
MAX_WINDOW = 2048
ATTN_BLOCK = 128
D_SSD = 1024
SSD_P = 64
SSD_HEADS = 16
SSD_GROUPS = 2
SSD_N = 128
SSD_CHUNK = 128
D_XBC = D_SSD + 2 * SSD_GROUPS * SSD_N
GROUP_W = D_SSD // SSD_GROUPS

OFF_XLRU, OFF_GLRU, OFF_Q, OFF_K, OFF_V, OFF_GATTN, OFF_Z, OFF_XBC, OFF_DT = (
    0, 512, 1024, 1536, 2048, 2560, 3072, 4096, 5632)
D_IN = 5648
LANES = 128
D_IN_PAD = OFF_DT + LANES

SPAN = 2048
NRES = SPAN // ATTN_BLOCK
SSD_PROMPT_CHUNKS = 2
SSD_SAMPLE_SEQS = 4
MASKED = 1e33
NEG = -1e30

VMEM_LIMIT = 56 * 1024 * 1024


def _cparams(sem):
    return pltpu.CompilerParams(dimension_semantics=sem, vmem_limit_bytes=VMEM_LIMIT)


def _silu(x):
    return x * jax.nn.sigmoid(x)


def _softplus(x):
    return jnp.maximum(x, 0.0) + jnp.log1p(jnp.exp(-jnp.abs(x)))


def _split3(x):
    h1 = x.astype(BF16)
    r1 = x - h1.astype(F32)
    h2 = r1.astype(BF16)
    h3 = (r1 - h2.astype(F32)).astype(BF16)
    return h1, h2, h3


def _dot3_right(x, m3_bf16):
    return jnp.dot(jnp.concatenate(_split3(x), axis=1), m3_bf16, preferred_element_type=F32)


def _dot3_left(m_bf16, x):
    return jnp.dot(jnp.concatenate([m_bf16] * 3, axis=1), jnp.concatenate(_split3(x), axis=0),
                   preferred_element_type=F32)


def _dot_nt(a, b):
    return lax.dot_general(a, b, (((1,), (1,)), ((), ())), preferred_element_type=F32)


def _dot_tn(a, b):
    return lax.dot_general(a, b, (((0,), (0,)), ((), ())), preferred_element_type=F32)


def _proj_kernel(x_ref, g_ref, w_ref, wdt_ref, *rest, first_win_tile, n_alias, cache_li):
    rest = rest[n_alias:]
    xlru_ref, glru_ref, q_ref, k_ref, v_ref, gattn_ref, z_ref, xbc_ref, dt_ref = rest[:9]
    x = x_ref[...]
    h = x * lax.rsqrt(jnp.mean(x * x, axis=-1, keepdims=True) + EPS) * g_ref[...]
    hb = h.astype(BF16)

    def mm(off, width):
        return jnp.dot(hb, w_ref[:, off:off + width], preferred_element_type=F32)

    xlru_ref[...] = mm(OFF_XLRU, D_LRU)
    glru_ref[...] = mm(OFF_GLRU, D_LRU)
    q = mm(OFF_Q, D_ATTN) * (HEAD_DIM ** -0.5)
    k = mm(OFF_K, D_ATTN)
    v = mm(OFF_V, D_ATTN)
    for c in range(D_ATTN // LANES):
        q_ref[c] = q[:, c * LANES:(c + 1) * LANES]
        k_ref[c] = k[:, c * LANES:(c + 1) * LANES]
        v_ref[c] = v[:, c * LANES:(c + 1) * LANES]
    if len(rest) > 9:
        kt_ref, vt_ref = rest[9:]

        @pl.when(pl.program_id(1) >= first_win_tile)
        def _():
            for dst, val in ((kt_ref, k.T), (vt_ref, v.T)):
                if dst.ndim == 2:
                    dst[...] = val
                else:
                    for l in range(dst.shape[0]):
                        dst[l] = val if l == cache_li else jnp.zeros_like(val)

    gattn_ref[...] = mm(OFF_GATTN, D_ATTN)
    z_ref[...] = mm(OFF_Z, D_SSD)
    xbc_ref[...] = mm(OFF_XBC, D_XBC)
    dt_ref[...] = jnp.dot(hb, wdt_ref[...], preferred_element_type=F32)


def _proj(x3, g, w_bf16, wdt_bf16, tm, cache=None):
    B, S, _ = x3.shape
    row = lambda width: pl.BlockSpec((None, tm, width), lambda b, t: (b, t, 0))
    pair = pl.BlockSpec((D_ATTN // LANES, None, tm, LANES), lambda b, t: (0, b, t, 0))
    pair_shape = (4, B, S, LANES)
    sds = lambda *s: jax.ShapeDtypeStruct(s, F32)
    in_specs = [row(D_MODEL), pl.BlockSpec((1, D_MODEL), lambda b, t: (0, 0)),
                pl.BlockSpec((D_MODEL, D_IN), lambda b, t: (0, 0)),
                pl.BlockSpec((D_MODEL, LANES), lambda b, t: (0, 0))]
    args = [x3, g, w_bf16, wdt_bf16]
    out_specs = [row(D_LRU), row(D_LRU), pair, pair, pair, row(D_ATTN), row(D_SSD), row(D_XBC), row(LANES)]
    out_shape = [sds(B, S, D_LRU), sds(B, S, D_LRU), sds(*pair_shape), sds(*pair_shape), sds(*pair_shape),
                 sds(B, S, D_ATTN), sds(B, S, D_SSD), sds(B, S, D_XBC), sds(B, S, LANES)]
    first_win_tile, n_alias, aliases, cache_li = 0, 0, {}, 0
    if cache is not None:
        li, depth, win, prev_kt, prev_vt = cache
        first_win_tile = (S - win) // tm
        cache_li = li
        win_tile = lambda t: jnp.maximum(t - first_win_tile, 0)
        if prev_kt is None:
            cspec = pl.BlockSpec((depth, None, D_ATTN, tm), lambda b, t: (0, b, 0, win_tile(t)))
        else:
            cspec = pl.BlockSpec((None, None, D_ATTN, tm), lambda b, t: (li, b, 0, win_tile(t)))
        out_specs += [cspec, cspec]
        out_shape += [sds(depth, B, D_ATTN, win), sds(depth, B, D_ATTN, win)]
        if prev_kt is not None:
            n_alias = 2
            in_specs += [pl.BlockSpec(memory_space=pl.ANY)] * 2
            args += [prev_kt, prev_vt]
            aliases = {4: 9, 5: 10}
    return pl.pallas_call(
        functools.partial(_proj_kernel, first_win_tile=first_win_tile, n_alias=n_alias, cache_li=cache_li),
        grid=(B, S // tm),
        in_specs=in_specs,
        out_specs=out_specs,
        out_shape=out_shape,
        input_output_aliases=aliases,
        compiler_params=_cparams(("arbitrary", "arbitrary")),
        name="in_proj",
    )(*args)


def _lru_gates(xc, wg_ref, bg_ref, lam_ref):
    pre = jnp.dot(xc.astype(BF16), wg_ref[...], preferred_element_type=F32) + bg_ref[...]
    r = jax.nn.sigmoid(pre[:, :D_LRU])
    i = jax.nn.sigmoid(pre[:, D_LRU:])
    log_a = (-LRU_C) * r * _softplus(-lam_ref[...])
    a = jnp.exp(log_a)
    t = jnp.tanh(log_a)
    return a, jnp.sqrt(-2.0 * t / (1.0 - t)) * (i * xc)


def _scan_rows(a, b, h0):
    n, width = a.shape
    a = a.reshape(n // 8, 8, width)
    b = b.reshape(n // 8, 8, width)
    row = lax.broadcasted_iota(jnp.int32, a.shape, 1)
    k = 1
    while k < 8:
        keep = row >= k
        a_sh = jnp.where(keep, pltpu.roll(a, k, axis=1), 1.0)
        b_sh = jnp.where(keep, pltpu.roll(b, k, axis=1), 0.0)
        b = a * b_sh + b
        a = a * a_sh
        k *= 2
    blocks = []
    h = h0
    for j in range(n // 8):
        blk = b[j] + a[j] * h
        blocks.append(blk)
        h = blk[7:8, :]
    return jnp.concatenate(blocks, axis=0), h


def _conv_rows(xe_ref, n, w_ref, b_ref):
    y = b_ref[...] + xe_ref[8:8 + n, :] * w_ref[3:4, :]
    for tap in range(CONV_W - 1):
        y = y + xe_ref[5 + tap:5 + tap + n, :] * w_ref[tap:tap + 1, :]
    return y


def _lru_p_kernel(x_ref, g_ref, cw_ref, cb_ref, wg_ref, bg_ref, lam_ref, y_ref, hl_ref, xe_ref, h_ref):
    t = pl.program_id(1)
    n = x_ref.shape[0]

    @pl.when(t == 0)
    def _():
        xe_ref[0:8, :] = jnp.zeros((8, D_LRU), F32)
        h_ref[...] = jnp.zeros_like(h_ref)

    xe_ref[8:8 + n, :] = x_ref[...]
    xc = _conv_rows(xe_ref, n, cw_ref, cb_ref)
    xe_ref[0:8, :] = xe_ref[n:n + 8, :]
    a, b = _lru_gates(xc, wg_ref, bg_ref, lam_ref)
    hs, h_last = _scan_rows(a, b, h_ref[0:1, :])
    h_ref[...] = jnp.broadcast_to(h_last, h_ref.shape)
    hl_ref[...] = h_last
    y_ref[...] = (hs * _silu(g_ref[...])).astype(y_ref.dtype)


def _lru_prompt(x, g, cw, cb, wg, bg, lam, tt):
    B, S, _ = x.shape
    tile = pl.BlockSpec((None, tt, D_LRU), lambda b, t: (b, t, 0))
    const = lambda r, c: pl.BlockSpec((r, c), lambda b, t: (0, 0))
    return pl.pallas_call(
        _lru_p_kernel,
        grid=(B, S // tt),
        in_specs=[tile, tile, const(CONV_W, D_LRU), const(1, D_LRU), const(D_LRU, 2 * D_LRU), const(1, 2 * D_LRU),
                  const(1, D_LRU)],
        out_specs=[tile, pl.BlockSpec((None, 1, D_LRU), lambda b, t: (b, 0, 0))],
        out_shape=[jax.ShapeDtypeStruct((B, S, D_LRU), BF16), jax.ShapeDtypeStruct((B, 1, D_LRU), F32)],
        scratch_shapes=[pltpu.VMEM((tt + 8, D_LRU), F32), pltpu.VMEM((8, D_LRU), F32)],
        compiler_params=_cparams(("arbitrary", "arbitrary")),
        name="lru_prompt",
    )(x, g, cw, cb, wg, bg, lam)


def _lru_s_kernel(x_ref, g_ref, cs_ref, h0_ref, cw_ref, cb_ref, wg_ref, bg_ref, lam_ref, y_ref, hl_ref):
    L = x_ref.shape[0]
    db = x_ref.shape[1]
    xs = [cs_ref[i] for i in range(CONV_W - 1)] + [x_ref[i] for i in range(L)]
    xc = []
    for t in range(L):
        y = cb_ref[...] + xs[t] * cw_ref[0:1, :]
        for tap in range(1, CONV_W):
            y = y + xs[t + tap] * cw_ref[tap:tap + 1, :]
        xc.append(y)
    a, b = _lru_gates(jnp.concatenate(xc, axis=0), wg_ref, bg_ref, lam_ref)
    h = h0_ref[...]
    for t in range(L):
        h = a[t * db:(t + 1) * db] * h + b[t * db:(t + 1) * db]
        y_ref[t] = h * _silu(g_ref[t])
    hl_ref[...] = h


def _lru_sample(x_tm, g_tm, cs_tm, h0, cw, cb, wg, bg, lam):
    L, DB, _ = x_tm.shape
    return pl.pallas_call(
        _lru_s_kernel,
        out_shape=[jax.ShapeDtypeStruct((L, DB, D_LRU), F32), jax.ShapeDtypeStruct((DB, D_LRU), F32)],
        compiler_params=pltpu.CompilerParams(vmem_limit_bytes=VMEM_LIMIT),
        name="lru_sample",
    )(x_tm, g_tm, cs_tm, h0, cw, cb, wg, bg, lam)


def _block_order(rho, dil):
    if dil == 16:
        return rho
    if dil == 4:
        return 4 * (rho & 31) + (rho >> 5)
    return 16 * (rho & 7) + (rho >> 3)


def _chunks(dil):
    return {16: (1, 128), 4: (4, 32), 1: (16, 8)}[dil]


def _attn_p_kernel(q_ref, k_ref, v_ref, o_ref, qs, ks, vs, r_st, m_st, bias_ref, *, unroll):
    c = pl.program_id(1)
    s = pl.program_id(2)
    nres = SPAN // ATTN_BLOCK
    lane_q = lax.broadcasted_iota(jnp.int32, (ATTN_BLOCK, LANES), 1)
    first_half = lane_q < HEAD_DIM

    cur = s & 1
    prev = 1 - cur

    @pl.when(s == 0)
    def _():
        ks[prev] = jnp.zeros((SPAN, LANES), F32)
        vs[0, prev] = jnp.zeros((SPAN, LANES), F32)
        vs[1, prev] = jnp.zeros((SPAN, LANES), F32)

    for r in range(nres):
        rows = pl.ds(r, ATTN_BLOCK, stride=nres)
        dst = slice(r * ATTN_BLOCK, (r + 1) * ATTN_BLOCK)
        q = q_ref[rows, :]
        v = v_ref[rows, :]
        qs[0, dst, :] = jnp.where(first_half, q, 0.0)
        qs[1, dst, :] = jnp.where(first_half, 0.0, q)
        ks[cur, dst, :] = k_ref[rows, :]
        vs[0, cur, dst, :] = jnp.where(first_half, v, 1.0)
        vs[1, cur, dst, :] = jnp.where(first_half, 1.0, v)

    span0 = s == 0

    @pl.when(span0)
    def _():
        iq = lax.broadcasted_iota(jnp.int32, (ATTN_BLOCK, 2 * ATTN_BLOCK), 0)
        ik = lax.broadcasted_iota(jnp.int32, (ATTN_BLOCK, 2 * ATTN_BLOCK), 1)
        in_cur = ik >= ATTN_BLOCK
        for di, (window, dil) in enumerate(DILATIONS):
            j = ATTN_BLOCK + _block_order(iq, dil) - (
                jnp.where(in_cur, ATTN_BLOCK, 0) + _block_order(ik & (ATTN_BLOCK - 1), dil))
            valid = (j >= 0) & (j <= window // dil)
            dist = (j * dil).astype(F32)
            dist_all = jnp.where(valid, dist, MASKED)
            dist_cur = jnp.where(valid & in_cur, dist, MASKED)
            for hh in range(2):
                slope = jnp.exp2(-(jnp.zeros((1, 1), F32) + (2 * c + hh + 1).astype(F32)))
                bias_ref[di, hh, 0] = -slope * dist_all
                bias_ref[di, hh, 1] = -slope * dist_cur

    def block(di, dil, it):
        nchunk, crow = _chunks(dil)

        def gather(ref, lead, starts):
            parts = [ref[lead + (pl.ds(st, crow), slice(None))] for st in starts]
            return parts[0] if len(parts) == 1 else jnp.concatenate(parts, axis=0)

        if dil == 16:
            base, cb, last_off = it * ATTN_BLOCK, 0, 0
        elif dil == 4:
            base, cb, last_off = (it >> 2) * ATTN_BLOCK, it & 3, 3 * crow
        else:
            base, cb, last_off = 0, it, 15 * crow
        stride = {16: 0, 4: 4 * ATTN_BLOCK, 1: ATTN_BLOCK}[dil]
        cur_starts = [pl.multiple_of(base + a * stride + crow * cb, 8) for a in range(nchunk)]
        if dil == 16:
            prev_slot = prev
            prev_starts = cur_starts
            first_in_span = True
        else:
            has_prev = cb > 0
            prev_slot = jnp.where(has_prev, cur, prev)
            prev_off = jnp.where(has_prev, crow * (cb - 1), last_off)
            prev_starts = [pl.multiple_of(base + a * stride + prev_off, 8) for a in range(nchunk)]
            first_in_span = jnp.logical_not(has_prev)
        kb = jnp.concatenate([gather(ks, (prev_slot,), prev_starts), gather(ks, (cur,), cur_starts)],
                             axis=0).astype(BF16)
        variant = jnp.where(jnp.logical_and(span0, first_in_span), 1, 0)
        for hh in range(2):
            qh = gather(qs, (hh,), cur_starts).astype(BF16)
            vh = jnp.concatenate([gather(vs, (hh, prev_slot), prev_starts), gather(vs, (hh, cur), cur_starts)],
                                 axis=0).astype(BF16)
            sc = _dot_nt(qh, kb) + bias_ref[di, hh, variant]
            m = jnp.max(sc, axis=1, keepdims=True)
            p = jnp.exp(sc - m)
            r_new = jnp.dot(p.astype(BF16), vh, preferred_element_type=F32)
            m_new = jnp.broadcast_to(m, (ATTN_BLOCK, LANES))
            for a, st in enumerate(cur_starts):
                r_st[di, hh, pl.ds(st, crow), :] = r_new[a * crow:(a + 1) * crow, :]
                m_st[di, hh, pl.ds(st, crow), :] = m_new[a * crow:(a + 1) * crow, :]

    def body(it, carry):
        for di, (_, dil) in enumerate(DILATIONS):
            block(di, dil, it)
        return carry

    lax.fori_loop(0, nres, body, 0, unroll=unroll)

    ndil = len(DILATIONS)
    for r in range(nres):
        src = slice(r * ATTN_BLOCK, (r + 1) * ATTN_BLOCK)
        outs = []
        for hh in range(2):
            ms = [m_st[di, hh, src, :] for di in range(ndil)]
            m_tot = functools.reduce(jnp.maximum, ms)
            tot = sum(r_st[di, hh, src, :] * jnp.exp(ms[di] - m_tot) for di in range(ndil))
            outs.append(tot / pltpu.roll(tot, HEAD_DIM, axis=1))
        o_ref[pl.ds(r, ATTN_BLOCK, stride=nres), :] = jnp.where(first_half, outs[0], outs[1])


def _attn_prompt(q4, k4, v4, unroll=8):
    npair, B, S, _ = q4.shape
    ndil = len(DILATIONS)
    spec = pl.BlockSpec((None, None, SPAN, LANES), lambda b, c, s: (c, b, s, 0))
    return pl.pallas_call(
        functools.partial(_attn_p_kernel, unroll=unroll),
        grid=(B, npair, S // SPAN),
        in_specs=[spec, spec, spec],
        out_specs=spec,
        out_shape=jax.ShapeDtypeStruct(q4.shape, F32),
        scratch_shapes=[pltpu.VMEM((2, SPAN, LANES), F32), pltpu.VMEM((2, SPAN, LANES), F32),
                        pltpu.VMEM((2, 2, SPAN, LANES), F32),
                        pltpu.VMEM((ndil, 2, SPAN, LANES), F32), pltpu.VMEM((ndil, 2, SPAN, LANES), F32),
                        pltpu.VMEM((ndil, 2, 2, ATTN_BLOCK, 2 * ATTN_BLOCK), F32)],
        compiler_params=_cparams(("arbitrary", "arbitrary", "arbitrary")),
        name="attn_prompt",
    )(q4, k4, v4)


def _attn_s_kernel(q_ref, kn_ref, vn_ref, kc_ref, vc_ref, o_ref, dist_ref, mult_ref, *, n_new):
    nseq, _, _, W = kc_ref.shape
    rows = q_ref.shape[1]
    shift = n_new.bit_length() - 1

    @pl.when(pl.program_id(0) == 0)
    def _():
        i = lax.broadcasted_iota(jnp.int32, (rows, W), 0) & (n_new - 1)
        pos = lax.broadcasted_iota(jnp.int32, (rows, W), 1)
        d = W + i - pos
        cnt = jnp.zeros((rows, W), F32)
        for window, dil in DILATIONS:
            cnt = cnt + jnp.where(((d & (dil - 1)) == 0) & (d <= window), 1.0, 0.0)
        dist_ref[...] = d.astype(F32)
        mult_ref[...] = cnt

    r = lax.broadcasted_iota(jnp.int32, (rows, rows), 0)
    r2 = lax.broadcasted_iota(jnp.int32, (rows, rows), 1)
    dn = (r & (n_new - 1)) - (r2 & (n_new - 1))
    same_seq = (r >> shift) == (r2 >> shift)
    mult_n = jnp.zeros((rows, rows), F32)
    for window, dil in DILATIONS:
        ok = ((dn & (dil - 1)) == 0) & (dn >= 0) & (dn <= window) & same_seq
        mult_n = mult_n + jnp.where(ok, 1.0, 0.0)
    dist_n = dn.astype(F32)
    dist_c = dist_ref[...]
    mult_c = mult_ref[...]
    row_seq = lax.broadcasted_iota(jnp.int32, (rows, HEAD_DIM), 0) >> shift
    for c in range(D_ATTN // LANES):
        halves = []
        for hh in range(2):
            h = 2 * c + hh
            slope = 2.0 ** (-(h + 1))
            lanes = slice(hh * HEAD_DIM, (hh + 1) * HEAD_DIM)
            q = q_ref[c][:, lanes].astype(BF16)
            vn = vn_ref[c][:, lanes].astype(BF16)
            sn = _dot_nt(q, kn_ref[c][:, lanes].astype(BF16))
            sn = jnp.where(mult_n > 0, sn - slope * dist_n, NEG)
            mn = jnp.max(sn, axis=1, keepdims=True)
            o_h = jnp.zeros((rows, HEAD_DIM), F32)
            for b in range(nseq):
                sc = jnp.dot(q, kc_ref[b, h].astype(BF16), preferred_element_type=F32)
                sc = jnp.where(mult_c > 0, sc - slope * dist_c, NEG)
                m = jnp.maximum(jnp.max(sc, axis=1, keepdims=True), mn)
                pc = mult_c * jnp.exp(sc - m)
                pn = mult_n * jnp.exp(sn - m)
                l = jnp.sum(pc, axis=1, keepdims=True) + jnp.sum(pn, axis=1, keepdims=True)
                o = _dot_nt(pc.astype(BF16), vc_ref[b, h].astype(BF16)) + jnp.dot(
                    pn.astype(BF16), vn, preferred_element_type=F32)
                o_h = jnp.where(row_seq == b, o / l, o_h)
            halves.append(o_h)
        o_ref[c] = jnp.concatenate(halves, axis=1)


def _attn_sample(q4, k4, v4, kc_t, vc_t, li, n_new):
    npair, T, _ = q4.shape
    W = kc_t.shape[-1]
    rows = 8
    assert n_new & (n_new - 1) == 0 and rows % n_new == 0 and T % rows == 0
    small = pl.BlockSpec((npair, rows, LANES), lambda b: (0, b, 0))
    cache = pl.BlockSpec((None, rows // n_new, N_HEADS, HEAD_DIM, W), lambda b: (li, b, 0, 0, 0))
    return pl.pallas_call(
        functools.partial(_attn_s_kernel, n_new=n_new),
        grid=(T // rows,),
        in_specs=[small, small, small, cache, cache],
        out_specs=small,
        out_shape=jax.ShapeDtypeStruct(q4.shape, F32),
        scratch_shapes=[pltpu.VMEM((rows, W), F32), pltpu.VMEM((rows, W), F32)],
        compiler_params=_cparams(("arbitrary",)),
        name="attn_sample",
    )(q4, k4, v4, kc_t, vc_t)


def _group_norm_out(y, z, ng_ref):
    yg = y * _silu(z)
    outs = []
    for g in range(SSD_GROUPS):
        part = yg[:, g * GROUP_W:(g + 1) * GROUP_W]
        outs.append(part * lax.rsqrt(jnp.mean(part * part, axis=-1, keepdims=True) + EPS))
    return jnp.concatenate(outs, axis=1) * ng_ref[...]


def _ssd_p_kernel(xbc_ref, dt_ref, z_ref, cw_ref, cb_ref, dtb_ref, alog_ref, dx_ref, ng_ref, e_ref,
                  y_ref, hl_ref, xe_ref, ht_ref):
    t = pl.program_id(1)
    Q = SSD_CHUNK
    n = xbc_ref.shape[0]

    @pl.when(t == 0)
    def _():
        xe_ref[0:8, :] = jnp.zeros((8, D_XBC), F32)
        ht_ref[...] = jnp.zeros_like(ht_ref)

    xe_ref[8:8 + n, :] = xbc_ref[...]
    xc_all = _silu(_conv_rows(xe_ref, n, cw_ref, cb_ref))
    xe_ref[0:8, :] = xe_ref[n:n + 8, :]
    dt_all = _softplus(dt_ref[...] + dtb_ref[...])
    a_neg = -jnp.exp(alog_ref[...])
    ri = lax.broadcasted_iota(jnp.int32, (Q, Q), 0)
    ci = lax.broadcasted_iota(jnp.int32, (Q, Q), 1)
    causal = ri >= ci
    tri = jnp.where(causal, 1.0, 0.0).astype(BF16)
    lane = lax.broadcasted_iota(jnp.int32, (Q, LANES), 1)
    for ck in range(n // Q):
        rows = slice(ck * Q, (ck + 1) * Q)
        y = _ssd_chunk(xc_all[rows, :], dt_all[rows, :], a_neg, causal, tri, lane, dx_ref, e_ref, ht_ref)
        y_ref[rows, :] = _group_norm_out(y, z_ref[rows, :], ng_ref).astype(y_ref.dtype)

    @pl.when(t == pl.num_programs(1) - 1)
    def _():
        for g in range(SSD_GROUPS):
            hl_ref[g * (SSD_HEADS // SSD_GROUPS):(g + 1) * (SSD_HEADS // SSD_GROUPS)] = (
                ht_ref[g].T.reshape(SSD_HEADS // SSD_GROUPS, SSD_P, SSD_N))


def _ssd_chunk(xc, dt, a_neg, causal, tri, lane, dx_ref, e_ref, ht_ref):
    Q = SSD_CHUNK
    xs = xc[:, :D_SSD]
    bm = xc[:, D_SSD:D_SSD + SSD_GROUPS * SSD_N]
    cm = xc[:, D_SSD + SSD_GROUPS * SSD_N:]
    cum = _dot3_left(tri, dt * a_neg)
    cum_t = cum.T
    both_x = _dot3_right(jnp.concatenate([cum, dt], axis=0), e_ref[...])
    cum_x = both_x[:Q]
    dt_x = both_x[Q:]
    xdt = xs * dt_x
    ecum = jnp.exp(cum_x)
    cl = cum_x[Q - 1:Q, :]
    w_end = xdt * jnp.exp(cl - cum_x)
    dec = jnp.exp(cl)

    y_parts = []
    for g in range(SSD_GROUPS):
        cg = cm[:, g * SSD_N:(g + 1) * SSD_N].astype(BF16)
        bg = bm[:, g * SSD_N:(g + 1) * SSD_N]
        cb = _dot_nt(cg, bg.astype(BF16))
        gl = slice(g * GROUP_W, (g + 1) * GROUP_W)
        h_old = ht_ref[g]
        y_off = jnp.dot(cg, h_old.astype(BF16), preferred_element_type=F32) * ecum[:, gl]
        ht_ref[g] = h_old * dec[:, gl] + jnp.dot(bg.T.astype(BF16), w_end[:, gl].astype(BF16),
                                                  preferred_element_type=F32)
        for pp in range(GROUP_W // LANES):
            pl_ = slice(g * GROUP_W + pp * LANES, g * GROUP_W + (pp + 1) * LANES)
            xdt_pair = xdt[:, pl_].astype(BF16)
            halves = []
            for hh in range(2):
                h = (g * GROUP_W + pp * LANES) // SSD_P + hh
                seg = jnp.broadcast_to(cum[:, h:h + 1], (Q, Q)) - jnp.broadcast_to(cum_t[h:h + 1, :], (Q, Q))
                scores = cb * jnp.exp(jnp.where(causal, seg, NEG))
                halves.append(jnp.dot(scores.astype(BF16), xdt_pair, preferred_element_type=F32))
            y_diag = jnp.where(lane < SSD_P, halves[0], halves[1])
            y_parts.append(y_diag + y_off[:, pp * LANES:(pp + 1) * LANES])
    return jnp.concatenate(y_parts, axis=1) + dx_ref[...] * xs


def _ssd_prompt(xbc, dt, z, cw, cb, dtb, alog, dx, ng, e):
    B, S, _ = xbc.shape
    Q = SSD_PROMPT_CHUNKS * SSD_CHUNK
    tile = lambda w: pl.BlockSpec((None, Q, w), lambda b, t: (b, t, 0))
    const = lambda r, c: pl.BlockSpec((r, c), lambda b, t: (0, 0))
    return pl.pallas_call(
        _ssd_p_kernel,
        grid=(B, S // Q),
        in_specs=[tile(D_XBC), tile(LANES), tile(D_SSD), const(CONV_W, D_XBC), const(1, D_XBC), const(1, LANES),
                  const(1, LANES), const(1, D_SSD), const(1, D_SSD), const(3 * LANES, D_SSD)],
        out_specs=[tile(D_SSD), pl.BlockSpec((None, SSD_HEADS, SSD_P, SSD_N), lambda b, t: (b, 0, 0, 0))],
        out_shape=[jax.ShapeDtypeStruct((B, S, D_SSD), BF16), jax.ShapeDtypeStruct((B, SSD_HEADS, SSD_P, SSD_N), F32)],
        scratch_shapes=[pltpu.VMEM((Q + 8, D_XBC), F32), pltpu.VMEM((SSD_GROUPS, SSD_N, GROUP_W), F32)],
        compiler_params=_cparams(("arbitrary", "arbitrary")),
        name="ssd_prompt",
    )(xbc, dt, z, cw, cb, dtb, alog, dx, ng, e)


def _ssd_s_kernel(xbc_ref, cs_ref, dt_ref, z_ref, h_ref, cw_ref, cb_ref, dtb_ref, alog_ref, dx_ref, ng_ref, e_ref,
                  *rest, n_new, li):
    y_ref, hn_ref = rest[-2:]
    if hn_ref.ndim == 5:
        for l in range(hn_ref.shape[0]):
            if l != li:
                hn_ref[l] = jnp.zeros(hn_ref.shape[1:], F32)
        hn_ref = hn_ref.at[li]
    for i in range(xbc_ref.shape[0]):
        _ssd_s_one(xbc_ref.at[i], cs_ref.at[i], dt_ref.at[i], z_ref.at[i], h_ref.at[i], cw_ref, cb_ref, dtb_ref,
                   alog_ref, dx_ref, ng_ref, e_ref, y_ref.at[i], hn_ref.at[i], n_new)


def _ssd_s_one(xbc_ref, cs_ref, dt_ref, z_ref, h_ref, cw_ref, cb_ref, dtb_ref, alog_ref, dx_ref, ng_ref, e_ref,
               y_ref, hn_ref, n_new):
    R = 8
    row = lax.broadcasted_iota(jnp.int32, (R, LANES), 0)
    real = row < n_new
    xe = jnp.concatenate([cs_ref[...], xbc_ref[...]], axis=0)
    y = cb_ref[...] + xe[8:16, :] * cw_ref[3:4, :]
    for tap in range(CONV_W - 1):
        y = y + xe[5 + tap:13 + tap, :] * cw_ref[tap:tap + 1, :]
    xc = _silu(y)
    xs = xc[:, :D_SSD]
    bm = xc[:, D_SSD:D_SSD + SSD_GROUPS * SSD_N]
    cm = xc[:, D_SSD + SSD_GROUPS * SSD_N:]

    dt = jnp.where(real, _softplus(dt_ref[...] + dtb_ref[...]), 0.0)
    da = dt * (-jnp.exp(alog_ref[...]))
    cum = da
    k = 1
    while k < R:
        cum = cum + jnp.where(row >= k, pltpu.roll(cum, k, axis=0), 0.0)
        k *= 2
    e = e_ref[...]
    cum_x = _dot3_right(cum, e)
    dt_x = _dot3_right(dt, e)
    xdt = xs * dt_x
    ecum = jnp.exp(cum_x)
    cl = cum_x[R - 1:R, :]
    w_end = xdt * jnp.exp(cl - cum_x)
    dec3 = _split3(jnp.exp(cl))
    ones = jnp.ones((R, SSD_N), BF16)
    zrow = jnp.zeros((R - 3, GROUP_W), BF16)

    y_parts = []
    for g in range(SSD_GROUPS):
        gl = slice(g * GROUP_W, (g + 1) * GROUP_W)
        cg = cm[:, g * SSD_N:(g + 1) * SSD_N]
        bg = bm[:, g * SSD_N:(g + 1) * SSD_N]
        hs = slice(g * (SSD_HEADS // SSD_GROUPS), (g + 1) * (SSD_HEADS // SSD_GROUPS))
        h_old = h_ref[hs].reshape(GROUP_W, SSD_N)
        y_g = _dot_nt(cg.astype(BF16), h_old.astype(BF16)) * ecum[:, gl]
        for j in range(n_new):
            cbj = jnp.sum(cg * bg[j:j + 1, :], axis=1, keepdims=True)
            term = cbj * jnp.exp(cum_x[:, gl] - cum_x[j:j + 1, gl]) * xdt[j:j + 1, gl]
            y_g = y_g + jnp.where(row[:, 0:1] >= j, term, 0.0)
        y_parts.append(y_g)
        dec_col = _dot_tn(jnp.concatenate([p[:, gl] for p in dec3] + [zrow], axis=0), ones)
        upd = _dot_tn(w_end[:, gl].astype(BF16), bg.astype(BF16))
        hn_ref[hs] = (h_old * dec_col + upd).reshape(SSD_HEADS // SSD_GROUPS, SSD_P, SSD_N)
    y = jnp.concatenate(y_parts, axis=1) + dx_ref[...] * xs
    y_ref[...] = _group_norm_out(y, z_ref[...], ng_ref)


def _ssd_sample(xbc, cs, dt, z, h_all, li, h_new_prev, cw, cb, dtb, alog, dx, ng, e, n_new):
    DB = xbc.shape[0]
    bb = SSD_SAMPLE_SEQS if DB % SSD_SAMPLE_SEQS == 0 else 1
    tile = lambda w: pl.BlockSpec((bb, 8, w), lambda b: (b, 0, 0))
    const = lambda r, c: pl.BlockSpec((r, c), lambda b: (0, 0))
    state = pl.BlockSpec((None, bb, SSD_HEADS, SSD_P, SSD_N), lambda b: (li, b, 0, 0, 0))
    in_specs = [tile(D_XBC), tile(D_XBC), tile(LANES), tile(D_SSD), state, const(CONV_W, D_XBC), const(1, D_XBC),
                const(1, LANES), const(1, LANES), const(1, D_SSD), const(1, D_SSD), const(3 * LANES, D_SSD)]
    args = [xbc, cs, dt, z, h_all, cw, cb, dtb, alog, dx, ng, e]
    aliases = {}
    if h_new_prev is not None:
        in_specs.append(pl.BlockSpec(memory_space=pl.ANY))
        args.append(h_new_prev)
        aliases = {len(args) - 1: 1}
        state_out = state
    else:
        state_out = pl.BlockSpec((h_all.shape[0], bb, SSD_HEADS, SSD_P, SSD_N), lambda b: (0, b, 0, 0, 0))
    return pl.pallas_call(
        functools.partial(_ssd_s_kernel, n_new=n_new, li=li),
        grid=(DB // bb,),
        in_specs=in_specs,
        out_specs=[tile(D_SSD), state_out],
        out_shape=[jax.ShapeDtypeStruct((DB, 8, D_SSD), F32), jax.ShapeDtypeStruct(h_all.shape, F32)],
        input_output_aliases=aliases,
        compiler_params=_cparams(("arbitrary",)),
        name="ssd_sample",
    )(*args)


def _out_kernel(x_ref, ylru_ref, o_ref, gattn_ref, yssd_ref, w_ref, g_ref, out_ref):
    o = jnp.concatenate([o_ref[c] for c in range(D_ATTN // LANES)], axis=1)
    y_attn = o * _silu(gattn_ref[...])
    mix_in = jnp.concatenate([ylru_ref[...].astype(BF16), y_attn.astype(BF16), yssd_ref[...].astype(BF16)], axis=1)
    mix = jnp.dot(mix_in, w_ref[...], preferred_element_type=F32)
    out_ref[...] = x_ref[...] + mix * lax.rsqrt(jnp.mean(mix * mix, axis=-1, keepdims=True) + EPS) * g_ref[...]


def _out_proj(x2d, y_lru, o4, g_attn, y_ssd, w_bf16, g, tm):
    T = x2d.shape[0]
    row = lambda width: pl.BlockSpec((tm, width), lambda i: (i, 0))
    o_spec = pl.BlockSpec((D_ATTN // LANES, tm, LANES), lambda i: (0, i, 0))
    return pl.pallas_call(
        _out_kernel,
        grid=(T // tm,),
        in_specs=[row(D_MODEL), row(D_LRU), o_spec,
                  row(D_ATTN), row(D_SSD), pl.BlockSpec((D_LRU + D_ATTN + D_SSD, D_MODEL), lambda i: (0, 0)),
                  pl.BlockSpec((1, D_MODEL), lambda i: (0, 0))],
        out_specs=row(D_MODEL),
        out_shape=jax.ShapeDtypeStruct((T, D_MODEL), F32),
        compiler_params=_cparams(("arbitrary",)),
        name="out_proj",
    )(x2d, y_lru, o4, g_attn, y_ssd, w_bf16, g)


def _block_diag(w):
    k, n, _ = w.shape
    return jnp.einsum("kij,kl->kilj", w, jnp.eye(k, dtype=w.dtype)).reshape(k * n, k * n)


def _pad_rows(a, rows):
    return jnp.pad(a, ((0, 0), (0, rows - a.shape[1]), (0, 0)))


def _pairs_to_heads(a4, nb, L):
    return a4.reshape(4, nb, L, 2, HEAD_DIM).transpose(1, 2, 0, 3, 4).reshape(nb, L, N_HEADS, HEAD_DIM)


def kernel(x_prompt, x_sample, state_lru_conv, state_lru_h, cache_attn_k, cache_attn_v, state_ssd_conv, state_ssd_h, pre_norm_g, post_norm_g, w_in, lru_conv_w, lru_conv_b, lru_w_a, lru_b_a, lru_w_x, lru_b_x, lru_lambda, ssd_conv_w, ssd_conv_b, ssd_dt_bias, ssd_a_log, ssd_d, ssd_norm_g, w_out):
    depth = w_in.shape[0]
    B, S, _ = x_prompt.shape
    DB, L, _ = x_sample.shape
    assert S % SPAN == 0 and L <= 8 and L >= CONV_W - 1
    win = min(MAX_WINDOW, S)

    lane_pad = lambda a: jnp.pad(a.reshape(1, -1), ((0, 0), (0, LANES - a.shape[-1])))
    expand = jnp.asarray(np.tile(np.repeat(np.eye(LANES, SSD_HEADS, dtype=np.float32), SSD_P, axis=1), (3, 1)), BF16)
    kc_t = jnp.transpose(cache_attn_k, (0, 1, 3, 4, 2))
    vc_t = jnp.transpose(cache_attn_v, (0, 1, 3, 4, 2))

    yp = x_prompt.reshape(B * S, D_MODEL)
    ys = x_sample.reshape(DB * L, D_MODEL)
    outs_p = []
    outs_s = []
    kt_p = vt_p = ssd_h_s = None
    for li in range(depth):
        w_in_b = w_in[li].astype(BF16)
        w_dt_b = jnp.pad(w_in[li][:, OFF_DT:], ((0, 0), (0, D_IN_PAD - D_IN))).astype(BF16)
        w_out_b = w_out[li].astype(BF16)
        pre_g = pre_norm_g[li].reshape(1, -1)
        post_g = post_norm_g[li].reshape(1, -1)
        wg = jnp.concatenate([_block_diag(lru_w_a[li]), _block_diag(lru_w_x[li])], axis=1).astype(BF16)
        bg = jnp.concatenate([lru_b_a[li], lru_b_x[li]]).reshape(1, -1)
        lcw, lcb, lam = lru_conv_w[li], lru_conv_b[li].reshape(1, -1), lru_lambda[li].reshape(1, -1)
        scw, scb = ssd_conv_w[li], ssd_conv_b[li].reshape(1, -1)
        dtb, alog = lane_pad(ssd_dt_bias[li]), lane_pad(ssd_a_log[li])
        dx = jnp.repeat(ssd_d[li], SSD_P).reshape(1, -1)
        ng = ssd_norm_g[li].reshape(1, -1)

        xlru3, glru, q4, k4, v4, gattn, z, xbc3, dtr, kt_p, vt_p = _proj(
            yp.reshape(B, S, D_MODEL), pre_g, w_in_b, w_dt_b, 256, cache=(li, depth, win, kt_p, vt_p))
        y_lru, lru_h_p = _lru_prompt(xlru3, glru, lcw, lcb, wg, bg, lam, 512)
        o4 = _attn_prompt(q4, k4, v4)
        y_ssd, ssd_h_p = _ssd_prompt(xbc3, dtr, z, scw, scb, dtb, alog, dx, ng, expand)
        yp = _out_proj(yp, y_lru.reshape(B * S, D_LRU), o4.reshape(4, B * S, LANES), gattn.reshape(B * S, D_ATTN),
                       y_ssd.reshape(B * S, D_SSD), w_out_b, post_g, 512)
        outs_p.append((xlru3[:, S - (CONV_W - 1):], lru_h_p.reshape(B, D_LRU), xbc3[:, S - (CONV_W - 1):], ssd_h_p))

        xlru, glru, q4, k4, v4, gattn, z, xbc, dtr = [
            a.reshape(a.shape[:-3] + (DB * L, a.shape[-1]))
            for a in _proj(ys.reshape(1, DB * L, D_MODEL), pre_g, w_in_b, w_dt_b, min(256, DB * L))]
        xlru3 = xlru.reshape(DB, L, D_LRU)
        y_lru_tm, lru_h_s = _lru_sample(
            xlru3.transpose(1, 0, 2), glru.reshape(DB, L, D_LRU).transpose(1, 0, 2),
            state_lru_conv[li].transpose(1, 0, 2), state_lru_h[li], lcw, lcb, wg, bg, lam)
        y_lru = y_lru_tm.transpose(1, 0, 2).reshape(DB * L, D_LRU)

        o_s = _attn_sample(q4, k4, v4, kc_t, vc_t, li, L)

        xbc3 = xbc.reshape(DB, L, D_XBC)
        cs8 = jnp.pad(state_ssd_conv[li], ((0, 0), (8 - (CONV_W - 1), 0), (0, 0)))
        y_ssd8, ssd_h_s = _ssd_sample(_pad_rows(xbc3, 8), cs8, _pad_rows(dtr.reshape(DB, L, LANES), 8),
                                      _pad_rows(z.reshape(DB, L, D_SSD), 8), state_ssd_h, li, ssd_h_s, scw, scb, dtb,
                                      alog, dx, ng, expand, L)
        y_ssd = y_ssd8[:, :L].reshape(DB * L, D_SSD)
        ys = _out_proj(ys, y_lru, o_s, gattn, y_ssd, w_out_b, post_g, DB * L)
        outs_s.append((xlru3[:, L - (CONV_W - 1):], lru_h_s, _pairs_to_heads(k4, DB, L), _pairs_to_heads(v4, DB, L),
                       xbc3[:, L - (CONV_W - 1):]))

    lru_conv_p, lru_h_p, ssd_conv_p, ssd_h_p = [jnp.stack(c) for c in zip(*outs_p)]
    lru_conv_s, lru_h_s, k_s, v_s, ssd_conv_s = [jnp.stack(c) for c in zip(*outs_s)]
    k_p = kt_p.reshape(depth, B, N_HEADS, HEAD_DIM, win).transpose(0, 1, 4, 2, 3)
    v_p = vt_p.reshape(depth, B, N_HEADS, HEAD_DIM, win).transpose(0, 1, 4, 2, 3)
    return (yp.reshape(B, S, D_MODEL), ys.reshape(DB, L, D_MODEL), lru_conv_p, lru_conv_s, lru_h_p, lru_h_s,
            k_p, k_s, v_p, v_s, ssd_conv_p, ssd_conv_s, ssd_h_p, ssd_h_s)
```

```python
import functools

import numpy as np
import jax
import jax.numpy as jnp
from jax import lax
from jax.experimental import pallas as pl
from jax.experimental.pallas import tpu as pltpu

F32 = jnp.float32
BF16 = jnp.bfloat16

D_MODEL = 1024
CONV_W = 4
EPS = 1e-6
D_LRU = 512
LRU_BLOCKS = 8
LRU_C = 8.0
HEAD_DIM = 64
D_ATTN = 512
N_HEADS = 8
DILATIONS = ((128, 1), (512, 4), (2048, 16))
MAX_WINDOW = 2048
ATTN_BLOCK = 128
D_SSD = 1024
SSD_P = 64
SSD_HEADS = 16
SSD_GROUPS = 2
SSD_N = 128
SSD_CHUNK = 128
D_XBC = D_SSD + 2 * SSD_GROUPS * SSD_N
GROUP_W = D_SSD // SSD_GROUPS

OFF_XLRU, OFF_GLRU, OFF_Q, OFF_K, OFF_V, OFF_GATTN, OFF_Z, OFF_XBC, OFF_DT = (
    0, 512, 1024, 1536, 2048, 2560, 3072, 4096, 5632)
D_IN = 5648
LANES = 128
D_IN_PAD = OFF_DT + LANES

SPAN = 2048
NRES = SPAN // ATTN_BLOCK
SSD_PROMPT_CHUNKS = 2
SSD_SAMPLE_SEQS = 4
MASKED = 1e33
NEG = -1e30

VMEM_LIMIT = 56 * 1024 * 1024


def _cparams(sem):
    return pltpu.CompilerParams(dimension_semantics=sem, vmem_limit_bytes=VMEM_LIMIT)


def _silu(x):
    return x * jax.nn.sigmoid(x)


def _softplus(x):
    return jnp.maximum(x, 0.0) + jnp.log1p(jnp.exp(-jnp.abs(x)))


def _split3(x):
    h1 = x.astype(BF16)
    r1 = x - h1.astype(F32)
    h2 = r1.astype(BF16)
    h3 = (r1 - h2.astype(F32)).astype(BF16)
    return h1, h2, h3


def _dot3_right(x, m3_bf16):
    return jnp.dot(jnp.concatenate(_split3(x), axis=1), m3_bf16, preferred_element_type=F32)


def _dot3_left(m_bf16, x):
    return jnp.dot(jnp.concatenate([m_bf16] * 3, axis=1), jnp.concatenate(_split3(x), axis=0),
                   preferred_element_type=F32)


def _dot_nt(a, b):
    return lax.dot_general(a, b, (((1,), (1,)), ((), ())), preferred_element_type=F32)


def _dot_tn(a, b):
    return lax.dot_general(a, b, (((0,), (0,)), ((), ())), preferred_element_type=F32)


def _proj_kernel(x_ref, g_ref, w_ref, wdt_ref, *rest, first_win_tile, n_alias, cache_li, prompt):
    if prompt:
        lcw_ref, lcb_ref, scw_ref, scb_ref = rest[:4]
        rest = rest[4 + n_alias:]
        kt_ref, vt_ref, tail_lru_ref, tail_xbc_ref, xe_lru, xe_xbc = rest[9:]
    xlru_ref, glru_ref, q_ref, k_ref, v_ref, gattn_ref, z_ref, xbc_ref, dt_ref = rest[:9]
    n = x_ref.shape[0]
    x = x_ref[...]
    h = x * lax.rsqrt(jnp.mean(x * x, axis=-1, keepdims=True) + EPS) * g_ref[...]
    hb = h.astype(BF16)

    def mm(off, width):
        return jnp.dot(hb, w_ref[:, off:off + width], preferred_element_type=F32)

    def conv(raw, xe_ref, cw_ref, cb_ref, tail_ref):
        xe_ref[8:8 + n, :] = raw
        out = _conv_rows(xe_ref, n, cw_ref, cb_ref)
        last = xe_ref[n:n + 8, :]
        xe_ref[0:8, :] = last
        tail_ref[...] = last
        return out

    if prompt:
        @pl.when(pl.program_id(1) == 0)
        def _():
            xe_lru[0:8, :] = jnp.zeros((8, D_LRU), F32)
            xe_xbc[0:8, :] = jnp.zeros((8, D_XBC), F32)

    xbc = mm(OFF_XBC, D_XBC)
    xbc_ref[...] = _silu(conv(xbc, xe_xbc, scw_ref, scb_ref, tail_xbc_ref)) if prompt else xbc
    x_lru = mm(OFF_XLRU, D_LRU)
    xlru_ref[...] = conv(x_lru, xe_lru, lcw_ref, lcb_ref, tail_lru_ref) if prompt else x_lru
    z = mm(OFF_Z, D_SSD)
    z_ref[...] = _silu(z) if prompt else z
    g_lru = mm(OFF_GLRU, D_LRU)
    glru_ref[...] = _silu(g_lru) if prompt else g_lru
    q = mm(OFF_Q, D_ATTN) * (HEAD_DIM ** -0.5)
    k = mm(OFF_K, D_ATTN)
    v = mm(OFF_V, D_ATTN)
    for c in range(D_ATTN // LANES):
        q_ref[c] = q[:, c * LANES:(c + 1) * LANES]
        k_ref[c] = k[:, c * LANES:(c + 1) * LANES]
        v_ref[c] = v[:, c * LANES:(c + 1) * LANES]
    gattn_ref[...] = mm(OFF_GATTN, D_ATTN)
    dt_ref[...] = jnp.dot(hb, wdt_ref[...], preferred_element_type=F32)
    if prompt:
        @pl.when(pl.program_id(1) >= first_win_tile)
        def _():
            for dst, val in ((kt_ref, k.T), (vt_ref, v.T)):
                if dst.ndim == 2:
                    dst[...] = val
                else:
                    for l in range(dst.shape[0]):
                        dst[l] = val if l == cache_li else jnp.zeros_like(val)


def _proj(x3, g, w_bf16, wdt_bf16, tm, cache=None):
    B, S, _ = x3.shape
    row = lambda width: pl.BlockSpec((None, tm, width), lambda b, t: (b, t, 0))
    const = lambda r, c: pl.BlockSpec((r, c), lambda b, t: (0, 0))
    pair = pl.BlockSpec((D_ATTN // LANES, None, tm, LANES), lambda b, t: (0, b, t, 0))
    pair_shape = (4, B, S, LANES)
    sds = lambda *s: jax.ShapeDtypeStruct(s, F32)
    in_specs = [row(D_MODEL), const(1, D_MODEL), const(D_MODEL, D_IN), const(D_MODEL, LANES)]
    args = [x3, g, w_bf16, wdt_bf16]
    out_specs = [row(D_LRU), row(D_LRU), pair, pair, pair, row(D_ATTN), row(D_SSD), row(D_XBC), row(LANES)]
    out_shape = [sds(B, S, D_LRU), sds(B, S, D_LRU), sds(*pair_shape), sds(*pair_shape), sds(*pair_shape),
                 sds(B, S, D_ATTN), sds(B, S, D_SSD), sds(B, S, D_XBC), sds(B, S, LANES)]
    first_win_tile, n_alias, aliases, cache_li, scratch = 0, 0, {}, 0, []
    if cache is not None:
        li, depth, win, prev_kt, prev_vt, lcw, lcb, scw, scb = cache
        in_specs += [const(CONV_W, D_LRU), const(1, D_LRU), const(CONV_W, D_XBC), const(1, D_XBC)]
        args += [lcw, lcb, scw, scb]
        first_win_tile = (S - win) // tm
        cache_li = li
        win_tile = lambda t: jnp.maximum(t - first_win_tile, 0)
        if prev_kt is None:
            cspec = pl.BlockSpec((depth, None, D_ATTN, tm), lambda b, t: (0, b, 0, win_tile(t)))
        else:
            cspec = pl.BlockSpec((None, None, D_ATTN, tm), lambda b, t: (li, b, 0, win_tile(t)))
        tail = lambda width: pl.BlockSpec((None, 8, width), lambda b, t: (b, 0, 0))
        out_specs += [cspec, cspec, tail(D_LRU), tail(D_XBC)]
        out_shape += [sds(depth, B, D_ATTN, win), sds(depth, B, D_ATTN, win), sds(B, 8, D_LRU), sds(B, 8, D_XBC)]
        scratch = [pltpu.VMEM((tm + 8, D_LRU), F32), pltpu.VMEM((tm + 8, D_XBC), F32)]
        if prev_kt is not None:
            n_alias = 2
            in_specs += [pl.BlockSpec(memory_space=pl.ANY)] * 2
            args += [prev_kt, prev_vt]
            aliases = {len(args) - 2: 9, len(args) - 1: 10}
    return pl.pallas_call(
        functools.partial(_proj_kernel, first_win_tile=first_win_tile, n_alias=n_alias, cache_li=cache_li,
                          prompt=cache is not None),
        grid=(B, S // tm),
        in_specs=in_specs,
        out_specs=out_specs,
        out_shape=out_shape,
        scratch_shapes=scratch,
        input_output_aliases=aliases,
        compiler_params=_cparams(("arbitrary", "arbitrary")),
        name="in_proj",
    )(*args)


def _lru_gates(xc, wg_ref, bg_ref, lam_ref):
    pre = jnp.dot(xc.astype(BF16), wg_ref[...], preferred_element_type=F32) + bg_ref[...]
    r = jax.nn.sigmoid(pre[:, :D_LRU])
    i = jax.nn.sigmoid(pre[:, D_LRU:])
    log_a = (-LRU_C) * r * _softplus(-lam_ref[...])
    a = jnp.exp(log_a)
    t = jnp.tanh(log_a)
    return a, jnp.sqrt(-2.0 * t / (1.0 - t)) * (i * xc)


def _scan_rows(a, b, h0):
    n, width = a.shape
    a = a.reshape(n // 8, 8, width)
    b = b.reshape(n // 8, 8, width)
    row = lax.broadcasted_iota(jnp.int32, a.shape, 1)
    k = 1
    while k < 8:
        keep = row >= k
        a_sh = jnp.where(keep, pltpu.roll(a, k, axis=1), 1.0)
        b_sh = jnp.where(keep, pltpu.roll(b, k, axis=1), 0.0)
        b = a * b_sh + b
        a = a * a_sh
        k *= 2
    blocks = []
    h = h0
    for j in range(n // 8):
        blk = b[j] + a[j] * h
        blocks.append(blk)
        h = blk[7:8, :]
    return jnp.concatenate(blocks, axis=0), h


def _conv_rows(xe_ref, n, w_ref, b_ref):
    y = b_ref[...] + xe_ref[8:8 + n, :] * w_ref[3:4, :]
    for tap in range(CONV_W - 1):
        y = y + xe_ref[5 + tap:5 + tap + n, :] * w_ref[tap:tap + 1, :]
    return y


def _lru_p_kernel(xc_ref, gate_ref, wg_ref, bg_ref, lam_ref, y_ref, hl_ref, h_ref):
    @pl.when(pl.program_id(1) == 0)
    def _():
        h_ref[...] = jnp.zeros_like(h_ref)

    a, b = _lru_gates(xc_ref[...], wg_ref, bg_ref, lam_ref)
    hs, h_last = _scan_rows(a, b, h_ref[0:1, :])
    h_ref[...] = jnp.broadcast_to(h_last, h_ref.shape)
    hl_ref[...] = h_last
    y_ref[...] = (hs * gate_ref[...]).astype(y_ref.dtype)


def _lru_prompt(xc, gate, wg, bg, lam, tt):
    B, S, _ = xc.shape
    tile = pl.BlockSpec((None, tt, D_LRU), lambda b, t: (b, t, 0))
    const = lambda r, c: pl.BlockSpec((r, c), lambda b, t: (0, 0))
    return pl.pallas_call(
        _lru_p_kernel,
        grid=(B, S // tt),
        in_specs=[tile, tile, const(D_LRU, 2 * D_LRU), const(1, 2 * D_LRU), const(1, D_LRU)],
        out_specs=[tile, pl.BlockSpec((None, 1, D_LRU), lambda b, t: (b, 0, 0))],
        out_shape=[jax.ShapeDtypeStruct((B, S, D_LRU), BF16), jax.ShapeDtypeStruct((B, 1, D_LRU), F32)],
        scratch_shapes=[pltpu.VMEM((8, D_LRU), F32)],
        compiler_params=_cparams(("arbitrary", "arbitrary")),
        name="lru_prompt",
    )(xc, gate, wg, bg, lam)


def _lru_s_kernel(x_ref, g_ref, cs_ref, h0_ref, cw_ref, cb_ref, wg_ref, bg_ref, lam_ref, y_ref, hl_ref):
    L = x_ref.shape[0]
    db = x_ref.shape[1]
    xs = [cs_ref[i] for i in range(CONV_W - 1)] + [x_ref[i] for i in range(L)]
    xc = []
    for t in range(L):
        y = cb_ref[...] + xs[t] * cw_ref[0:1, :]
        for tap in range(1, CONV_W):
            y = y + xs[t + tap] * cw_ref[tap:tap + 1, :]
        xc.append(y)
    a, b = _lru_gates(jnp.concatenate(xc, axis=0), wg_ref, bg_ref, lam_ref)
    h = h0_ref[...]
    for t in range(L):
        h = a[t * db:(t + 1) * db] * h + b[t * db:(t + 1) * db]
        y_ref[t] = h * _silu(g_ref[t])
    hl_ref[...] = h


def _lru_sample(x_tm, g_tm, cs_tm, h0, cw, cb, wg, bg, lam):
    L, DB, _ = x_tm.shape
    return pl.pallas_call(
        _lru_s_kernel,
        out_shape=[jax.ShapeDtypeStruct((L, DB, D_LRU), F32), jax.ShapeDtypeStruct((DB, D_LRU), F32)],
        compiler_params=pltpu.CompilerParams(vmem_limit_bytes=VMEM_LIMIT),
        name="lru_sample",
    )(x_tm, g_tm, cs_tm, h0, cw, cb, wg, bg, lam)


def _block_order(rho, dil):
    if dil == 16:
        return rho
    if dil == 4:
        return 4 * (rho & 31) + (rho >> 5)
    return 16 * (rho & 7) + (rho >> 3)


def _chunks(dil):
    return {16: (1, 128), 4: (4, 32), 1: (16, 8)}[dil]


def _attn_p_kernel(q_ref, k_ref, v_ref, o_ref, qs, ks, vs, r_st, m_st, bias_ref, *, unroll):
    c = pl.program_id(1)
    s = pl.program_id(2)
    nres = SPAN // ATTN_BLOCK
    lane_q = lax.broadcasted_iota(jnp.int32, (ATTN_BLOCK, LANES), 1)
    first_half = lane_q < HEAD_DIM

    cur = s & 1
    prev = 1 - cur

    @pl.when(s == 0)
    def _():
        ks[prev] = jnp.zeros((SPAN, LANES), F32)
        vs[0, prev] = jnp.zeros((SPAN, LANES), F32)
        vs[1, prev] = jnp.zeros((SPAN, LANES), F32)

    for r in range(nres):
        rows = pl.ds(r, ATTN_BLOCK, stride=nres)
        dst = slice(r * ATTN_BLOCK, (r + 1) * ATTN_BLOCK)
        q = q_ref[rows, :]
        v = v_ref[rows, :]
        qs[0, dst, :] = jnp.where(first_half, q, 0.0)
        qs[1, dst, :] = jnp.where(first_half, 0.0, q)
        ks[cur, dst, :] = k_ref[rows, :]
        vs[0, cur, dst, :] = jnp.where(first_half, v, 1.0)
        vs[1, cur, dst, :] = jnp.where(first_half, 1.0, v)

    span0 = s == 0

    @pl.when(span0)
    def _():
        iq = lax.broadcasted_iota(jnp.int32, (ATTN_BLOCK, 2 * ATTN_BLOCK), 0)
        ik = lax.broadcasted_iota(jnp.int32, (ATTN_BLOCK, 2 * ATTN_BLOCK), 1)
        in_cur = ik >= ATTN_BLOCK
        for di, (window, dil) in enumerate(DILATIONS):
            j = ATTN_BLOCK + _block_order(iq, dil) - (
                jnp.where(in_cur, ATTN_BLOCK, 0) + _block_order(ik & (ATTN_BLOCK - 1), dil))
            valid = (j >= 0) & (j <= window // dil)
            dist = (j * dil).astype(F32)
            dist_all = jnp.where(valid, dist, MASKED)
            dist_cur = jnp.where(valid & in_cur, dist, MASKED)
            for hh in range(2):
                slope = jnp.exp2(-(jnp.zeros((1, 1), F32) + (2 * c + hh + 1).astype(F32)))
                bias_ref[di, hh, 0] = -slope * dist_all
                bias_ref[di, hh, 1] = -slope * dist_cur

    def block(di, dil, it):
        nchunk, crow = _chunks(dil)

        def gather(ref, lead, starts):
            parts = [ref[lead + (pl.ds(st, crow), slice(None))] for st in starts]
            return parts[0] if len(parts) == 1 else jnp.concatenate(parts, axis=0)

        if dil == 16:
            base, cb, last_off = it * ATTN_BLOCK, 0, 0
        elif dil == 4:
            base, cb, last_off = (it >> 2) * ATTN_BLOCK, it & 3, 3 * crow
        else:
            base, cb, last_off = 0, it, 15 * crow
        stride = {16: 0, 4: 4 * ATTN_BLOCK, 1: ATTN_BLOCK}[dil]
        cur_starts = [pl.multiple_of(base + a * stride + crow * cb, 8) for a in range(nchunk)]
        if dil == 16:
            prev_slot = prev
            prev_starts = cur_starts
            first_in_span = True
        else:
            has_prev = cb > 0
            prev_slot = jnp.where(has_prev, cur, prev)
            prev_off = jnp.where(has_prev, crow * (cb - 1), last_off)
            prev_starts = [pl.multiple_of(base + a * stride + prev_off, 8) for a in range(nchunk)]
            first_in_span = jnp.logical_not(has_prev)
        kb = jnp.concatenate([gather(ks, (prev_slot,), prev_starts), gather(ks, (cur,), cur_starts)],
                             axis=0).astype(BF16)
        variant = jnp.where(jnp.logical_and(span0, first_in_span), 1, 0)
        for hh in range(2):
            qh = gather(qs, (hh,), cur_starts).astype(BF16)
            vh = jnp.concatenate([gather(vs, (hh, prev_slot), prev_starts), gather(vs, (hh, cur), cur_starts)],
                                 axis=0).astype(BF16)
            sc = _dot_nt(qh, kb) + bias_ref[di, hh, variant]
            m = jnp.max(sc, axis=1, keepdims=True)
            p = jnp.exp(sc - m)
            r_new = jnp.dot(p.astype(BF16), vh, preferred_element_type=F32)
            m_new = jnp.broadcast_to(m, (ATTN_BLOCK, LANES))
            for a, st in enumerate(cur_starts):
                r_st[di, hh, pl.ds(st, crow), :] = r_new[a * crow:(a + 1) * crow, :]
                m_st[di, hh, pl.ds(st, crow), :] = m_new[a * crow:(a + 1) * crow, :]

    def body(it, carry):
        for di, (_, dil) in enumerate(DILATIONS):
            block(di, dil, it)
        return carry

    lax.fori_loop(0, nres, body, 0, unroll=unroll)

    ndil = len(DILATIONS)
    for r in range(nres):
        src = slice(r * ATTN_BLOCK, (r + 1) * ATTN_BLOCK)
        outs = []
        for hh in range(2):
            ms = [m_st[di, hh, src, :] for di in range(ndil)]
            m_tot = functools.reduce(jnp.maximum, ms)
            tot = sum(r_st[di, hh, src, :] * jnp.exp(ms[di] - m_tot) for di in range(ndil))
            outs.append(tot / pltpu.roll(tot, HEAD_DIM, axis=1))
        o_ref[pl.ds(r, ATTN_BLOCK, stride=nres), :] = jnp.where(first_half, outs[0], outs[1])


def _attn_prompt(q4, k4, v4, unroll=8):
    npair, B, S, _ = q4.shape
    ndil = len(DILATIONS)
    spec = pl.BlockSpec((None, None, SPAN, LANES), lambda b, c, s: (c, b, s, 0))
    return pl.pallas_call(
        functools.partial(_attn_p_kernel, unroll=unroll),
        grid=(B, npair, S // SPAN),
        in_specs=[spec, spec, spec],
        out_specs=spec,
        out_shape=jax.ShapeDtypeStruct(q4.shape, F32),
        scratch_shapes=[pltpu.VMEM((2, SPAN, LANES), F32), pltpu.VMEM((2, SPAN, LANES), F32),
                        pltpu.VMEM((2, 2, SPAN, LANES), F32),
                        pltpu.VMEM((ndil, 2, SPAN, LANES), F32), pltpu.VMEM((ndil, 2, SPAN, LANES), F32),
                        pltpu.VMEM((ndil, 2, 2, ATTN_BLOCK, 2 * ATTN_BLOCK), F32)],
        compiler_params=_cparams(("arbitrary", "arbitrary", "arbitrary")),
        name="attn_prompt",
    )(q4, k4, v4)


def _attn_s_kernel(*refs, n_new):
    _attn_s_body(*refs, n_new=n_new, first=pl.program_id(0) == 0)


def _attn_s_body(q_ref, kn_ref, vn_ref, kc_ref, vc_ref, o_ref, dist_ref, mult_ref, *, n_new, first):
    nseq, _, _, W = kc_ref.shape
    rows = q_ref.shape[1]
    shift = n_new.bit_length() - 1

    @pl.when(first)
    def _():
        i = lax.broadcasted_iota(jnp.int32, (rows, W), 0) & (n_new - 1)
        pos = lax.broadcasted_iota(jnp.int32, (rows, W), 1)
        d = W + i - pos
        cnt = jnp.zeros((rows, W), F32)
        for window, dil in DILATIONS:
            cnt = cnt + jnp.where(((d & (dil - 1)) == 0) & (d <= window), 1.0, 0.0)
        dist_ref[...] = d.astype(F32)
        mult_ref[...] = cnt

    r = lax.broadcasted_iota(jnp.int32, (rows, rows), 0)
    r2 = lax.broadcasted_iota(jnp.int32, (rows, rows), 1)
    dn = (r & (n_new - 1)) - (r2 & (n_new - 1))
    same_seq = (r >> shift) == (r2 >> shift)
    mult_n = jnp.zeros((rows, rows), F32)
    for window, dil in DILATIONS:
        ok = ((dn & (dil - 1)) == 0) & (dn >= 0) & (dn <= window) & same_seq
        mult_n = mult_n + jnp.where(ok, 1.0, 0.0)
    dist_n = dn.astype(F32)
    dist_c = dist_ref[...]
    mult_c = mult_ref[...]
    row_seq = lax.broadcasted_iota(jnp.int32, (rows, HEAD_DIM), 0) >> shift
    for c in range(D_ATTN // LANES):
        halves = []
        for hh in range(2):
            h = 2 * c + hh
            slope = 2.0 ** (-(h + 1))
            lanes = slice(hh * HEAD_DIM, (hh + 1) * HEAD_DIM)
            q = q_ref[c][:, lanes].astype(BF16)
            vn = vn_ref[c][:, lanes].astype(BF16)
            sn = _dot_nt(q, kn_ref[c][:, lanes].astype(BF16))
            sn = jnp.where(mult_n > 0, sn - slope * dist_n, NEG)
            mn = jnp.max(sn, axis=1, keepdims=True)
            o_h = jnp.zeros((rows, HEAD_DIM), F32)
            for b in range(nseq):
                sc = jnp.dot(q, kc_ref[b, h].astype(BF16), preferred_element_type=F32)
                sc = jnp.where(mult_c > 0, sc - slope * dist_c, NEG)
                m = jnp.maximum(jnp.max(sc, axis=1, keepdims=True), mn)
                pc = mult_c * jnp.exp(sc - m)
                pn = mult_n * jnp.exp(sn - m)
                l = jnp.sum(pc, axis=1, keepdims=True) + jnp.sum(pn, axis=1, keepdims=True)
                o = _dot_nt(pc.astype(BF16), vc_ref[b, h].astype(BF16)) + jnp.dot(
                    pn.astype(BF16), vn, preferred_element_type=F32)
                o_h = jnp.where(row_seq == b, o / l, o_h)
            halves.append(o_h)
        o_ref[c] = jnp.concatenate(halves, axis=1)


def _attn_sample(q4, k4, v4, kc_t, vc_t, li, n_new):
    small, cache, tables = _attn_s_specs(q4, kc_t, li, n_new, lambda b: b)
    return pl.pallas_call(
        functools.partial(_attn_s_kernel, n_new=n_new),
        grid=(q4.shape[1] // 8,),
        in_specs=[small, small, small, cache, cache],
        out_specs=small,
        out_shape=jax.ShapeDtypeStruct(q4.shape, F32),
        scratch_shapes=tables,
        compiler_params=_cparams(("arbitrary",)),
        name="attn_sample",
    )(q4, k4, v4, kc_t, vc_t)


def _attn_s_specs(q4, kc_t, li, n_new, group_of):
    npair, T, _ = q4.shape
    W = kc_t.shape[-1]
    rows = 8
    assert n_new & (n_new - 1) == 0 and rows % n_new == 0 and T % rows == 0
    small = pl.BlockSpec((npair, rows, LANES), lambda *g: (0, group_of(*g), 0))
    cache = pl.BlockSpec((None, rows // n_new, N_HEADS, HEAD_DIM, W), lambda *g: (li, group_of(*g), 0, 0, 0))
    return small, cache, [pltpu.VMEM((rows, W), F32), pltpu.VMEM((rows, W), F32)]


def _group_norm_out(y, z, ng_ref, gate_ready=False):
    yg = y * (z if gate_ready else _silu(z))
    outs = []
    for g in range(SSD_GROUPS):
        part = yg[:, g * GROUP_W:(g + 1) * GROUP_W]
        outs.append(part * lax.rsqrt(jnp.mean(part * part, axis=-1, keepdims=True) + EPS))
    return jnp.concatenate(outs, axis=1) * ng_ref[...]


def _ssd_p_kernel(*refs, n_new):
    fused = n_new is not None
    n_in = 13 if fused else 8
    y_ref, hl_ref = refs[n_in:n_in + 2]
    scratch = refs[n_in + (3 if fused else 2):]
    _ssd_p_tile(*refs[:8], y_ref, hl_ref, scratch[0])
    if fused:
        first = jnp.logical_and(pl.program_id(0) == 0, pl.program_id(1) == 0)
        _attn_s_body(*refs[8:13], refs[n_in + 2], scratch[1], scratch[2], n_new=n_new, first=first)


def _ssd_p_tile(xc_ref, dt_ref, gate_ref, dtb_ref, alog_ref, dx_ref, ng_ref, e_ref, y_ref, hl_ref, ht_ref):
    t = pl.program_id(1)
    Q = SSD_CHUNK
    n = xc_ref.shape[0]

    @pl.when(t == 0)
    def _():
        ht_ref[...] = jnp.zeros_like(ht_ref)

    xc_all = xc_ref[...]
    dt_all = _softplus(dt_ref[...] + dtb_ref[...])
    a_neg = -jnp.exp(alog_ref[...])
    ri = lax.broadcasted_iota(jnp.int32, (Q, Q), 0)
    ci = lax.broadcasted_iota(jnp.int32, (Q, Q), 1)
    causal = ri >= ci
    tri = jnp.where(causal, 1.0, 0.0).astype(BF16)
    lane = lax.broadcasted_iota(jnp.int32, (Q, LANES), 1)
    for ck in range(n // Q):
        rows = slice(ck * Q, (ck + 1) * Q)
        y = _ssd_chunk(xc_all[rows, :], dt_all[rows, :], a_neg, causal, tri, lane, dx_ref, e_ref, ht_ref)
        y_ref[rows, :] = _group_norm_out(y, gate_ref[rows, :], ng_ref, gate_ready=True).astype(y_ref.dtype)

    @pl.when(t == pl.num_programs(1) - 1)
    def _():
        for g in range(SSD_GROUPS):
            hl_ref[g * (SSD_HEADS // SSD_GROUPS):(g + 1) * (SSD_HEADS // SSD_GROUPS)] = (
                ht_ref[g].T.reshape(SSD_HEADS // SSD_GROUPS, SSD_P, SSD_N))


def _ssd_chunk(xc, dt, a_neg, causal, tri, lane, dx_ref, e_ref, ht_ref):
    Q = SSD_CHUNK
    xs = xc[:, :D_SSD]
    bm = xc[:, D_SSD:D_SSD + SSD_GROUPS * SSD_N]
    cm = xc[:, D_SSD + SSD_GROUPS * SSD_N:]
    cum = _dot3_left(tri, dt * a_neg)
    cum_t = cum.T
    both_x = _dot3_right(jnp.concatenate([cum, dt], axis=0), e_ref[...])
    cum_x = both_x[:Q]
    dt_x = both_x[Q:]
    xdt = xs * dt_x
    ecum = jnp.exp(cum_x)
    cl = cum_x[Q - 1:Q, :]
    w_end = xdt * jnp.exp(cl - cum_x)
    dec = jnp.exp(cl)

    y_parts = []
    for g in range(SSD_GROUPS):
        cg = cm[:, g * SSD_N:(g + 1) * SSD_N].astype(BF16)
        bg = bm[:, g * SSD_N:(g + 1) * SSD_N]
        cb = _dot_nt(cg, bg.astype(BF16))
        gl = slice(g * GROUP_W, (g + 1) * GROUP_W)
        h_old = ht_ref[g]
        y_off = jnp.dot(cg, h_old.astype(BF16), preferred_element_type=F32) * ecum[:, gl]
        ht_ref[g] = h_old * dec[:, gl] + jnp.dot(bg.T.astype(BF16), w_end[:, gl].astype(BF16),
                                                  preferred_element_type=F32)
        for pp in range(GROUP_W // LANES):
            pl_ = slice(g * GROUP_W + pp * LANES, g * GROUP_W + (pp + 1) * LANES)
            xdt_pair = xdt[:, pl_].astype(BF16)
            halves = []
            for hh in range(2):
                h = (g * GROUP_W + pp * LANES) // SSD_P + hh
                seg = jnp.broadcast_to(cum[:, h:h + 1], (Q, Q)) - jnp.broadcast_to(cum_t[h:h + 1, :], (Q, Q))
                scores = cb * jnp.exp(jnp.where(causal, seg, NEG))
                halves.append(jnp.dot(scores.astype(BF16), xdt_pair, preferred_element_type=F32))
            y_diag = jnp.where(lane < SSD_P, halves[0], halves[1])
            y_parts.append(y_diag + y_off[:, pp * LANES:(pp + 1) * LANES])
    return jnp.concatenate(y_parts, axis=1) + dx_ref[...] * xs


def _ssd_prompt(xc, dt, gate, dtb, alog, dx, ng, e, sample_attn=None):
    B, S, _ = xc.shape
    Q = SSD_PROMPT_CHUNKS * SSD_CHUNK
    nt = S // Q
    tile = lambda w: pl.BlockSpec((None, Q, w), lambda b, t: (b, t, 0))
    const = lambda r, c: pl.BlockSpec((r, c), lambda b, t: (0, 0))
    in_specs = [tile(D_XBC), tile(LANES), tile(D_SSD), const(1, LANES), const(1, LANES), const(1, D_SSD),
                const(1, D_SSD), const(3 * LANES, D_SSD)]
    args = [xc, dt, gate, dtb, alog, dx, ng, e]
    out_specs = [tile(D_SSD), pl.BlockSpec((None, SSD_HEADS, SSD_P, SSD_N), lambda b, t: (b, 0, 0, 0))]
    out_shape = [jax.ShapeDtypeStruct((B, S, D_SSD), BF16), jax.ShapeDtypeStruct((B, SSD_HEADS, SSD_P, SSD_N), F32)]
    scratch = [pltpu.VMEM((SSD_GROUPS, SSD_N, GROUP_W), F32)]
    n_new = None
    if sample_attn is not None:
        q4, k4, v4, kc_t, vc_t, li, n_new = sample_attn
        small, cache, tables = _attn_s_specs(q4, kc_t, li, n_new, lambda b, t: b * nt + t)
        assert q4.shape[1] // 8 == B * nt
        in_specs += [small, small, small, cache, cache]
        args += [q4, k4, v4, kc_t, vc_t]
        out_specs.append(small)
        out_shape.append(jax.ShapeDtypeStruct(q4.shape, F32))
        scratch += tables
    return pl.pallas_call(
        functools.partial(_ssd_p_kernel, n_new=n_new),
        grid=(B, nt),
        in_specs=in_specs,
        out_specs=out_specs,
        out_shape=out_shape,
        scratch_shapes=scratch,
        compiler_params=_cparams(("arbitrary", "arbitrary")),
        name="ssd_prompt" if sample_attn is None else "ssd_prompt_attn_sample",
    )(*args)


def _ssd_s_kernel(xbc_ref, cs_ref, dt_ref, z_ref, h_ref, cw_ref, cb_ref, dtb_ref, alog_ref, dx_ref, ng_ref, e_ref,
                  *rest, n_new, li):
    y_ref, hn_ref = rest[-2:]
    if hn_ref.ndim == 5:
        for l in range(hn_ref.shape[0]):
            if l != li:
                hn_ref[l] = jnp.zeros(hn_ref.shape[1:], F32)
        hn_ref = hn_ref.at[li]
    for i in range(xbc_ref.shape[0]):
        _ssd_s_one(xbc_ref.at[i], cs_ref.at[i], dt_ref.at[i], z_ref.at[i], h_ref.at[i], cw_ref, cb_ref, dtb_ref,
                   alog_ref, dx_ref, ng_ref, e_ref, y_ref.at[i], hn_ref.at[i], n_new)


def _ssd_s_one(xbc_ref, cs_ref, dt_ref, z_ref, h_ref, cw_ref, cb_ref, dtb_ref, alog_ref, dx_ref, ng_ref, e_ref,
               y_ref, hn_ref, n_new):
    R = 8
    row = lax.broadcasted_iota(jnp.int32, (R, LANES), 0)
    real = row < n_new
    xe = jnp.concatenate([cs_ref[...], xbc_ref[...]], axis=0)
    y = cb_ref[...] + xe[8:16, :] * cw_ref[3:4, :]
    for tap in range(CONV_W - 1):
        y = y + xe[5 + tap:13 + tap, :] * cw_ref[tap:tap + 1, :]
    xc = _silu(y)
    xs = xc[:, :D_SSD]
    bm = xc[:, D_SSD:D_SSD + SSD_GROUPS * SSD_N]
    cm = xc[:, D_SSD + SSD_GROUPS * SSD_N:]

    dt = jnp.where(real, _softplus(dt_ref[...] + dtb_ref[...]), 0.0)
    da = dt * (-jnp.exp(alog_ref[...]))
    cum = da
    k = 1
    while k < R:
        cum = cum + jnp.where(row >= k, pltpu.roll(cum, k, axis=0), 0.0)
        k *= 2
    e = e_ref[...]
    cum_x = _dot3_right(cum, e)
    dt_x = _dot3_right(dt, e)
    xdt = xs * dt_x
    ecum = jnp.exp(cum_x)
    cl = cum_x[R - 1:R, :]
    w_end = xdt * jnp.exp(cl - cum_x)
    dec3 = _split3(jnp.exp(cl))
    ones = jnp.ones((R, SSD_N), BF16)
    zrow = jnp.zeros((R - 3, GROUP_W), BF16)

    y_parts = []
    for g in range(SSD_GROUPS):
        gl = slice(g * GROUP_W, (g + 1) * GROUP_W)
        cg = cm[:, g * SSD_N:(g + 1) * SSD_N]
        bg = bm[:, g * SSD_N:(g + 1) * SSD_N]
        hs = slice(g * (SSD_HEADS // SSD_GROUPS), (g + 1) * (SSD_HEADS // SSD_GROUPS))
        h_old = h_ref[hs].reshape(GROUP_W, SSD_N)
        y_g = _dot_nt(cg.astype(BF16), h_old.astype(BF16)) * ecum[:, gl]
        for j in range(n_new):
            cbj = jnp.sum(cg * bg[j:j + 1, :], axis=1, keepdims=True)
            term = cbj * jnp.exp(cum_x[:, gl] - cum_x[j:j + 1, gl]) * xdt[j:j + 1, gl]
            y_g = y_g + jnp.where(row[:, 0:1] >= j, term, 0.0)
        y_parts.append(y_g)
        dec_col = _dot_tn(jnp.concatenate([p[:, gl] for p in dec3] + [zrow], axis=0), ones)
        upd = _dot_tn(w_end[:, gl].astype(BF16), bg.astype(BF16))
        hn_ref[hs] = (h_old * dec_col + upd).reshape(SSD_HEADS // SSD_GROUPS, SSD_P, SSD_N)
    y = jnp.concatenate(y_parts, axis=1) + dx_ref[...] * xs
    y_ref[...] = _group_norm_out(y, z_ref[...], ng_ref)


def _ssd_sample(xbc, cs, dt, z, h_all, li, h_new_prev, cw, cb, dtb, alog, dx, ng, e, n_new):
    DB = xbc.shape[0]
    bb = SSD_SAMPLE_SEQS if DB % SSD_SAMPLE_SEQS == 0 else 1
    tile = lambda w: pl.BlockSpec((bb, 8, w), lambda b: (b, 0, 0))
    const = lambda r, c: pl.BlockSpec((r, c), lambda b: (0, 0))
    state = pl.BlockSpec((None, bb, SSD_HEADS, SSD_P, SSD_N), lambda b: (li, b, 0, 0, 0))
    in_specs = [tile(D_XBC), tile(D_XBC), tile(LANES), tile(D_SSD), state, const(CONV_W, D_XBC), const(1, D_XBC),
                const(1, LANES), const(1, LANES), const(1, D_SSD), const(1, D_SSD), const(3 * LANES, D_SSD)]
    args = [xbc, cs, dt, z, h_all, cw, cb, dtb, alog, dx, ng, e]
    aliases = {}
    if h_new_prev is not None:
        in_specs.append(pl.BlockSpec(memory_space=pl.ANY))
        args.append(h_new_prev)
        aliases = {len(args) - 1: 1}
        state_out = state
    else:
        state_out = pl.BlockSpec((h_all.shape[0], bb, SSD_HEADS, SSD_P, SSD_N), lambda b: (0, b, 0, 0, 0))
    return pl.pallas_call(
        functools.partial(_ssd_s_kernel, n_new=n_new, li=li),
        grid=(DB // bb,),
        in_specs=in_specs,
        out_specs=[tile(D_SSD), state_out],
        out_shape=[jax.ShapeDtypeStruct((DB, 8, D_SSD), F32), jax.ShapeDtypeStruct(h_all.shape, F32)],
        input_output_aliases=aliases,
        compiler_params=_cparams(("arbitrary",)),
        name="ssd_sample",
    )(*args)


def _out_kernel(x_ref, ylru_ref, o_ref, gattn_ref, yssd_ref, w_ref, g_ref, out_ref):
    o = jnp.concatenate([o_ref[c] for c in range(D_ATTN // LANES)], axis=1)
    y_attn = o * _silu(gattn_ref[...])
    mix_in = jnp.concatenate([ylru_ref[...].astype(BF16), y_attn.astype(BF16), yssd_ref[...].astype(BF16)], axis=1)
    mix = jnp.dot(mix_in, w_ref[...], preferred_element_type=F32)
    out_ref[...] = x_ref[...] + mix * lax.rsqrt(jnp.mean(mix * mix, axis=-1, keepdims=True) + EPS) * g_ref[...]


def _out_proj(x2d, y_lru, o4, g_attn, y_ssd, w_bf16, g, tm):
    T = x2d.shape[0]
    row = lambda width: pl.BlockSpec((tm, width), lambda i: (i, 0))
    o_spec = pl.BlockSpec((D_ATTN // LANES, tm, LANES), lambda i: (0, i, 0))
    return pl.pallas_call(
        _out_kernel,
        grid=(T // tm,),
        in_specs=[row(D_MODEL), row(D_LRU), o_spec,
                  row(D_ATTN), row(D_SSD), pl.BlockSpec((D_LRU + D_ATTN + D_SSD, D_MODEL), lambda i: (0, 0)),
                  pl.BlockSpec((1, D_MODEL), lambda i: (0, 0))],
        out_specs=row(D_MODEL),
        out_shape=jax.ShapeDtypeStruct((T, D_MODEL), F32),
        compiler_params=_cparams(("arbitrary",)),
        name="out_proj",
    )(x2d, y_lru, o4, g_attn, y_ssd, w_bf16, g)


def _block_diag(w):
    k, n, _ = w.shape
    return jnp.einsum("kij,kl->kilj", w, jnp.eye(k, dtype=w.dtype)).reshape(k * n, k * n)


def _pad_rows(a, rows):
    return jnp.pad(a, ((0, 0), (0, rows - a.shape[1]), (0, 0)))


def _pairs_to_heads(a4, nb, L):
    return a4.reshape(4, nb, L, 2, HEAD_DIM).transpose(1, 2, 0, 3, 4).reshape(nb, L, N_HEADS, HEAD_DIM)


def kernel(x_prompt, x_sample, state_lru_conv, state_lru_h, cache_attn_k, cache_attn_v, state_ssd_conv, state_ssd_h, pre_norm_g, post_norm_g, w_in, lru_conv_w, lru_conv_b, lru_w_a, lru_b_a, lru_w_x, lru_b_x, lru_lambda, ssd_conv_w, ssd_conv_b, ssd_dt_bias, ssd_a_log, ssd_d, ssd_norm_g, w_out):
    depth = w_in.shape[0]
    B, S, _ = x_prompt.shape
    DB, L, _ = x_sample.shape
    assert S % SPAN == 0 and L <= 8 and L >= CONV_W - 1
    win = min(MAX_WINDOW, S)

    lane_pad = lambda a: jnp.pad(a.reshape(1, -1), ((0, 0), (0, LANES - a.shape[-1])))
    expand = jnp.asarray(np.tile(np.repeat(np.eye(LANES, SSD_HEADS, dtype=np.float32), SSD_P, axis=1), (3, 1)), BF16)
    kc_t = jnp.transpose(cache_attn_k, (0, 1, 3, 4, 2))
    vc_t = jnp.transpose(cache_attn_v, (0, 1, 3, 4, 2))

    yp = x_prompt.reshape(B * S, D_MODEL)
    ys = x_sample.reshape(DB * L, D_MODEL)
    outs_p = []
    outs_s = []
    kt_p = vt_p = ssd_h_s = None
    for li in range(depth):
        w_in_b = w_in[li].astype(BF16)
        w_dt_b = jnp.pad(w_in[li][:, OFF_DT:], ((0, 0), (0, D_IN_PAD - D_IN))).astype(BF16)
        w_out_b = w_out[li].astype(BF16)
        pre_g = pre_norm_g[li].reshape(1, -1)
        post_g = post_norm_g[li].reshape(1, -1)
        wg = jnp.concatenate([_block_diag(lru_w_a[li]), _block_diag(lru_w_x[li])], axis=1).astype(BF16)
        bg = jnp.concatenate([lru_b_a[li], lru_b_x[li]]).reshape(1, -1)
        lcw, lcb, lam = lru_conv_w[li], lru_conv_b[li].reshape(1, -1), lru_lambda[li].reshape(1, -1)
        scw, scb = ssd_conv_w[li], ssd_conv_b[li].reshape(1, -1)
        dtb, alog = lane_pad(ssd_dt_bias[li]), lane_pad(ssd_a_log[li])
        dx = jnp.repeat(ssd_d[li], SSD_P).reshape(1, -1)
        ng = ssd_norm_g[li].reshape(1, -1)

        xc_lru, gate_lru, q4, k4, v4, gattn, gate_z, xc_ssd, dtr, kt_p, vt_p, tail_lru, tail_xbc = _proj(
            yp.reshape(B, S, D_MODEL), pre_g, w_in_b, w_dt_b, 256,
            cache=(li, depth, win, kt_p, vt_p, lcw, lcb, scw, scb))
        xlru_s, glru_s, q4s, k4s, v4s, gattn_s, z_s, xbc_s, dtr_s = [
            a.reshape(a.shape[:-3] + (DB * L, a.shape[-1]))
            for a in _proj(ys.reshape(1, DB * L, D_MODEL), pre_g, w_in_b, w_dt_b, min(256, DB * L))]

        y_lru, lru_h_p = _lru_prompt(xc_lru, gate_lru, wg, bg, lam, 512)
        o4 = _attn_prompt(q4, k4, v4)
        ssd_steps = B * (S // (SSD_PROMPT_CHUNKS * SSD_CHUNK))
        ride = L & (L - 1) == 0 and 8 % L == 0 and (DB * L) % 8 == 0 and (DB * L) // 8 == ssd_steps
        if ride:
            y_ssd, ssd_h_p, o_s = _ssd_prompt(xc_ssd, dtr, gate_z, dtb, alog, dx, ng, expand,
                                              sample_attn=(q4s, k4s, v4s, kc_t, vc_t, li, L))
        else:
            y_ssd, ssd_h_p = _ssd_prompt(xc_ssd, dtr, gate_z, dtb, alog, dx, ng, expand)
            o_s = _attn_sample(q4s, k4s, v4s, kc_t, vc_t, li, L)
        yp = _out_proj(yp, y_lru.reshape(B * S, D_LRU), o4.reshape(4, B * S, LANES), gattn.reshape(B * S, D_ATTN),
                       y_ssd.reshape(B * S, D_SSD), w_out_b, post_g, 512)
        outs_p.append((tail_lru[:, 8 - (CONV_W - 1):], lru_h_p.reshape(B, D_LRU), tail_xbc[:, 8 - (CONV_W - 1):],
                       ssd_h_p))

        xlru3 = xlru_s.reshape(DB, L, D_LRU)
        y_lru_tm, lru_h_s = _lru_sample(
            xlru3.transpose(1, 0, 2), glru_s.reshape(DB, L, D_LRU).transpose(1, 0, 2),
            state_lru_conv[li].transpose(1, 0, 2), state_lru_h[li], lcw, lcb, wg, bg, lam)
        y_lru = y_lru_tm.transpose(1, 0, 2).reshape(DB * L, D_LRU)

        xbc3 = xbc_s.reshape(DB, L, D_XBC)
        cs8 = jnp.pad(state_ssd_conv[li], ((0, 0), (8 - (CONV_W - 1), 0), (0, 0)))
        y_ssd8, ssd_h_s = _ssd_sample(_pad_rows(xbc3, 8), cs8, _pad_rows(dtr_s.reshape(DB, L, LANES), 8),
                                      _pad_rows(z_s.reshape(DB, L, D_SSD), 8), state_ssd_h, li, ssd_h_s, scw, scb,
                                      dtb, alog, dx, ng, expand, L)
        y_ssd = y_ssd8[:, :L].reshape(DB * L, D_SSD)
        ys = _out_proj(ys, y_lru, o_s, gattn_s, y_ssd, w_out_b, post_g, DB * L)
        outs_s.append((xlru3[:, L - (CONV_W - 1):], lru_h_s, _pairs_to_heads(k4s, DB, L),
                       _pairs_to_heads(v4s, DB, L), xbc3[:, L - (CONV_W - 1):]))

    lru_conv_p, lru_h_p, ssd_conv_p, ssd_h_p = [jnp.stack(c) for c in zip(*outs_p)]
    lru_conv_s, lru_h_s, k_s, v_s, ssd_conv_s = [jnp.stack(c) for c in zip(*outs_s)]
    k_p = kt_p.reshape(depth, B, N_HEADS, HEAD_DIM, win).transpose(0, 1, 4, 2, 3)
    v_p = vt_p.reshape(depth, B, N_HEADS, HEAD_DIM, win).transpose(0, 1, 4, 2, 3)
    return (yp.reshape(B, S, D_MODEL), ys.reshape(DB, L, D_MODEL), lru_conv_p, lru_conv_s, lru_h_p, lru_h_s,
            k_p, k_s, v_p, v_s, ssd_conv_p, ssd_conv_s, ssd_h_p, ssd_h_s)
```

```python
import functools

import numpy as np
import jax
import jax.numpy as jnp
from jax import lax
from jax.experimental import pallas as pl
from jax.experimental.pallas import tpu as pltpu

F32 = jnp.float32
BF16 = jnp.bfloat16

D_MODEL = 1024
CONV_W = 4
EPS = 1e-6
D_LRU = 512
LRU_BLOCKS = 8
LRU_C = 8.0
HEAD_DIM = 64
D_ATTN = 512
N_HEADS = 8
DILATIONS = ((128, 1), (512, 4), (2048, 16))
MAX_WINDOW = 2048
ATTN_BLOCK = 128
D_SSD = 1024
SSD_P = 64
SSD_HEADS = 16
SSD_GROUPS = 2
SSD_N = 128
SSD_CHUNK = 128
D_XBC = D_SSD + 2 * SSD_GROUPS * SSD_N
GROUP_W = D_SSD // SSD_GROUPS

OFF_XLRU, OFF_GLRU, OFF_Q, OFF_K, OFF_V, OFF_GATTN, OFF_Z, OFF_XBC, OFF_DT = (
    0, 512, 1024, 1536, 2048, 2560, 3072, 4096, 5632)
D_IN = 5648
LANES = 128
D_IN_PAD = OFF_DT + LANES

SPAN = 2048
NRES = SPAN // ATTN_BLOCK
SSD_PROMPT_CHUNKS = 2
SSD_SAMPLE_SEQS = 4
MASKED = 1e33
NEG = -1e30

VMEM_LIMIT = 56 * 1024 * 1024


def _cparams(sem):
    return pltpu.CompilerParams(dimension_semantics=sem, vmem_limit_bytes=VMEM_LIMIT)


def _silu(x):
    return x * jax.nn.sigmoid(x)


def _softplus(x):
    return jnp.maximum(x, 0.0) + jnp.log1p(jnp.exp(-jnp.abs(x)))


def _split3(x):
    h1 = x.astype(BF16)
    r1 = x - h1.astype(F32)
    h2 = r1.astype(BF16)
    h3 = (r1 - h2.astype(F32)).astype(BF16)
    return h1, h2, h3


def _dot3_right(x, m3_bf16):
    return jnp.dot(jnp.concatenate(_split3(x), axis=1), m3_bf16, preferred_element_type=F32)


def _dot3_left(m_bf16, x):
    return jnp.dot(jnp.concatenate([m_bf16] * 3, axis=1), jnp.concatenate(_split3(x), axis=0),
                   preferred_element_type=F32)


def _dot_nt(a, b):
    return lax.dot_general(a, b, (((1,), (1,)), ((), ())), preferred_element_type=F32)


def _dot_tn(a, b):
    return lax.dot_general(a, b, (((0,), (0,)), ((), ())), preferred_element_type=F32)


def _proj_kernel(x_ref, g_ref, w_ref, wdt_ref, *rest, first_win_tile, n_alias, cache_li, prompt):
    if prompt:
        lcw_ref, lcb_ref, scw_ref, scb_ref = rest[:4]
        rest = rest[4 + n_alias:]
        kt_ref, vt_ref, tail_lru_ref, tail_xbc_ref, xe_lru, xe_xbc = rest[9:]
    xlru_ref, glru_ref, q_ref, k_ref, v_ref, gattn_ref, z_ref, xbc_ref, dt_ref = rest[:9]
    n = x_ref.shape[0]
    x = x_ref[...]
    h = x * lax.rsqrt(jnp.mean(x * x, axis=-1, keepdims=True) + EPS) * g_ref[...]
    hb = h.astype(BF16)

    def mm(off, width):
        return jnp.dot(hb, w_ref[:, off:off + width], preferred_element_type=F32)

    def conv(raw, xe_ref, cw_ref, cb_ref, tail_ref):
        xe_ref[8:8 + n, :] = raw
        out = _conv_rows(xe_ref, n, cw_ref, cb_ref)
        last = xe_ref[n:n + 8, :]
        xe_ref[0:8, :] = last
        tail_ref[...] = last
        return out

    if prompt:
        @pl.when(pl.program_id(1) == 0)
        def _():
            xe_lru[0:8, :] = jnp.zeros((8, D_LRU), F32)
            xe_xbc[0:8, :] = jnp.zeros((8, D_XBC), F32)

    xbc = mm(OFF_XBC, D_XBC)
    xbc_ref[...] = _silu(conv(xbc, xe_xbc, scw_ref, scb_ref, tail_xbc_ref)) if prompt else xbc
    x_lru = mm(OFF_XLRU, D_LRU)
    xlru_ref[...] = conv(x_lru, xe_lru, lcw_ref, lcb_ref, tail_lru_ref) if prompt else x_lru
    z = mm(OFF_Z, D_SSD)
    z_ref[...] = _silu(z) if prompt else z
    g_lru = mm(OFF_GLRU, D_LRU)
    glru_ref[...] = _silu(g_lru) if prompt else g_lru
    q = mm(OFF_Q, D_ATTN) * (HEAD_DIM ** -0.5)
    k = mm(OFF_K, D_ATTN)
    v = mm(OFF_V, D_ATTN)
    for c in range(D_ATTN // LANES):
        q_ref[c] = q[:, c * LANES:(c + 1) * LANES]
        k_ref[c] = k[:, c * LANES:(c + 1) * LANES]
        v_ref[c] = v[:, c * LANES:(c + 1) * LANES]
    gattn_ref[...] = mm(OFF_GATTN, D_ATTN)
    dt_ref[...] = jnp.dot(hb, wdt_ref[...], preferred_element_type=F32)
    if prompt:
        @pl.when(pl.program_id(1) >= first_win_tile)
        def _():
            for dst, val in ((kt_ref, k.T), (vt_ref, v.T)):
                if dst.ndim == 2:
                    dst[...] = val
                else:
                    for l in range(dst.shape[0]):
                        dst[l] = val if l == cache_li else jnp.zeros_like(val)


def _proj(x3, g, w_bf16, wdt_bf16, tm, cache=None):
    B, S, _ = x3.shape
    row = lambda width: pl.BlockSpec((None, tm, width), lambda b, t: (b, t, 0))
    const = lambda r, c: pl.BlockSpec((r, c), lambda b, t: (0, 0))
    pair = pl.BlockSpec((D_ATTN // LANES, None, tm, LANES), lambda b, t: (0, b, t, 0))
    pair_shape = (4, B, S, LANES)
    sds = lambda *s: jax.ShapeDtypeStruct(s, F32)
    in_specs = [row(D_MODEL), const(1, D_MODEL), const(D_MODEL, D_IN), const(D_MODEL, LANES)]
    args = [x3, g, w_bf16, wdt_bf16]
    out_specs = [row(D_LRU), row(D_LRU), pair, pair, pair, row(D_ATTN), row(D_SSD), row(D_XBC), row(LANES)]
    out_shape = [sds(B, S, D_LRU), sds(B, S, D_LRU), sds(*pair_shape), sds(*pair_shape), sds(*pair_shape),
                 sds(B, S, D_ATTN), sds(B, S, D_SSD), sds(B, S, D_XBC), sds(B, S, LANES)]
    first_win_tile, n_alias, aliases, cache_li, scratch = 0, 0, {}, 0, []
    if cache is not None:
        li, depth, win, prev_kt, prev_vt, lcw, lcb, scw, scb = cache
        in_specs += [const(CONV_W, D_LRU), const(1, D_LRU), const(CONV_W, D_XBC), const(1, D_XBC)]
        args += [lcw, lcb, scw, scb]
        first_win_tile = (S - win) // tm
        cache_li = li
        win_tile = lambda t: jnp.maximum(t - first_win_tile, 0)
        if prev_kt is None:
            cspec = pl.BlockSpec((depth, None, D_ATTN, tm), lambda b, t: (0, b, 0, win_tile(t)))
        else:
            cspec = pl.BlockSpec((None, None, D_ATTN, tm), lambda b, t: (li, b, 0, win_tile(t)))
        tail = lambda width: pl.BlockSpec((None, 8, width), lambda b, t: (b, 0, 0))
        out_specs += [cspec, cspec, tail(D_LRU), tail(D_XBC)]
        out_shape += [sds(depth, B, D_ATTN, win), sds(depth, B, D_ATTN, win), sds(B, 8, D_LRU), sds(B, 8, D_XBC)]
        scratch = [pltpu.VMEM((tm + 8, D_LRU), F32), pltpu.VMEM((tm + 8, D_XBC), F32)]
        if prev_kt is not None:
            n_alias = 2
            in_specs += [pl.BlockSpec(memory_space=pl.ANY)] * 2
            args += [prev_kt, prev_vt]
            aliases = {len(args) - 2: 9, len(args) - 1: 10}
    return pl.pallas_call(
        functools.partial(_proj_kernel, first_win_tile=first_win_tile, n_alias=n_alias, cache_li=cache_li,
                          prompt=cache is not None),
        grid=(B, S // tm),
        in_specs=in_specs,
        out_specs=out_specs,
        out_shape=out_shape,
        scratch_shapes=scratch,
        input_output_aliases=aliases,
        compiler_params=_cparams(("arbitrary", "arbitrary")),
        name="in_proj",
    )(*args)


def _lru_gates(xc, wg_ref, bg_ref, lam_ref):
    pre = jnp.dot(xc.astype(BF16), wg_ref[...], preferred_element_type=F32) + bg_ref[...]
    r = jax.nn.sigmoid(pre[:, :D_LRU])
    i = jax.nn.sigmoid(pre[:, D_LRU:])
    log_a = (-LRU_C) * r * _softplus(-lam_ref[...])
    a = jnp.exp(log_a)
    t = jnp.tanh(log_a)
    return a, jnp.sqrt(-2.0 * t / (1.0 - t)) * (i * xc)


def _scan_rows(a, b, h0):
    n, width = a.shape
    a = a.reshape(n // 8, 8, width)
    b = b.reshape(n // 8, 8, width)
    row = lax.broadcasted_iota(jnp.int32, a.shape, 1)
    k = 1
    while k < 8:
        keep = row >= k
        a_sh = jnp.where(keep, pltpu.roll(a, k, axis=1), 1.0)
        b_sh = jnp.where(keep, pltpu.roll(b, k, axis=1), 0.0)
        b = a * b_sh + b
        a = a * a_sh
        k *= 2
    blocks = []
    h = h0
    for j in range(n // 8):
        blk = b[j] + a[j] * h
        blocks.append(blk)
        h = blk[7:8, :]
    return jnp.concatenate(blocks, axis=0), h


def _conv_rows(xe_ref, n, w_ref, b_ref):
    y = b_ref[...] + xe_ref[8:8 + n, :] * w_ref[3:4, :]
    for tap in range(CONV_W - 1):
        y = y + xe_ref[5 + tap:5 + tap + n, :] * w_ref[tap:tap + 1, :]
    return y


def _lru_p_kernel(xc_ref, gate_ref, wg_ref, bg_ref, lam_ref, y_ref, hl_ref, h_ref):
    @pl.when(pl.program_id(1) == 0)
    def _():
        h_ref[...] = jnp.zeros_like(h_ref)

    a, b = _lru_gates(xc_ref[...], wg_ref, bg_ref, lam_ref)
    hs, h_last = _scan_rows(a, b, h_ref[0:1, :])
    h_ref[...] = jnp.broadcast_to(h_last, h_ref.shape)
    hl_ref[...] = h_last
    y_ref[...] = (hs * gate_ref[...]).astype(y_ref.dtype)


def _lru_prompt(xc, gate, wg, bg, lam, tt):
    B, S, _ = xc.shape
    tile = pl.BlockSpec((None, tt, D_LRU), lambda b, t: (b, t, 0))
    const = lambda r, c: pl.BlockSpec((r, c), lambda b, t: (0, 0))
    return pl.pallas_call(
        _lru_p_kernel,
        grid=(B, S // tt),
        in_specs=[tile, tile, const(D_LRU, 2 * D_LRU), const(1, 2 * D_LRU), const(1, D_LRU)],
        out_specs=[tile, pl.BlockSpec((None, 1, D_LRU), lambda b, t: (b, 0, 0))],
        out_shape=[jax.ShapeDtypeStruct((B, S, D_LRU), BF16), jax.ShapeDtypeStruct((B, 1, D_LRU), F32)],
        scratch_shapes=[pltpu.VMEM((8, D_LRU), F32)],
        compiler_params=_cparams(("arbitrary", "arbitrary")),
        name="lru_prompt",
    )(xc, gate, wg, bg, lam)


def _lru_s_kernel(x_ref, g_ref, cs_ref, h0_ref, cw_ref, cb_ref, wg_ref, bg_ref, lam_ref, y_ref, hl_ref):
    L = x_ref.shape[0]
    db = x_ref.shape[1]
    xs = [cs_ref[i] for i in range(CONV_W - 1)] + [x_ref[i] for i in range(L)]
    xc = []
    for t in range(L):
        y = cb_ref[...] + xs[t] * cw_ref[0:1, :]
        for tap in range(1, CONV_W):
            y = y + xs[t + tap] * cw_ref[tap:tap + 1, :]
        xc.append(y)
    a, b = _lru_gates(jnp.concatenate(xc, axis=0), wg_ref, bg_ref, lam_ref)
    h = h0_ref[...]
    for t in range(L):
        h = a[t * db:(t + 1) * db] * h + b[t * db:(t + 1) * db]
        y_ref[t] = h * _silu(g_ref[t])
    hl_ref[...] = h


def _lru_sample(x_tm, g_tm, cs_tm, h0, cw, cb, wg, bg, lam):
    L, DB, _ = x_tm.shape
    return pl.pallas_call(
        _lru_s_kernel,
        out_shape=[jax.ShapeDtypeStruct((L, DB, D_LRU), F32), jax.ShapeDtypeStruct((DB, D_LRU), F32)],
        compiler_params=pltpu.CompilerParams(vmem_limit_bytes=VMEM_LIMIT),
        name="lru_sample",
    )(x_tm, g_tm, cs_tm, h0, cw, cb, wg, bg, lam)


def _block_order(rho, dil):
    if dil == 16:
        return rho
    if dil == 4:
        return 4 * (rho & 31) + (rho >> 5)
    return 16 * (rho & 7) + (rho >> 3)


def _chunks(dil):
    return {16: (1, 128), 4: (4, 32), 1: (16, 8)}[dil]


def _attn_p_kernel(q_ref, k_ref, v_ref, o_ref, qs, ks, vs, r_st, m_st, bias_ref, *, unroll):
    c = pl.program_id(1)
    s = pl.program_id(2)
    nres = SPAN // ATTN_BLOCK
    lane_q = lax.broadcasted_iota(jnp.int32, (ATTN_BLOCK, LANES), 1)
    first_half = lane_q < HEAD_DIM

    cur = s & 1
    prev = 1 - cur

    @pl.when(s == 0)
    def _():
        ks[prev] = jnp.zeros((SPAN, LANES), F32)
        vs[0, prev] = jnp.zeros((SPAN, LANES), F32)
        vs[1, prev] = jnp.zeros((SPAN, LANES), F32)

    pr = lax.broadcasted_iota(jnp.int32, (ATTN_BLOCK, ATTN_BLOCK), 0)
    pc = lax.broadcasted_iota(jnp.int32, (ATTN_BLOCK, ATTN_BLOCK), 1)
    perm = jnp.where(pc == nres * (pr & 7) + (pr >> 3), 1.0, 0.0).astype(BF16)
    for g in range(SPAN // ATTN_BLOCK):
        src = slice(g * ATTN_BLOCK, (g + 1) * ATTN_BLOCK)
        move = lambda ref: jnp.dot(perm, ref[src, :].astype(BF16), preferred_element_type=F32)
        q, k, v = move(q_ref), move(k_ref), move(v_ref)
        pieces = ((qs, (0,), jnp.where(first_half, q, 0.0)), (qs, (1,), jnp.where(first_half, 0.0, q)),
                  (ks, (cur,), k),
                  (vs, (0, cur), jnp.where(first_half, v, 1.0)), (vs, (1, cur), jnp.where(first_half, 1.0, v)))
        for r in range(nres):
            for ref, lead, val in pieces:
                ref[lead + (slice(r * ATTN_BLOCK + 8 * g, r * ATTN_BLOCK + 8 * g + 8), slice(None))] = (
                    val[8 * r:8 * r + 8, :])

    span0 = s == 0

    @pl.when(span0)
    def _():
        iq = lax.broadcasted_iota(jnp.int32, (ATTN_BLOCK, 2 * ATTN_BLOCK), 0)
        ik = lax.broadcasted_iota(jnp.int32, (ATTN_BLOCK, 2 * ATTN_BLOCK), 1)
        in_cur = ik >= ATTN_BLOCK
        for di, (window, dil) in enumerate(DILATIONS):
            j = ATTN_BLOCK + _block_order(iq, dil) - (
                jnp.where(in_cur, ATTN_BLOCK, 0) + _block_order(ik & (ATTN_BLOCK - 1), dil))
            valid = (j >= 0) & (j <= window // dil)
            dist = (j * dil).astype(F32)
            dist_all = jnp.where(valid, dist, MASKED)
            dist_cur = jnp.where(valid & in_cur, dist, MASKED)
            for hh in range(2):
                slope = jnp.exp2(-(jnp.zeros((1, 1), F32) + (2 * c + hh + 1).astype(F32)))
                bias_ref[di, hh, 0] = -slope * dist_all
                bias_ref[di, hh, 1] = -slope * dist_cur

    def block(di, dil, it):
        nchunk, crow = _chunks(dil)

        def gather(ref, lead, starts):
            parts = [ref[lead + (pl.ds(st, crow), slice(None))] for st in starts]
            return parts[0] if len(parts) == 1 else jnp.concatenate(parts, axis=0)

        if dil == 16:
            base, cb, last_off = it * ATTN_BLOCK, 0, 0
        elif dil == 4:
            base, cb, last_off = (it >> 2) * ATTN_BLOCK, it & 3, 3 * crow
        else:
            base, cb, last_off = 0, it, 15 * crow
        stride = {16: 0, 4: 4 * ATTN_BLOCK, 1: ATTN_BLOCK}[dil]
        cur_starts = [pl.multiple_of(base + a * stride + crow * cb, 8) for a in range(nchunk)]
        if dil == 16:
            prev_slot = prev
            prev_starts = cur_starts
            first_in_span = True
        else:
            has_prev = cb > 0
            prev_slot = jnp.where(has_prev, cur, prev)
            prev_off = jnp.where(has_prev, crow * (cb - 1), last_off)
            prev_starts = [pl.multiple_of(base + a * stride + prev_off, 8) for a in range(nchunk)]
            first_in_span = jnp.logical_not(has_prev)
        kb = jnp.concatenate([gather(ks, (prev_slot,), prev_starts), gather(ks, (cur,), cur_starts)],
                             axis=0).astype(BF16)
        variant = jnp.where(jnp.logical_and(span0, first_in_span), 1, 0)
        for hh in range(2):
            qh = gather(qs, (hh,), cur_starts).astype(BF16)
            vh = jnp.concatenate([gather(vs, (hh, prev_slot), prev_starts), gather(vs, (hh, cur), cur_starts)],
                                 axis=0).astype(BF16)
            sc = _dot_nt(qh, kb) + bias_ref[di, hh, variant]
            m = jnp.max(sc, axis=1, keepdims=True)
            p = jnp.exp(sc - m)
            r_new = jnp.dot(p.astype(BF16), vh, preferred_element_type=F32)
            m_new = jnp.broadcast_to(m, (ATTN_BLOCK, LANES))
            for a, st in enumerate(cur_starts):
                r_st[di, hh, pl.ds(st, crow), :] = r_new[a * crow:(a + 1) * crow, :]
                m_st[di, hh, pl.ds(st, crow), :] = m_new[a * crow:(a + 1) * crow, :]

    def body(it, carry):
        for di, (_, dil) in enumerate(DILATIONS):
            block(di, dil, it)
        return carry

    lax.fori_loop(0, nres, body, 0, unroll=unroll)

    ndil = len(DILATIONS)
    for r in range(nres):
        src = slice(r * ATTN_BLOCK, (r + 1) * ATTN_BLOCK)
        outs = []
        for hh in range(2):
            ms = [m_st[di, hh, src, :] for di in range(ndil)]
            m_tot = functools.reduce(jnp.maximum, ms)
            tot = sum(r_st[di, hh, src, :] * jnp.exp(ms[di] - m_tot) for di in range(ndil))
            outs.append(tot / pltpu.roll(tot, HEAD_DIM, axis=1))
        o_ref[pl.ds(r, ATTN_BLOCK, stride=nres), :] = jnp.where(first_half, outs[0], outs[1])


def _attn_prompt(q4, k4, v4, unroll=8):
    npair, B, S, _ = q4.shape
    ndil = len(DILATIONS)
    spec = pl.BlockSpec((None, None, SPAN, LANES), lambda b, c, s: (c, b, s, 0))
    return pl.pallas_call(
        functools.partial(_attn_p_kernel, unroll=unroll),
        grid=(B, npair, S // SPAN),
        in_specs=[spec, spec, spec],
        out_specs=spec,
        out_shape=jax.ShapeDtypeStruct(q4.shape, F32),
        scratch_shapes=[pltpu.VMEM((2, SPAN, LANES), F32), pltpu.VMEM((2, SPAN, LANES), F32),
                        pltpu.VMEM((2, 2, SPAN, LANES), F32),
                        pltpu.VMEM((ndil, 2, SPAN, LANES), F32), pltpu.VMEM((ndil, 2, SPAN, LANES), F32),
                        pltpu.VMEM((ndil, 2, 2, ATTN_BLOCK, 2 * ATTN_BLOCK), F32)],
        compiler_params=_cparams(("arbitrary", "arbitrary", "arbitrary")),
        name="attn_prompt",
    )(q4, k4, v4)


def _attn_s_kernel(*refs, n_new):
    _attn_s_body(*refs, n_new=n_new, first=pl.program_id(0) == 0)


def _attn_s_body(q_ref, kn_ref, vn_ref, kc_ref, vc_ref, o_ref, dist_ref, mult_ref, *, n_new, first):
    nseq, nheads, hd, W = kc_ref.shape
    rows = q_ref.shape[1]
    G = 4
    gr, gl = G * rows, G * hd
    shift = n_new.bit_length() - 1

    @pl.when(first)
    def _():
        i = lax.broadcasted_iota(jnp.int32, (gr, W), 0) & (n_new - 1)
        pos = lax.broadcasted_iota(jnp.int32, (gr, W), 1)
        d = W + i - pos
        cnt = jnp.zeros((gr, W), F32)
        for window, dil in DILATIONS:
            cnt = cnt + jnp.where(((d & (dil - 1)) == 0) & (d <= window), 1.0, 0.0)
        dist_ref[...] = d.astype(F32)
        mult_ref[...] = cnt

    r = lax.broadcasted_iota(jnp.int32, (gr, rows), 0) & (rows - 1)
    r2 = lax.broadcasted_iota(jnp.int32, (gr, rows), 1)
    dn = (r & (n_new - 1)) - (r2 & (n_new - 1))
    same_seq = (r >> shift) == (r2 >> shift)
    mult_n = jnp.zeros((gr, rows), F32)
    for window, dil in DILATIONS:
        ok = ((dn & (dil - 1)) == 0) & (dn >= 0) & (dn <= window) & same_seq
        mult_n = mult_n + jnp.where(ok, 1.0, 0.0)
    dist_n = dn.astype(F32)
    dist_c = dist_ref[...]
    mult_c = mult_ref[...]
    row_head = lax.broadcasted_iota(jnp.int32, (gr, gl), 0) >> 3
    hd_shift = hd.bit_length() - 1
    lane_head = lax.broadcasted_iota(jnp.int32, (gr, gl), 1) >> hd_shift
    diag = row_head == lane_head
    out_lane_head = lax.broadcasted_iota(jnp.int32, (rows, gl), 1) >> hd_shift
    out_row_seq = lax.broadcasted_iota(jnp.int32, (rows, gl), 0) >> shift
    head_col = lax.broadcasted_iota(jnp.int32, (gr, 1), 0) >> 3
    npair_g = gl // LANES
    for g in range(nheads // G):
        pairs = range(g * npair_g, (g + 1) * npair_g)
        cat = lambda ref: jnp.concatenate([ref[c] for c in pairs], axis=1)
        q_bd = jnp.where(diag, jnp.concatenate([cat(q_ref)] * G, axis=0), 0.0).astype(BF16)
        slope = jnp.exp2(-(head_col + (g * G + 1)).astype(F32))
        sn = _dot_nt(q_bd, cat(kn_ref).astype(BF16))
        sn = jnp.where(mult_n > 0, sn - slope * dist_n, NEG)
        mn = jnp.max(sn, axis=1, keepdims=True)
        vn = cat(vn_ref).astype(BF16)
        o_g = jnp.zeros((rows, gl), F32)
        for b in range(nseq):
            kst = kc_ref[b, g * G:(g + 1) * G].reshape(gl, W).astype(BF16)
            vst = vc_ref[b, g * G:(g + 1) * G].reshape(gl, W).astype(BF16)
            sc = jnp.dot(q_bd, kst, preferred_element_type=F32)
            sc = jnp.where(mult_c > 0, sc - slope * dist_c, NEG)
            m = jnp.maximum(jnp.max(sc, axis=1, keepdims=True), mn)
            pc = mult_c * jnp.exp(sc - m)
            pn = mult_n * jnp.exp(sn - m)
            l = jnp.sum(pc, axis=1, keepdims=True) + jnp.sum(pn, axis=1, keepdims=True)
            o = (_dot_nt(pc.astype(BF16), vst) + jnp.dot(pn.astype(BF16), vn, preferred_element_type=F32)) / l
            sel = sum(jnp.where(out_lane_head == h, o[h * rows:(h + 1) * rows, :], 0.0) for h in range(G))
            o_g = jnp.where(out_row_seq == b, sel, o_g)
        for j, c in enumerate(pairs):
            o_ref[c] = o_g[:, j * LANES:(j + 1) * LANES]


def _attn_sample(q4, k4, v4, kc_t, vc_t, li, n_new):
    small, cache, tables = _attn_s_specs(q4, kc_t, li, n_new, lambda b: b)
    return pl.pallas_call(
        functools.partial(_attn_s_kernel, n_new=n_new),
        grid=(q4.shape[1] // 8,),
        in_specs=[small, small, small, cache, cache],
        out_specs=small,
        out_shape=jax.ShapeDtypeStruct(q4.shape, F32),
        scratch_shapes=tables,
        compiler_params=_cparams(("arbitrary",)),
        name="attn_sample",
    )(q4, k4, v4, kc_t, vc_t)


def _attn_s_specs(q4, kc_t, li, n_new, group_of):
    npair, T, _ = q4.shape
    W = kc_t.shape[-1]
    rows = 8
    assert n_new & (n_new - 1) == 0 and rows % n_new == 0 and T % rows == 0
    small = pl.BlockSpec((npair, rows, LANES), lambda *g: (0, group_of(*g), 0))
    cache = pl.BlockSpec((None, rows // n_new, N_HEADS, HEAD_DIM, W), lambda *g: (li, group_of(*g), 0, 0, 0))
    return small, cache, [pltpu.VMEM((4 * rows, W), F32), pltpu.VMEM((4 * rows, W), F32)]


def _group_norm_out(y, z, ng_ref, gate_ready=False):
    yg = y * (z if gate_ready else _silu(z))
    outs = []
    for g in range(SSD_GROUPS):
        part = yg[:, g * GROUP_W:(g + 1) * GROUP_W]
        outs.append(part * lax.rsqrt(jnp.mean(part * part, axis=-1, keepdims=True) + EPS))
    return jnp.concatenate(outs, axis=1) * ng_ref[...]


def _ssd_p_kernel(*refs, n_new):
    fused = n_new is not None
    n_in = 13 if fused else 8
    y_ref, hl_ref = refs[n_in:n_in + 2]
    scratch = refs[n_in + (3 if fused else 2):]
    _ssd_p_tile(*refs[:8], y_ref, hl_ref, scratch[0])
    if fused:
        first = jnp.logical_and(pl.program_id(0) == 0, pl.program_id(1) == 0)
        _attn_s_body(*refs[8:13], refs[n_in + 2], scratch[1], scratch[2], n_new=n_new, first=first)


def _ssd_p_tile(xc_ref, dt_ref, gate_ref, dtb_ref, alog_ref, dx_ref, ng_ref, e_ref, y_ref, hl_ref, ht_ref):
    t = pl.program_id(1)
    Q = SSD_CHUNK
    n = xc_ref.shape[0]

    @pl.when(t == 0)
    def _():
        ht_ref[...] = jnp.zeros_like(ht_ref)

    xc_all = xc_ref[...]
    dt_all = _softplus(dt_ref[...] + dtb_ref[...])
    a_neg = -jnp.exp(alog_ref[...])
    ri = lax.broadcasted_iota(jnp.int32, (Q, Q), 0)
    ci = lax.broadcasted_iota(jnp.int32, (Q, Q), 1)
    causal = ri >= ci
    tri = jnp.where(causal, 1.0, 0.0).astype(BF16)
    lane = lax.broadcasted_iota(jnp.int32, (Q, LANES), 1)
    for ck in range(n // Q):
        rows = slice(ck * Q, (ck + 1) * Q)
        y = _ssd_chunk(xc_all[rows, :], dt_all[rows, :], a_neg, causal, tri, lane, dx_ref, e_ref, ht_ref)
        y_ref[rows, :] = _group_norm_out(y, gate_ref[rows, :], ng_ref, gate_ready=True).astype(y_ref.dtype)

    @pl.when(t == pl.num_programs(1) - 1)
    def _():
        for g in range(SSD_GROUPS):
            hl_ref[g * (SSD_HEADS // SSD_GROUPS):(g + 1) * (SSD_HEADS // SSD_GROUPS)] = (
                ht_ref[g].T.reshape(SSD_HEADS // SSD_GROUPS, SSD_P, SSD_N))


def _ssd_chunk(xc, dt, a_neg, causal, tri, lane, dx_ref, e_ref, ht_ref):
    Q = SSD_CHUNK
    xs = xc[:, :D_SSD]
    bm = xc[:, D_SSD:D_SSD + SSD_GROUPS * SSD_N]
    cm = xc[:, D_SSD + SSD_GROUPS * SSD_N:]
    cum = _dot3_left(tri, dt * a_neg)
    cum_t = cum.T
    both_x = _dot3_right(jnp.concatenate([cum, dt], axis=0), e_ref[...])
    cum_x = both_x[:Q]
    dt_x = both_x[Q:]
    xdt = xs * dt_x
    ecum = jnp.exp(cum_x)
    cl = cum_x[Q - 1:Q, :]
    w_end = xdt * jnp.exp(cl - cum_x)
    dec = jnp.exp(cl)

    y_parts = []
    for g in range(SSD_GROUPS):
        cg = cm[:, g * SSD_N:(g + 1) * SSD_N].astype(BF16)
        bg = bm[:, g * SSD_N:(g + 1) * SSD_N]
        cb = _dot_nt(cg, bg.astype(BF16))
        gl = slice(g * GROUP_W, (g + 1) * GROUP_W)
        h_old = ht_ref[g]
        y_off = jnp.dot(cg, h_old.astype(BF16), preferred_element_type=F32) * ecum[:, gl]
        ht_ref[g] = h_old * dec[:, gl] + jnp.dot(bg.T.astype(BF16), w_end[:, gl].astype(BF16),
                                                  preferred_element_type=F32)
        for pp in range(GROUP_W // LANES):
            pl_ = slice(g * GROUP_W + pp * LANES, g * GROUP_W + (pp + 1) * LANES)
            xdt_pair = xdt[:, pl_].astype(BF16)
            halves = []
            for hh in range(2):
                h = (g * GROUP_W + pp * LANES) // SSD_P + hh
                seg = jnp.broadcast_to(cum[:, h:h + 1], (Q, Q)) - jnp.broadcast_to(cum_t[h:h + 1, :], (Q, Q))
                scores = cb * jnp.exp(jnp.where(causal, seg, NEG))
                halves.append(jnp.dot(scores.astype(BF16), xdt_pair, preferred_element_type=F32))
            y_diag = jnp.where(lane < SSD_P, halves[0], halves[1])
            y_parts.append(y_diag + y_off[:, pp * LANES:(pp + 1) * LANES])
    return jnp.concatenate(y_parts, axis=1) + dx_ref[...] * xs


def _ssd_prompt(xc, dt, gate, dtb, alog, dx, ng, e, sample_attn=None):
    B, S, _ = xc.shape
    Q = SSD_PROMPT_CHUNKS * SSD_CHUNK
    nt = S // Q
    tile = lambda w: pl.BlockSpec((None, Q, w), lambda b, t: (b, t, 0))
    const = lambda r, c: pl.BlockSpec((r, c), lambda b, t: (0, 0))
    in_specs = [tile(D_XBC), tile(LANES), tile(D_SSD), const(1, LANES), const(1, LANES), const(1, D_SSD),
                const(1, D_SSD), const(3 * LANES, D_SSD)]
    args = [xc, dt, gate, dtb, alog, dx, ng, e]
    out_specs = [tile(D_SSD), pl.BlockSpec((None, SSD_HEADS, SSD_P, SSD_N), lambda b, t: (b, 0, 0, 0))]
    out_shape = [jax.ShapeDtypeStruct((B, S, D_SSD), BF16), jax.ShapeDtypeStruct((B, SSD_HEADS, SSD_P, SSD_N), F32)]
    scratch = [pltpu.VMEM((SSD_GROUPS, SSD_N, GROUP_W), F32)]
    n_new = None
    if sample_attn is not None:
        q4, k4, v4, kc_t, vc_t, li, n_new = sample_attn
        small, cache, tables = _attn_s_specs(q4, kc_t, li, n_new, lambda b, t: b * nt + t)
        assert q4.shape[1] // 8 == B * nt
        in_specs += [small, small, small, cache, cache]
        args += [q4, k4, v4, kc_t, vc_t]
        out_specs.append(small)
        out_shape.append(jax.ShapeDtypeStruct(q4.shape, F32))
        scratch += tables
    return pl.pallas_call(
        functools.partial(_ssd_p_kernel, n_new=n_new),
        grid=(B, nt),
        in_specs=in_specs,
        out_specs=out_specs,
        out_shape=out_shape,
        scratch_shapes=scratch,
        compiler_params=_cparams(("arbitrary", "arbitrary")),
        name="ssd_prompt" if sample_attn is None else "ssd_prompt_attn_sample",
    )(*args)


def _ssd_s_kernel(xbc_ref, cs_ref, dt_ref, z_ref, h_ref, cw_ref, cb_ref, dtb_ref, alog_ref, dx_ref, ng_ref, e_ref,
                  *rest, n_new, li):
    y_ref, hn_ref = rest[-2:]
    if hn_ref.ndim == 5:
        for l in range(hn_ref.shape[0]):
            if l != li:
                hn_ref[l] = jnp.zeros(hn_ref.shape[1:], F32)
        hn_ref = hn_ref.at[li]
    for i in range(xbc_ref.shape[0]):
        _ssd_s_one(xbc_ref.at[i], cs_ref.at[i], dt_ref.at[i], z_ref.at[i], h_ref.at[i], cw_ref, cb_ref, dtb_ref,
                   alog_ref, dx_ref, ng_ref, e_ref, y_ref.at[i], hn_ref.at[i], n_new)


def _ssd_s_one(xbc_ref, cs_ref, dt_ref, z_ref, h_ref, cw_ref, cb_ref, dtb_ref, alog_ref, dx_ref, ng_ref, e_ref,
               y_ref, hn_ref, n_new):
    R = 8
    row = lax.broadcasted_iota(jnp.int32, (R, LANES), 0)
    real = row < n_new
    xe = jnp.concatenate([cs_ref[...], xbc_ref[...]], axis=0)
    y = cb_ref[...] + xe[8:16, :] * cw_ref[3:4, :]
    for tap in range(CONV_W - 1):
        y = y + xe[5 + tap:13 + tap, :] * cw_ref[tap:tap + 1, :]
    xc = _silu(y)
    xs = xc[:, :D_SSD]
    bm = xc[:, D_SSD:D_SSD + SSD_GROUPS * SSD_N]
    cm = xc[:, D_SSD + SSD_GROUPS * SSD_N:]

    dt = jnp.where(real, _softplus(dt_ref[...] + dtb_ref[...]), 0.0)
    da = dt * (-jnp.exp(alog_ref[...]))
    cum = da
    k = 1
    while k < R:
        cum = cum + jnp.where(row >= k, pltpu.roll(cum, k, axis=0), 0.0)
        k *= 2
    e = e_ref[...]
    cum_x = _dot3_right(cum, e)
    dt_x = _dot3_right(dt, e)
    xdt = xs * dt_x
    ecum = jnp.exp(cum_x)
    cl = cum_x[R - 1:R, :]
    w_end = xdt * jnp.exp(cl - cum_x)
    dec3 = _split3(jnp.exp(cl))
    ones = jnp.ones((R, SSD_N), BF16)
    zrow = jnp.zeros((R - 3, GROUP_W), BF16)

    y_parts = []
    for g in range(SSD_GROUPS):
        gl = slice(g * GROUP_W, (g + 1) * GROUP_W)
        cg = cm[:, g * SSD_N:(g + 1) * SSD_N]
        bg = bm[:, g * SSD_N:(g + 1) * SSD_N]
        hs = slice(g * (SSD_HEADS // SSD_GROUPS), (g + 1) * (SSD_HEADS // SSD_GROUPS))
        h_old = h_ref[hs].reshape(GROUP_W, SSD_N)
        y_g = _dot_nt(cg.astype(BF16), h_old.astype(BF16)) * ecum[:, gl]
        for j in range(n_new):
            cbj = jnp.sum(cg * bg[j:j + 1, :], axis=1, keepdims=True)
            term = cbj * jnp.exp(cum_x[:, gl] - cum_x[j:j + 1, gl]) * xdt[j:j + 1, gl]
            y_g = y_g + jnp.where(row[:, 0:1] >= j, term, 0.0)
        y_parts.append(y_g)
        dec_col = _dot_tn(jnp.concatenate([p[:, gl] for p in dec3] + [zrow], axis=0), ones)
        upd = _dot_tn(w_end[:, gl].astype(BF16), bg.astype(BF16))
        hn_ref[hs] = (h_old * dec_col + upd).reshape(SSD_HEADS // SSD_GROUPS, SSD_P, SSD_N)
    y = jnp.concatenate(y_parts, axis=1) + dx_ref[...] * xs
    y_ref[...] = _group_norm_out(y, z_ref[...], ng_ref)


def _ssd_sample(xbc, cs, dt, z, h_all, li, h_new_prev, cw, cb, dtb, alog, dx, ng, e, n_new):
    DB = xbc.shape[0]
    bb = SSD_SAMPLE_SEQS if DB % SSD_SAMPLE_SEQS == 0 else 1
    tile = lambda w: pl.BlockSpec((bb, 8, w), lambda b: (b, 0, 0))
    const = lambda r, c: pl.BlockSpec((r, c), lambda b: (0, 0))
    state = pl.BlockSpec((None, bb, SSD_HEADS, SSD_P, SSD_N), lambda b: (li, b, 0, 0, 0))
    in_specs = [tile(D_XBC), tile(D_XBC), tile(LANES), tile(D_SSD), state, const(CONV_W, D_XBC), const(1, D_XBC),
                const(1, LANES), const(1, LANES), const(1, D_SSD), const(1, D_SSD), const(3 * LANES, D_SSD)]
    args = [xbc, cs, dt, z, h_all, cw, cb, dtb, alog, dx, ng, e]
    aliases = {}
    if h_new_prev is not None:
        in_specs.append(pl.BlockSpec(memory_space=pl.ANY))
        args.append(h_new_prev)
        aliases = {len(args) - 1: 1}
        state_out = state
    else:
        state_out = pl.BlockSpec((h_all.shape[0], bb, SSD_HEADS, SSD_P, SSD_N), lambda b: (0, b, 0, 0, 0))
    return pl.pallas_call(
        functools.partial(_ssd_s_kernel, n_new=n_new, li=li),
        grid=(DB // bb,),
        in_specs=in_specs,
        out_specs=[tile(D_SSD), state_out],
        out_shape=[jax.ShapeDtypeStruct((DB, 8, D_SSD), F32), jax.ShapeDtypeStruct(h_all.shape, F32)],
        input_output_aliases=aliases,
        compiler_params=_cparams(("arbitrary",)),
        name="ssd_sample",
    )(*args)


def _out_kernel(x_ref, ylru_ref, o_ref, gattn_ref, yssd_ref, w_ref, g_ref, out_ref):
    o = jnp.concatenate([o_ref[c] for c in range(D_ATTN // LANES)], axis=1)
    y_attn = o * _silu(gattn_ref[...])
    mix_in = jnp.concatenate([ylru_ref[...].astype(BF16), y_attn.astype(BF16), yssd_ref[...].astype(BF16)], axis=1)
    mix = jnp.dot(mix_in, w_ref[...], preferred_element_type=F32)
    out_ref[...] = x_ref[...] + mix * lax.rsqrt(jnp.mean(mix * mix, axis=-1, keepdims=True) + EPS) * g_ref[...]


def _out_proj(x2d, y_lru, o4, g_attn, y_ssd, w_bf16, g, tm):
    T = x2d.shape[0]
    row = lambda width: pl.BlockSpec((tm, width), lambda i: (i, 0))
    o_spec = pl.BlockSpec((D_ATTN // LANES, tm, LANES), lambda i: (0, i, 0))
    return pl.pallas_call(
        _out_kernel,
        grid=(T // tm,),
        in_specs=[row(D_MODEL), row(D_LRU), o_spec,
                  row(D_ATTN), row(D_SSD), pl.BlockSpec((D_LRU + D_ATTN + D_SSD, D_MODEL), lambda i: (0, 0)),
                  pl.BlockSpec((1, D_MODEL), lambda i: (0, 0))],
        out_specs=row(D_MODEL),
        out_shape=jax.ShapeDtypeStruct((T, D_MODEL), F32),
        compiler_params=_cparams(("arbitrary",)),
        name="out_proj",
    )(x2d, y_lru, o4, g_attn, y_ssd, w_bf16, g)


def _block_diag(w):
    k, n, _ = w.shape
    return jnp.einsum("kij,kl->kilj", w, jnp.eye(k, dtype=w.dtype)).reshape(k * n, k * n)


def _pad_rows(a, rows):
    return jnp.pad(a, ((0, 0), (0, rows - a.shape[1]), (0, 0)))


def _pairs_to_heads(a4, nb, L):
    return a4.reshape(4, nb, L, 2, HEAD_DIM).transpose(1, 2, 0, 3, 4).reshape(nb, L, N_HEADS, HEAD_DIM)


def kernel(x_prompt, x_sample, state_lru_conv, state_lru_h, cache_attn_k, cache_attn_v, state_ssd_conv, state_ssd_h, pre_norm_g, post_norm_g, w_in, lru_conv_w, lru_conv_b, lru_w_a, lru_b_a, lru_w_x, lru_b_x, lru_lambda, ssd_conv_w, ssd_conv_b, ssd_dt_bias, ssd_a_log, ssd_d, ssd_norm_g, w_out):
    depth = w_in.shape[0]
    B, S, _ = x_prompt.shape
    DB, L, _ = x_sample.shape
    assert S % SPAN == 0 and L <= 8 and L >= CONV_W - 1
    win = min(MAX_WINDOW, S)

    lane_pad = lambda a: jnp.pad(a.reshape(1, -1), ((0, 0), (0, LANES - a.shape[-1])))
    expand = jnp.asarray(np.tile(np.repeat(np.eye(LANES, SSD_HEADS, dtype=np.float32), SSD_P, axis=1), (3, 1)), BF16)
    kc_t = jnp.transpose(cache_attn_k, (0, 1, 3, 4, 2))
    vc_t = jnp.transpose(cache_attn_v, (0, 1, 3, 4, 2))

    yp = x_prompt.reshape(B * S, D_MODEL)
    ys = x_sample.reshape(DB * L, D_MODEL)
    outs_p = []
    outs_s = []
    kt_p = vt_p = ssd_h_s = None
    for li in range(depth):
        w_in_b = w_in[li].astype(BF16)
        w_dt_b = jnp.pad(w_in[li][:, OFF_DT:], ((0, 0), (0, D_IN_PAD - D_IN))).astype(BF16)
        w_out_b = w_out[li].astype(BF16)
        pre_g = pre_norm_g[li].reshape(1, -1)
        post_g = post_norm_g[li].reshape(1, -1)
        wg = jnp.concatenate([_block_diag(lru_w_a[li]), _block_diag(lru_w_x[li])], axis=1).astype(BF16)
        bg = jnp.concatenate([lru_b_a[li], lru_b_x[li]]).reshape(1, -1)
        lcw, lcb, lam = lru_conv_w[li], lru_conv_b[li].reshape(1, -1), lru_lambda[li].reshape(1, -1)
        scw, scb = ssd_conv_w[li], ssd_conv_b[li].reshape(1, -1)
        dtb, alog = lane_pad(ssd_dt_bias[li]), lane_pad(ssd_a_log[li])
        dx = jnp.repeat(ssd_d[li], SSD_P).reshape(1, -1)
        ng = ssd_norm_g[li].reshape(1, -1)

        xc_lru, gate_lru, q4, k4, v4, gattn, gate_z, xc_ssd, dtr, kt_p, vt_p, tail_lru, tail_xbc = _proj(
            yp.reshape(B, S, D_MODEL), pre_g, w_in_b, w_dt_b, 256,
            cache=(li, depth, win, kt_p, vt_p, lcw, lcb, scw, scb))
        xlru_s, glru_s, q4s, k4s, v4s, gattn_s, z_s, xbc_s, dtr_s = [
            a.reshape(a.shape[:-3] + (DB * L, a.shape[-1]))
            for a in _proj(ys.reshape(1, DB * L, D_MODEL), pre_g, w_in_b, w_dt_b, min(256, DB * L))]

        y_lru, lru_h_p = _lru_prompt(xc_lru, gate_lru, wg, bg, lam, 512)
        o4 = _attn_prompt(q4, k4, v4)
        ssd_steps = B * (S // (SSD_PROMPT_CHUNKS * SSD_CHUNK))
        ride = L & (L - 1) == 0 and 8 % L == 0 and (DB * L) % 8 == 0 and (DB * L) // 8 == ssd_steps
        if ride:
            y_ssd, ssd_h_p, o_s = _ssd_prompt(xc_ssd, dtr, gate_z, dtb, alog, dx, ng, expand,
                                              sample_attn=(q4s, k4s, v4s, kc_t, vc_t, li, L))
        else:
            y_ssd, ssd_h_p = _ssd_prompt(xc_ssd, dtr, gate_z, dtb, alog, dx, ng, expand)
            o_s = _attn_sample(q4s, k4s, v4s, kc_t, vc_t, li, L)
        yp = _out_proj(yp, y_lru.reshape(B * S, D_LRU), o4.reshape(4, B * S, LANES), gattn.reshape(B * S, D_ATTN),
                       y_ssd.reshape(B * S, D_SSD), w_out_b, post_g, 512)
        outs_p.append((tail_lru[:, 8 - (CONV_W - 1):], lru_h_p.reshape(B, D_LRU), tail_xbc[:, 8 - (CONV_W - 1):],
                       ssd_h_p))

        xlru3 = xlru_s.reshape(DB, L, D_LRU)
        y_lru_tm, lru_h_s = _lru_sample(
            xlru3.transpose(1, 0, 2), glru_s.reshape(DB, L, D_LRU).transpose(1, 0, 2),
            state_lru_conv[li].transpose(1, 0, 2), state_lru_h[li], lcw, lcb, wg, bg, lam)
        y_lru = y_lru_tm.transpose(1, 0, 2).reshape(DB * L, D_LRU)

        xbc3 = xbc_s.reshape(DB, L, D_XBC)
        cs8 = jnp.pad(state_ssd_conv[li], ((0, 0), (8 - (CONV_W - 1), 0), (0, 0)))
        y_ssd8, ssd_h_s = _ssd_sample(_pad_rows(xbc3, 8), cs8, _pad_rows(dtr_s.reshape(DB, L, LANES), 8),
                                      _pad_rows(z_s.reshape(DB, L, D_SSD), 8), state_ssd_h, li, ssd_h_s, scw, scb,
                                      dtb, alog, dx, ng, expand, L)
        y_ssd = y_ssd8[:, :L].reshape(DB * L, D_SSD)
        ys = _out_proj(ys, y_lru, o_s, gattn_s, y_ssd, w_out_b, post_g, DB * L)
        outs_s.append((xlru3[:, L - (CONV_W - 1):], lru_h_s, _pairs_to_heads(k4s, DB, L),
                       _pairs_to_heads(v4s, DB, L), xbc3[:, L - (CONV_W - 1):]))

    lru_conv_p, lru_h_p, ssd_conv_p, ssd_h_p = [jnp.stack(c) for c in zip(*outs_p)]
    lru_conv_s, lru_h_s, k_s, v_s, ssd_conv_s = [jnp.stack(c) for c in zip(*outs_s)]
    k_p = kt_p.reshape(depth, B, N_HEADS, HEAD_DIM, win).transpose(0, 1, 4, 2, 3)
    v_p = vt_p.reshape(depth, B, N_HEADS, HEAD_DIM, win).transpose(0, 1, 4, 2, 3)
    return (yp.reshape(B, S, D_MODEL), ys.reshape(DB, L, D_MODEL), lru_conv_p, lru_conv_s, lru_h_p, lru_h_s,
            k_p, k_s, v_p, v_s, ssd_conv_p, ssd_conv_s, ssd_h_p, ssd_h_s)
```

```python
import functools

import numpy as np
import jax
import jax.numpy as jnp
from jax import lax
from jax.experimental import pallas as pl
from jax.experimental.pallas import tpu as pltpu

F32 = jnp.float32
BF16 = jnp.bfloat16

D_MODEL = 1024
CONV_W = 4
EPS = 1e-6
D_LRU = 512
LRU_BLOCKS = 8
LRU_C = 8.0
HEAD_DIM = 64
D_ATTN = 512
N_HEADS = 8
DILATIONS = ((128, 1), (512, 4), (2048, 16))
MAX_WINDOW = 2048
ATTN_BLOCK = 128
D_SSD = 1024
SSD_P = 64
SSD_HEADS = 16
SSD_GROUPS = 2
SSD_N = 128
SSD_CHUNK = 128
D_XBC = D_SSD + 2 * SSD_GROUPS * SSD_N
GROUP_W = D_SSD // SSD_GROUPS

OFF_XLRU, OFF_GLRU, OFF_Q, OFF_K, OFF_V, OFF_GATTN, OFF_Z, OFF_XBC, OFF_DT = (
    0, 512, 1024, 1536, 2048, 2560, 3072, 4096, 5632)
D_IN = 5648
LANES = 128
D_IN_PAD = OFF_DT + LANES

SPAN = 2048
NRES = SPAN // ATTN_BLOCK
SSD_PROMPT_CHUNKS = 2
SSD_SAMPLE_SEQS = 8
MASKED = 1e33
NEG = -1e30

VMEM_LIMIT = 56 * 1024 * 1024


def _cparams(sem):
    return pltpu.CompilerParams(dimension_semantics=sem, vmem_limit_bytes=VMEM_LIMIT)


def _silu(x):
    return x * jax.nn.sigmoid(x)


def _softplus(x):
    return jnp.maximum(x, 0.0) + jnp.log1p(jnp.exp(-jnp.abs(x)))


def _split3(x):
    h1 = x.astype(BF16)
    r1 = x - h1.astype(F32)
    h2 = r1.astype(BF16)
    h3 = (r1 - h2.astype(F32)).astype(BF16)
    return h1, h2, h3


def _dot3_right(x, m3_bf16):
    return jnp.dot(jnp.concatenate(_split3(x), axis=1), m3_bf16, preferred_element_type=F32)


def _dot3_left(m_bf16, x):
    return jnp.dot(jnp.concatenate([m_bf16] * 3, axis=1), jnp.concatenate(_split3(x), axis=0),
                   preferred_element_type=F32)


def _dot_nt(a, b):
    return lax.dot_general(a, b, (((1,), (1,)), ((), ())), preferred_element_type=F32)


def _dot_tn(a, b):
    return lax.dot_general(a, b, (((0,), (0,)), ((), ())), preferred_element_type=F32)


def _proj_kernel(x_ref, g_ref, w_ref, wdt_ref, *rest, first_win_tile, n_alias, cache_li, prompt):
    if prompt:
        lcw_ref, lcb_ref, scw_ref, scb_ref = rest[:4]
        rest = rest[4 + n_alias:]
        kt_ref, vt_ref, tail_lru_ref, tail_xbc_ref, xe_lru, xe_xbc = rest[9:]
    xlru_ref, glru_ref, q_ref, k_ref, v_ref, gattn_ref, z_ref, xbc_ref, dt_ref = rest[:9]
    n = x_ref.shape[0]
    x = x_ref[...]
    h = x * lax.rsqrt(jnp.mean(x * x, axis=-1, keepdims=True) + EPS) * g_ref[...]
    hb = h.astype(BF16)

    def mm(off, width):
        return jnp.dot(hb, w_ref[:, off:off + width], preferred_element_type=F32)

    def conv(raw, xe_ref, cw_ref, cb_ref, tail_ref, cols):
        xe_ref[8:8 + n, cols] = raw
        out = _conv_rows(xe_ref, n, cw_ref, cb_ref, cols)
        last = xe_ref[n:n + 8, cols]
        xe_ref[0:8, cols] = last
        tail_ref[:, cols] = last
        return out

    def pairs(dst, val):
        for c in range(D_ATTN // LANES):
            dst[c] = val[:, c * LANES:(c + 1) * LANES]

    dt_of = lambda: jnp.dot(hb, wdt_ref[...], preferred_element_type=F32)
    if prompt:
        @pl.when(pl.program_id(1) == 0)
        def _():
            xe_lru[0:8, :] = jnp.zeros((8, D_LRU), F32)
            xe_xbc[0:8, :] = jnp.zeros((8, D_XBC), F32)

        pw = 512
        cols = [slice(j * pw, (j + 1) * pw) for j in range(D_XBC // pw)]
        xbc_conv = lambda raw, c: _silu(conv(raw, xe_xbc, scw_ref, scb_ref, tail_xbc_ref, c))
        raw = mm(OFF_XBC, pw)
        pairs(q_ref, mm(OFF_Q, D_ATTN) * (HEAD_DIM ** -0.5))
        xbc_ref[:, cols[0]] = xbc_conv(raw, cols[0])
        raw = mm(OFF_XBC + pw, pw)
        k = mm(OFF_K, D_ATTN)
        xbc_ref[:, cols[1]] = xbc_conv(raw, cols[1])
        raw = mm(OFF_XBC + 2 * pw, pw)
        v = mm(OFF_V, D_ATTN)
        xbc_ref[:, cols[2]] = xbc_conv(raw, cols[2])
        pairs(k_ref, k)
        pairs(v_ref, v)
        raw = mm(OFF_XLRU, D_LRU)
        gattn_ref[...] = mm(OFF_GATTN, D_ATTN)
        xlru_ref[...] = conv(raw, xe_lru, lcw_ref, lcb_ref, tail_lru_ref, slice(None))
        z0 = mm(OFF_Z, pw)
        g_lru = mm(OFF_GLRU, D_LRU)
        z_ref[:, :pw] = _silu(z0)
        z1 = mm(OFF_Z + pw, D_SSD - pw)
        dt_ref[...] = dt_of()
        z_ref[:, pw:] = _silu(z1)
        glru_ref[...] = _silu(g_lru)
    else:
        xbc_ref[...] = mm(OFF_XBC, D_XBC)
        xlru_ref[...] = mm(OFF_XLRU, D_LRU)
        z_ref[...] = mm(OFF_Z, D_SSD)
        glru_ref[...] = mm(OFF_GLRU, D_LRU)
        pairs(q_ref, mm(OFF_Q, D_ATTN) * (HEAD_DIM ** -0.5))
        k = mm(OFF_K, D_ATTN)
        v = mm(OFF_V, D_ATTN)
        pairs(k_ref, k)
        pairs(v_ref, v)
        gattn_ref[...] = mm(OFF_GATTN, D_ATTN)
        dt_ref[...] = dt_of()
    if prompt:
        @pl.when(pl.program_id(1) >= first_win_tile)
        def _():
            for dst, val in ((kt_ref, k.T), (vt_ref, v.T)):
                if dst.ndim == 2:
                    dst[...] = val
                else:
                    for l in range(dst.shape[0]):
                        dst[l] = val if l == cache_li else jnp.zeros_like(val)


def _proj(x3, g, w_bf16, wdt_bf16, tm, cache=None):
    B, S, _ = x3.shape
    row = lambda width: pl.BlockSpec((None, tm, width), lambda b, t: (b, t, 0))
    const = lambda r, c: pl.BlockSpec((r, c), lambda b, t: (0, 0))
    pair = pl.BlockSpec((D_ATTN // LANES, None, tm, LANES), lambda b, t: (0, b, t, 0))
    pair_shape = (4, B, S, LANES)
    sds = lambda *s: jax.ShapeDtypeStruct(s, F32)
    weight = pl.BlockSpec((D_MODEL, D_IN), lambda b, t: (0, 0), pipeline_mode=pl.Buffered(1))
    in_specs = [row(D_MODEL), const(1, D_MODEL), weight, const(D_MODEL, LANES)]
    args = [x3, g, w_bf16, wdt_bf16]
    out_specs = [row(D_LRU), row(D_LRU), pair, pair, pair, row(D_ATTN), row(D_SSD), row(D_XBC), row(LANES)]
    out_shape = [sds(B, S, D_LRU), sds(B, S, D_LRU), sds(*pair_shape), sds(*pair_shape), sds(*pair_shape),
                 sds(B, S, D_ATTN), sds(B, S, D_SSD), sds(B, S, D_XBC), sds(B, S, LANES)]
    first_win_tile, n_alias, aliases, cache_li, scratch = 0, 0, {}, 0, []
    if cache is not None:
        li, depth, win, prev_kt, prev_vt, lcw, lcb, scw, scb = cache
        in_specs += [const(CONV_W, D_LRU), const(1, D_LRU), const(CONV_W, D_XBC), const(1, D_XBC)]
        args += [lcw, lcb, scw, scb]
        first_win_tile = (S - win) // tm
        cache_li = li
        win_tile = lambda t: jnp.maximum(t - first_win_tile, 0)
        if prev_kt is None:
            cspec = pl.BlockSpec((depth, None, D_ATTN, tm), lambda b, t: (0, b, 0, win_tile(t)))
        else:
            cspec = pl.BlockSpec((None, None, D_ATTN, tm), lambda b, t: (li, b, 0, win_tile(t)))
        tail = lambda width: pl.BlockSpec((None, 8, width), lambda b, t: (b, 0, 0))
        out_specs += [cspec, cspec, tail(D_LRU), tail(D_XBC)]
        out_shape += [sds(depth, B, D_ATTN, win), sds(depth, B, D_ATTN, win), sds(B, 8, D_LRU), sds(B, 8, D_XBC)]
        scratch = [pltpu.VMEM((tm + 8, D_LRU), F32), pltpu.VMEM((tm + 8, D_XBC), F32)]
        if prev_kt is not None:
            n_alias = 2
            in_specs += [pl.BlockSpec(memory_space=pl.ANY)] * 2
            args += [prev_kt, prev_vt]
            aliases = {len(args) - 2: 9, len(args) - 1: 10}
    return pl.pallas_call(
        functools.partial(_proj_kernel, first_win_tile=first_win_tile, n_alias=n_alias, cache_li=cache_li,
                          prompt=cache is not None),
        grid=(B, S // tm),
        in_specs=in_specs,
        out_specs=out_specs,
        out_shape=out_shape,
        scratch_shapes=scratch,
        input_output_aliases=aliases,
        compiler_params=_cparams(("arbitrary", "arbitrary")),
        name="in_proj",
    )(*args)


def _lru_gates(xc, wg_ref, bg_ref, lam_ref):
    pre = jnp.dot(xc.astype(BF16), wg_ref[...], preferred_element_type=F32) + bg_ref[...]
    r = jax.nn.sigmoid(pre[:, :D_LRU])
    i = jax.nn.sigmoid(pre[:, D_LRU:])
    log_a = (-LRU_C) * r * _softplus(-lam_ref[...])
    a = jnp.exp(log_a)
    t = jnp.tanh(log_a)
    return a, jnp.sqrt(-2.0 * t / (1.0 - t)) * (i * xc)


def _scan_rows(a, b, h0):
    n, width = a.shape
    a = a.reshape(n // 8, 8, width)
    b = b.reshape(n // 8, 8, width)
    row = lax.broadcasted_iota(jnp.int32, a.shape, 1)
    k = 1
    while k < 8:
        keep = row >= k
        a_sh = jnp.where(keep, pltpu.roll(a, k, axis=1), 1.0)
        b_sh = jnp.where(keep, pltpu.roll(b, k, axis=1), 0.0)
        b = a * b_sh + b
        a = a * a_sh
        k *= 2
    blocks = []
    h = h0
    for j in range(n // 8):
        blk = b[j] + a[j] * h
        blocks.append(blk)
        h = blk[7:8, :]
    return jnp.concatenate(blocks, axis=0), h


def _conv_rows(xe_ref, n, w_ref, b_ref, cols=slice(None)):
    y = b_ref[:, cols] + xe_ref[8:8 + n, cols] * w_ref[3:4, cols]
    for tap in range(CONV_W - 1):
        y = y + xe_ref[5 + tap:5 + tap + n, cols] * w_ref[tap:tap + 1, cols]
    return y


def _lru_p_kernel(xc_ref, gate_ref, wg_ref, bg_ref, lam_ref, y_ref, hl_ref, h_ref):
    @pl.when(pl.program_id(1) == 0)
    def _():
        h_ref[...] = jnp.zeros_like(h_ref)

    a, b = _lru_gates(xc_ref[...], wg_ref, bg_ref, lam_ref)
    hs, h_last = _scan_rows(a, b, h_ref[0:1, :])
    h_ref[...] = jnp.broadcast_to(h_last, h_ref.shape)
    hl_ref[...] = h_last
    y_ref[...] = (hs * gate_ref[...]).astype(y_ref.dtype)


def _lru_prompt(xc, gate, wg, bg, lam, tt):
    B, S, _ = xc.shape
    tile = pl.BlockSpec((None, tt, D_LRU), lambda b, t: (b, t, 0))
    const = lambda r, c: pl.BlockSpec((r, c), lambda b, t: (0, 0))
    return pl.pallas_call(
        _lru_p_kernel,
        grid=(B, S // tt),
        in_specs=[tile, tile, const(D_LRU, 2 * D_LRU), const(1, 2 * D_LRU), const(1, D_LRU)],
        out_specs=[tile, pl.BlockSpec((None, 1, D_LRU), lambda b, t: (b, 0, 0))],
        out_shape=[jax.ShapeDtypeStruct((B, S, D_LRU), BF16), jax.ShapeDtypeStruct((B, 1, D_LRU), F32)],
        scratch_shapes=[pltpu.VMEM((8, D_LRU), F32)],
        compiler_params=_cparams(("arbitrary", "arbitrary")),
        name="lru_prompt",
    )(xc, gate, wg, bg, lam)


def _lru_s_kernel(x_ref, g_ref, cs_ref, h0_ref, cw_ref, cb_ref, wg_ref, bg_ref, lam_ref, y_ref, hl_ref):
    L = x_ref.shape[0]
    db = x_ref.shape[1]
    xs = [cs_ref[i] for i in range(CONV_W - 1)] + [x_ref[i] for i in range(L)]
    xc = []
    for t in range(L):
        y = cb_ref[...] + xs[t] * cw_ref[0:1, :]
        for tap in range(1, CONV_W):
            y = y + xs[t + tap] * cw_ref[tap:tap + 1, :]
        xc.append(y)
    a, b = _lru_gates(jnp.concatenate(xc, axis=0), wg_ref, bg_ref, lam_ref)
    h = h0_ref[...]
    for t in range(L):
        h = a[t * db:(t + 1) * db] * h + b[t * db:(t + 1) * db]
        y_ref[t] = h * _silu(g_ref[t])
    hl_ref[...] = h


def _lru_sample(x_tm, g_tm, cs_tm, h0, cw, cb, wg, bg, lam):
    L, DB, _ = x_tm.shape
    return pl.pallas_call(
        _lru_s_kernel,
        out_shape=[jax.ShapeDtypeStruct((L, DB, D_LRU), F32), jax.ShapeDtypeStruct((DB, D_LRU), F32)],
        compiler_params=pltpu.CompilerParams(vmem_limit_bytes=VMEM_LIMIT),
        name="lru_sample",
    )(x_tm, g_tm, cs_tm, h0, cw, cb, wg, bg, lam)


def _block_order(rho, dil):
    if dil == 16:
        return rho
    if dil == 4:
        return 4 * (rho & 31) + (rho >> 5)
    return 16 * (rho & 7) + (rho >> 3)


def _chunks(dil):
    return {16: (1, 128), 4: (4, 32), 1: (16, 8)}[dil]


def _attn_p_kernel(q_ref, k_ref, v_ref, o_ref, qs, ks, vs, r_st, m_st, bias_ref, *, unroll):
    c = pl.program_id(1)
    s = pl.program_id(2)
    nres = SPAN // ATTN_BLOCK
    lane_q = lax.broadcasted_iota(jnp.int32, (ATTN_BLOCK, LANES), 1)
    first_half = lane_q < HEAD_DIM

    cur = s & 1
    prev = 1 - cur

    @pl.when(s == 0)
    def _():
        ks[prev] = jnp.zeros((SPAN, LANES), F32)
        vs[0, prev] = jnp.zeros((SPAN, LANES), F32)
        vs[1, prev] = jnp.zeros((SPAN, LANES), F32)

    pr = lax.broadcasted_iota(jnp.int32, (ATTN_BLOCK, ATTN_BLOCK), 0)
    pc = lax.broadcasted_iota(jnp.int32, (ATTN_BLOCK, ATTN_BLOCK), 1)
    perm = jnp.where(pc == nres * (pr & 7) + (pr >> 3), 1.0, 0.0).astype(BF16)
    for g in range(SPAN // ATTN_BLOCK):
        src = slice(g * ATTN_BLOCK, (g + 1) * ATTN_BLOCK)
        move = lambda ref: jnp.dot(perm, ref[src, :].astype(BF16), preferred_element_type=F32)
        q, k, v = move(q_ref), move(k_ref), move(v_ref)
        pieces = ((qs, (0,), jnp.where(first_half, q, 0.0)), (qs, (1,), jnp.where(first_half, 0.0, q)),
                  (ks, (cur,), k),
                  (vs, (0, cur), jnp.where(first_half, v, 1.0)), (vs, (1, cur), jnp.where(first_half, 1.0, v)))
        for r in range(nres):
            for ref, lead, val in pieces:
                ref[lead + (slice(r * ATTN_BLOCK + 8 * g, r * ATTN_BLOCK + 8 * g + 8), slice(None))] = (
                    val[8 * r:8 * r + 8, :])

    span0 = s == 0

    @pl.when(span0)
    def _():
        iq = lax.broadcasted_iota(jnp.int32, (ATTN_BLOCK, 2 * ATTN_BLOCK), 0)
        ik = lax.broadcasted_iota(jnp.int32, (ATTN_BLOCK, 2 * ATTN_BLOCK), 1)
        in_cur = ik >= ATTN_BLOCK
        for di, (window, dil) in enumerate(DILATIONS):
            j = ATTN_BLOCK + _block_order(iq, dil) - (
                jnp.where(in_cur, ATTN_BLOCK, 0) + _block_order(ik & (ATTN_BLOCK - 1), dil))
            valid = (j >= 0) & (j <= window // dil)
            dist = (j * dil).astype(F32)
            dist_all = jnp.where(valid, dist, MASKED)
            dist_cur = jnp.where(valid & in_cur, dist, MASKED)
            for hh in range(2):
                slope = jnp.exp2(-(jnp.zeros((1, 1), F32) + (2 * c + hh + 1).astype(F32)))
                bias_ref[di, hh, 0] = -slope * dist_all
                bias_ref[di, hh, 1] = -slope * dist_cur

    def block(di, dil, it):
        nchunk, crow = _chunks(dil)

        def gather(ref, lead, starts):
            parts = [ref[lead + (pl.ds(st, crow), slice(None))] for st in starts]
            return parts[0] if len(parts) == 1 else jnp.concatenate(parts, axis=0)

        if dil == 16:
            base, cb, last_off = it * ATTN_BLOCK, 0, 0
        elif dil == 4:
            base, cb, last_off = (it >> 2) * ATTN_BLOCK, it & 3, 3 * crow
        else:
            base, cb, last_off = 0, it, 15 * crow
        stride = {16: 0, 4: 4 * ATTN_BLOCK, 1: ATTN_BLOCK}[dil]
        cur_starts = [pl.multiple_of(base + a * stride + crow * cb, 8) for a in range(nchunk)]
        if dil == 16:
            prev_slot = prev
            prev_starts = cur_starts
            first_in_span = True
        else:
            has_prev = cb > 0
            prev_slot = jnp.where(has_prev, cur, prev)
            prev_off = jnp.where(has_prev, crow * (cb - 1), last_off)
            prev_starts = [pl.multiple_of(base + a * stride + prev_off, 8) for a in range(nchunk)]
            first_in_span = jnp.logical_not(has_prev)
        kb = jnp.concatenate([gather(ks, (prev_slot,), prev_starts), gather(ks, (cur,), cur_starts)],
                             axis=0).astype(BF16)
        variant = jnp.where(jnp.logical_and(span0, first_in_span), 1, 0)
        for hh in range(2):
            qh = gather(qs, (hh,), cur_starts).astype(BF16)
            vh = jnp.concatenate([gather(vs, (hh, prev_slot), prev_starts), gather(vs, (hh, cur), cur_starts)],
                                 axis=0).astype(BF16)
            sc = _dot_nt(qh, kb) + bias_ref[di, hh, variant]
            m = jnp.max(sc, axis=1, keepdims=True)
            p = jnp.exp(sc - m)
            r_new = jnp.dot(p.astype(BF16), vh, preferred_element_type=F32)
            m_new = jnp.broadcast_to(m, (ATTN_BLOCK, LANES))
            for a, st in enumerate(cur_starts):
                r_st[di, hh, pl.ds(st, crow), :] = r_new[a * crow:(a + 1) * crow, :]
                m_st[di, hh, pl.ds(st, crow), :] = m_new[a * crow:(a + 1) * crow, :]

    def body(it, carry):
        for di, (_, dil) in enumerate(DILATIONS):
            block(di, dil, it)
        return carry

    lax.fori_loop(0, nres, body, 0, unroll=unroll)

    ndil = len(DILATIONS)
    for r in range(nres):
        src = slice(r * ATTN_BLOCK, (r + 1) * ATTN_BLOCK)
        outs = []
        for hh in range(2):
            ms = [m_st[di, hh, src, :] for di in range(ndil)]
            m_tot = functools.reduce(jnp.maximum, ms)
            tot = sum(r_st[di, hh, src, :] * jnp.exp(ms[di] - m_tot) for di in range(ndil))
            outs.append(tot / pltpu.roll(tot, HEAD_DIM, axis=1))
        o_ref[pl.ds(r, ATTN_BLOCK, stride=nres), :] = jnp.where(first_half, outs[0], outs[1])


def _attn_prompt(q4, k4, v4, unroll=8):
    npair, B, S, _ = q4.shape
    ndil = len(DILATIONS)
    spec = pl.BlockSpec((None, None, SPAN, LANES), lambda b, c, s: (c, b, s, 0))
    return pl.pallas_call(
        functools.partial(_attn_p_kernel, unroll=unroll),
        grid=(B, npair, S // SPAN),
        in_specs=[spec, spec, spec],
        out_specs=spec,
        out_shape=jax.ShapeDtypeStruct(q4.shape, F32),
        scratch_shapes=[pltpu.VMEM((2, SPAN, LANES), F32), pltpu.VMEM((2, SPAN, LANES), F32),
                        pltpu.VMEM((2, 2, SPAN, LANES), F32),
                        pltpu.VMEM((ndil, 2, SPAN, LANES), F32), pltpu.VMEM((ndil, 2, SPAN, LANES), F32),
                        pltpu.VMEM((ndil, 2, 2, ATTN_BLOCK, 2 * ATTN_BLOCK), F32)],
        compiler_params=_cparams(("arbitrary", "arbitrary", "arbitrary")),
        name="attn_prompt",
    )(q4, k4, v4)


def _attn_s_kernel(*refs, n_new):
    _attn_s_body(*refs, n_new=n_new, first=pl.program_id(0) == 0)


def _attn_s_body(q_ref, kn_ref, vn_ref, kc_ref, vc_ref, o_ref, dist_ref, mult_ref, *, n_new, first):
    nseq, nheads, hd, W = kc_ref.shape
    rows = q_ref.shape[1]
    G = 4
    gr, gl = G * rows, G * hd
    shift = n_new.bit_length() - 1

    @pl.when(first)
    def _():
        i = lax.broadcasted_iota(jnp.int32, (gr, W), 0) & (n_new - 1)
        pos = lax.broadcasted_iota(jnp.int32, (gr, W), 1)
        d = W + i - pos
        cnt = jnp.zeros((gr, W), F32)
        for window, dil in DILATIONS:
            cnt = cnt + jnp.where(((d & (dil - 1)) == 0) & (d <= window), 1.0, 0.0)
        dist_ref[...] = d.astype(F32)
        mult_ref[...] = cnt

    r = lax.broadcasted_iota(jnp.int32, (gr, rows), 0) & (rows - 1)
    r2 = lax.broadcasted_iota(jnp.int32, (gr, rows), 1)
    dn = (r & (n_new - 1)) - (r2 & (n_new - 1))
    same_seq = (r >> shift) == (r2 >> shift)
    mult_n = jnp.zeros((gr, rows), F32)
    for window, dil in DILATIONS:
        ok = ((dn & (dil - 1)) == 0) & (dn >= 0) & (dn <= window) & same_seq
        mult_n = mult_n + jnp.where(ok, 1.0, 0.0)
    dist_n = dn.astype(F32)
    dist_c = dist_ref[...]
    mult_c = mult_ref[...]
    row_head = lax.broadcasted_iota(jnp.int32, (gr, gl), 0) >> 3
    hd_shift = hd.bit_length() - 1
    lane_head = lax.broadcasted_iota(jnp.int32, (gr, gl), 1) >> hd_shift
    diag = row_head == lane_head
    out_lane_head = lax.broadcasted_iota(jnp.int32, (rows, gl), 1) >> hd_shift
    out_row_seq = lax.broadcasted_iota(jnp.int32, (rows, gl), 0) >> shift
    head_col = lax.broadcasted_iota(jnp.int32, (gr, 1), 0) >> 3
    npair_g = gl // LANES
    for g in range(nheads // G):
        pairs = range(g * npair_g, (g + 1) * npair_g)
        cat = lambda ref: jnp.concatenate([ref[c] for c in pairs], axis=1)
        q_bd = jnp.where(diag, jnp.concatenate([cat(q_ref)] * G, axis=0), 0.0).astype(BF16)
        slope = jnp.exp2(-(head_col + (g * G + 1)).astype(F32))
        sn = _dot_nt(q_bd, cat(kn_ref).astype(BF16))
        sn = jnp.where(mult_n > 0, sn - slope * dist_n, NEG)
        mn = jnp.max(sn, axis=1, keepdims=True)
        vn = cat(vn_ref).astype(BF16)
        o_g = jnp.zeros((rows, gl), F32)
        for b in range(nseq):
            kst = kc_ref[b, g * G:(g + 1) * G].reshape(gl, W).astype(BF16)
            vst = vc_ref[b, g * G:(g + 1) * G].reshape(gl, W).astype(BF16)
            sc = jnp.dot(q_bd, kst, preferred_element_type=F32)
            sc = jnp.where(mult_c > 0, sc - slope * dist_c, NEG)
            m = jnp.maximum(jnp.max(sc, axis=1, keepdims=True), mn)
            pc = mult_c * jnp.exp(sc - m)
            pn = mult_n * jnp.exp(sn - m)
            l = jnp.sum(pc, axis=1, keepdims=True) + jnp.sum(pn, axis=1, keepdims=True)
            o = (_dot_nt(pc.astype(BF16), vst) + jnp.dot(pn.astype(BF16), vn, preferred_element_type=F32)) / l
            sel = sum(jnp.where(out_lane_head == h, o[h * rows:(h + 1) * rows, :], 0.0) for h in range(G))
            o_g = jnp.where(out_row_seq == b, sel, o_g)
        for j, c in enumerate(pairs):
            o_ref[c] = o_g[:, j * LANES:(j + 1) * LANES]


def _attn_sample(q4, k4, v4, kc_t, vc_t, li, n_new):
    small, cache, tables = _attn_s_specs(q4, kc_t, li, n_new, lambda b: b)
    return pl.pallas_call(
        functools.partial(_attn_s_kernel, n_new=n_new),
        grid=(q4.shape[1] // 8,),
        in_specs=[small, small, small, cache, cache],
        out_specs=small,
        out_shape=jax.ShapeDtypeStruct(q4.shape, F32),
        scratch_shapes=tables,
        compiler_params=_cparams(("arbitrary",)),
        name="attn_sample",
    )(q4, k4, v4, kc_t, vc_t)


def _attn_s_specs(q4, kc_t, li, n_new, group_of):
    npair, T, _ = q4.shape
    W = kc_t.shape[-1]
    rows = 8
    assert n_new & (n_new - 1) == 0 and rows % n_new == 0 and T % rows == 0
    small = pl.BlockSpec((npair, rows, LANES), lambda *g: (0, group_of(*g), 0))
    cache = pl.BlockSpec((None, rows // n_new, N_HEADS, HEAD_DIM, W), lambda *g: (li, group_of(*g), 0, 0, 0))
    return small, cache, [pltpu.VMEM((4 * rows, W), F32), pltpu.VMEM((4 * rows, W), F32)]


def _group_norm_out(y, z, ng_ref, gate_ready=False):
    yg = y * (z if gate_ready else _silu(z))
    outs = []
    for g in range(SSD_GROUPS):
        part = yg[:, g * GROUP_W:(g + 1) * GROUP_W]
        outs.append(part * lax.rsqrt(jnp.mean(part * part, axis=-1, keepdims=True) + EPS))
    return jnp.concatenate(outs, axis=1) * ng_ref[...]


def _ssd_p_kernel(*refs, n_new):
    fused = n_new is not None
    n_in = 13 if fused else 8
    y_ref, hl_ref = refs[n_in:n_in + 2]
    scratch = refs[n_in + (3 if fused else 2):]
    _ssd_p_tile(*refs[:8], y_ref, hl_ref, scratch[0])
    if fused:
        first = jnp.logical_and(pl.program_id(0) == 0, pl.program_id(1) == 0)
        _attn_s_body(*refs[8:13], refs[n_in + 2], scratch[1], scratch[2], n_new=n_new, first=first)


def _ssd_p_tile(xc_ref, dt_ref, gate_ref, dtb_ref, alog_ref, dx_ref, ng_ref, e_ref, y_ref, hl_ref, ht_ref):
    t = pl.program_id(1)
    Q = SSD_CHUNK
    n = xc_ref.shape[0]

    @pl.when(t == 0)
    def _():
        ht_ref[...] = jnp.zeros_like(ht_ref)

    xc_all = xc_ref[...]
    dt_all = _softplus(dt_ref[...] + dtb_ref[...])
    a_neg = -jnp.exp(alog_ref[...])
    ri = lax.broadcasted_iota(jnp.int32, (Q, Q), 0)
    ci = lax.broadcasted_iota(jnp.int32, (Q, Q), 1)
    causal = ri >= ci
    tri = jnp.where(causal, 1.0, 0.0).astype(BF16)
    lane = lax.broadcasted_iota(jnp.int32, (Q, LANES), 1)
    for ck in range(n // Q):
        rows = slice(ck * Q, (ck + 1) * Q)
        y = _ssd_chunk(xc_all[rows, :], dt_all[rows, :], a_neg, causal, tri, lane, dx_ref, e_ref, ht_ref)
        y_ref[rows, :] = _group_norm_out(y, gate_ref[rows, :], ng_ref, gate_ready=True).astype(y_ref.dtype)

    @pl.when(t == pl.num_programs(1) - 1)
    def _():
        for g in range(SSD_GROUPS):
            hl_ref[g * (SSD_HEADS // SSD_GROUPS):(g + 1) * (SSD_HEADS // SSD_GROUPS)] = (
                ht_ref[g].T.reshape(SSD_HEADS // SSD_GROUPS, SSD_P, SSD_N))


def _ssd_chunk(xc, dt, a_neg, causal, tri, lane, dx_ref, e_ref, ht_ref):
    Q = SSD_CHUNK
    xs = xc[:, :D_SSD]
    bm = xc[:, D_SSD:D_SSD + SSD_GROUPS * SSD_N]
    cm = xc[:, D_SSD + SSD_GROUPS * SSD_N:]
    cum = _dot3_left(tri, dt * a_neg)
    cum_t = cum.T
    both_x = _dot3_right(jnp.concatenate([cum, dt], axis=0), e_ref[...])
    cum_x = both_x[:Q]
    dt_x = both_x[Q:]
    xdt = xs * dt_x
    ecum = jnp.exp(cum_x)
    cl = cum_x[Q - 1:Q, :]
    w_end = xdt * jnp.exp(cl - cum_x)
    dec = jnp.exp(cl)

    y_parts = []
    for g in range(SSD_GROUPS):
        cg = cm[:, g * SSD_N:(g + 1) * SSD_N].astype(BF16)
        bg = bm[:, g * SSD_N:(g + 1) * SSD_N]
        cb = _dot_nt(cg, bg.astype(BF16))
        gl = slice(g * GROUP_W, (g + 1) * GROUP_W)
        h_old = ht_ref[g]
        y_off = jnp.dot(cg, h_old.astype(BF16), preferred_element_type=F32) * ecum[:, gl]
        ht_ref[g] = h_old * dec[:, gl] + jnp.dot(bg.T.astype(BF16), w_end[:, gl].astype(BF16),
                                                  preferred_element_type=F32)
        for pp in range(GROUP_W // LANES):
            pl_ = slice(g * GROUP_W + pp * LANES, g * GROUP_W + (pp + 1) * LANES)
            xdt_pair = xdt[:, pl_].astype(BF16)
            halves = []
            for hh in range(2):
                h = (g * GROUP_W + pp * LANES) // SSD_P + hh
                seg = jnp.broadcast_to(cum[:, h:h + 1], (Q, Q)) - jnp.broadcast_to(cum_t[h:h + 1, :], (Q, Q))
                scores = cb * jnp.exp(jnp.where(causal, seg, NEG))
                halves.append(jnp.dot(scores.astype(BF16), xdt_pair, preferred_element_type=F32))
            y_diag = jnp.where(lane < SSD_P, halves[0], halves[1])
            y_parts.append(y_diag + y_off[:, pp * LANES:(pp + 1) * LANES])
    return jnp.concatenate(y_parts, axis=1) + dx_ref[...] * xs


def _ssd_prompt(xc, dt, gate, dtb, alog, dx, ng, e, sample_attn=None):
    B, S, _ = xc.shape
    Q = SSD_PROMPT_CHUNKS * SSD_CHUNK
    nt = S // Q
    tile = lambda w: pl.BlockSpec((None, Q, w), lambda b, t: (b, t, 0))
    const = lambda r, c: pl.BlockSpec((r, c), lambda b, t: (0, 0))
    in_specs = [tile(D_XBC), tile(LANES), tile(D_SSD), const(1, LANES), const(1, LANES), const(1, D_SSD),
                const(1, D_SSD), const(3 * LANES, D_SSD)]
    args = [xc, dt, gate, dtb, alog, dx, ng, e]
    out_specs = [tile(D_SSD), pl.BlockSpec((None, SSD_HEADS, SSD_P, SSD_N), lambda b, t: (b, 0, 0, 0))]
    out_shape = [jax.ShapeDtypeStruct((B, S, D_SSD), BF16), jax.ShapeDtypeStruct((B, SSD_HEADS, SSD_P, SSD_N), F32)]
    scratch = [pltpu.VMEM((SSD_GROUPS, SSD_N, GROUP_W), F32)]
    n_new = None
    if sample_attn is not None:
        q4, k4, v4, kc_t, vc_t, li, n_new = sample_attn
        small, cache, tables = _attn_s_specs(q4, kc_t, li, n_new, lambda b, t: b * nt + t)
        assert q4.shape[1] // 8 == B * nt
        in_specs += [small, small, small, cache, cache]
        args += [q4, k4, v4, kc_t, vc_t]
        out_specs.append(small)
        out_shape.append(jax.ShapeDtypeStruct(q4.shape, F32))
        scratch += tables
    return pl.pallas_call(
        functools.partial(_ssd_p_kernel, n_new=n_new),
        grid=(B, nt),
        in_specs=in_specs,
        out_specs=out_specs,
        out_shape=out_shape,
        scratch_shapes=scratch,
        compiler_params=_cparams(("arbitrary", "arbitrary")),
        name="ssd_prompt" if sample_attn is None else "ssd_prompt_attn_sample",
    )(*args)


def _ssd_s_kernel(xbc_ref, cs_ref, dt_ref, z_ref, h_ref, cw_ref, cb_ref, dtb_ref, alog_ref, dx_ref, ng_ref, e_ref,
                  *rest, n_new, li):
    y_ref, hn_ref = rest[-2:]
    if hn_ref.ndim == 5:
        for l in range(hn_ref.shape[0]):
            if l != li:
                hn_ref[l] = jnp.zeros(hn_ref.shape[1:], F32)
        hn_ref = hn_ref.at[li]
    for i in range(xbc_ref.shape[0]):
        _ssd_s_one(xbc_ref.at[i], cs_ref.at[i], dt_ref.at[i], z_ref.at[i], h_ref.at[i], cw_ref, cb_ref, dtb_ref,
                   alog_ref, dx_ref, ng_ref, e_ref, y_ref.at[i], hn_ref.at[i], n_new)


def _ssd_s_one(xbc_ref, cs_ref, dt_ref, z_ref, h_ref, cw_ref, cb_ref, dtb_ref, alog_ref, dx_ref, ng_ref, e_ref,
               y_ref, hn_ref, n_new):
    R = 8
    row = lax.broadcasted_iota(jnp.int32, (R, LANES), 0)
    real = row < n_new
    xe = jnp.concatenate([cs_ref[...], xbc_ref[...]], axis=0)
    y = cb_ref[...] + xe[8:16, :] * cw_ref[3:4, :]
    for tap in range(CONV_W - 1):
        y = y + xe[5 + tap:13 + tap, :] * cw_ref[tap:tap + 1, :]
    xc = _silu(y)
    xs = xc[:, :D_SSD]
    bm = xc[:, D_SSD:D_SSD + SSD_GROUPS * SSD_N]
    cm = xc[:, D_SSD + SSD_GROUPS * SSD_N:]

    dt = jnp.where(real, _softplus(dt_ref[...] + dtb_ref[...]), 0.0)
    da = dt * (-jnp.exp(alog_ref[...]))
    cum = da
    k = 1
    while k < R:
        cum = cum + jnp.where(row >= k, pltpu.roll(cum, k, axis=0), 0.0)
        k *= 2
    e = e_ref[...]
    cum_x = _dot3_right(cum, e)
    dt_x = _dot3_right(dt, e)
    xdt = xs * dt_x
    ecum = jnp.exp(cum_x)
    cl = cum_x[R - 1:R, :]
    w_end = xdt * jnp.exp(cl - cum_x)
    dec3 = _split3(jnp.exp(cl))
    ones = jnp.ones((R, SSD_N), BF16)
    zrow = jnp.zeros((R - 3, GROUP_W), BF16)

    y_parts = []
    for g in range(SSD_GROUPS):
        gl = slice(g * GROUP_W, (g + 1) * GROUP_W)
        cg = cm[:, g * SSD_N:(g + 1) * SSD_N]
        bg = bm[:, g * SSD_N:(g + 1) * SSD_N]
        hs = slice(g * (SSD_HEADS // SSD_GROUPS), (g + 1) * (SSD_HEADS // SSD_GROUPS))
        h_old = h_ref[hs].reshape(GROUP_W, SSD_N)
        y_g = _dot_nt(cg.astype(BF16), h_old.astype(BF16)) * ecum[:, gl]
        for j in range(n_new):
            cbj = jnp.sum(cg * bg[j:j + 1, :], axis=1, keepdims=True)
            term = cbj * jnp.exp(cum_x[:, gl] - cum_x[j:j + 1, gl]) * xdt[j:j + 1, gl]
            y_g = y_g + jnp.where(row[:, 0:1] >= j, term, 0.0)
        y_parts.append(y_g)
        dec_col = _dot_tn(jnp.concatenate([p[:, gl] for p in dec3] + [zrow], axis=0), ones)
        upd = _dot_tn(w_end[:, gl].astype(BF16), bg.astype(BF16))
        hn_ref[hs] = (h_old * dec_col + upd).reshape(SSD_HEADS // SSD_GROUPS, SSD_P, SSD_N)
    y = jnp.concatenate(y_parts, axis=1) + dx_ref[...] * xs
    y_ref[...] = _group_norm_out(y, z_ref[...], ng_ref)


def _ssd_sample(xbc, cs, dt, z, h_all, li, h_new_prev, cw, cb, dtb, alog, dx, ng, e, n_new):
    DB = xbc.shape[0]
    bb = SSD_SAMPLE_SEQS if DB % SSD_SAMPLE_SEQS == 0 else 1
    tile = lambda w: pl.BlockSpec((bb, 8, w), lambda b: (b, 0, 0))
    const = lambda r, c: pl.BlockSpec((r, c), lambda b: (0, 0))
    state = pl.BlockSpec((None, bb, SSD_HEADS, SSD_P, SSD_N), lambda b: (li, b, 0, 0, 0))
    in_specs = [tile(D_XBC), tile(D_XBC), tile(LANES), tile(D_SSD), state, const(CONV_W, D_XBC), const(1, D_XBC),
                const(1, LANES), const(1, LANES), const(1, D_SSD), const(1, D_SSD), const(3 * LANES, D_SSD)]
    args = [xbc, cs, dt, z, h_all, cw, cb, dtb, alog, dx, ng, e]
    aliases = {}
    if h_new_prev is not None:
        in_specs.append(pl.BlockSpec(memory_space=pl.ANY))
        args.append(h_new_prev)
        aliases = {len(args) - 1: 1}
        state_out = state
    else:
        state_out = pl.BlockSpec((h_all.shape[0], bb, SSD_HEADS, SSD_P, SSD_N), lambda b: (0, b, 0, 0, 0))
    return pl.pallas_call(
        functools.partial(_ssd_s_kernel, n_new=n_new, li=li),
        grid=(DB // bb,),
        in_specs=in_specs,
        out_specs=[tile(D_SSD), state_out],
        out_shape=[jax.ShapeDtypeStruct((DB, 8, D_SSD), F32), jax.ShapeDtypeStruct(h_all.shape, F32)],
        input_output_aliases=aliases,
        compiler_params=_cparams(("arbitrary",)),
        name="ssd_sample",
    )(*args)


def _out_kernel(x_ref, ylru_ref, o_ref, gattn_ref, yssd_ref, w_ref, g_ref, out_ref):
    o = jnp.concatenate([o_ref[c] for c in range(D_ATTN // LANES)], axis=1)
    y_attn = o * _silu(gattn_ref[...])
    mix_in = jnp.concatenate([ylru_ref[...].astype(BF16), y_attn.astype(BF16), yssd_ref[...].astype(BF16)], axis=1)
    mix = jnp.dot(mix_in, w_ref[...], preferred_element_type=F32)
    out_ref[...] = x_ref[...] + mix * lax.rsqrt(jnp.mean(mix * mix, axis=-1, keepdims=True) + EPS) * g_ref[...]


def _out_proj(x2d, y_lru, o4, g_attn, y_ssd, w_bf16, g, tm):
    T = x2d.shape[0]
    row = lambda width: pl.BlockSpec((tm, width), lambda i: (i, 0))
    o_spec = pl.BlockSpec((D_ATTN // LANES, tm, LANES), lambda i: (0, i, 0))
    return pl.pallas_call(
        _out_kernel,
        grid=(T // tm,),
        in_specs=[row(D_MODEL), row(D_LRU), o_spec,
                  row(D_ATTN), row(D_SSD), pl.BlockSpec((D_LRU + D_ATTN + D_SSD, D_MODEL), lambda i: (0, 0)),
                  pl.BlockSpec((1, D_MODEL), lambda i: (0, 0))],
        out_specs=row(D_MODEL),
        out_shape=jax.ShapeDtypeStruct((T, D_MODEL), F32),
        compiler_params=_cparams(("arbitrary",)),
        name="out_proj",
    )(x2d, y_lru, o4, g_attn, y_ssd, w_bf16, g)


def _block_diag(w):
    k, n, _ = w.shape
    return jnp.einsum("kij,kl->kilj", w, jnp.eye(k, dtype=w.dtype)).reshape(k * n, k * n)


def _pad_rows(a, rows):
    return jnp.pad(a, ((0, 0), (0, rows - a.shape[1]), (0, 0)))


def _pairs_to_heads(a4, nb, L):
    return a4.reshape(4, nb, L, 2, HEAD_DIM).transpose(1, 2, 0, 3, 4).reshape(nb, L, N_HEADS, HEAD_DIM)


def kernel(x_prompt, x_sample, state_lru_conv, state_lru_h, cache_attn_k, cache_attn_v, state_ssd_conv, state_ssd_h, pre_norm_g, post_norm_g, w_in, lru_conv_w, lru_conv_b, lru_w_a, lru_b_a, lru_w_x, lru_b_x, lru_lambda, ssd_conv_w, ssd_conv_b, ssd_dt_bias, ssd_a_log, ssd_d, ssd_norm_g, w_out):
    depth = w_in.shape[0]
    B, S, _ = x_prompt.shape
    DB, L, _ = x_sample.shape
    assert S % SPAN == 0 and L <= 8 and L >= CONV_W - 1
    win = min(MAX_WINDOW, S)

    lane_pad = lambda a: jnp.pad(a.reshape(1, -1), ((0, 0), (0, LANES - a.shape[-1])))
    expand = jnp.asarray(np.tile(np.repeat(np.eye(LANES, SSD_HEADS, dtype=np.float32), SSD_P, axis=1), (3, 1)), BF16)
    kc_t = jnp.transpose(cache_attn_k, (0, 1, 3, 4, 2))
    vc_t = jnp.transpose(cache_attn_v, (0, 1, 3, 4, 2))

    yp = x_prompt.reshape(B * S, D_MODEL)
    ys = x_sample.reshape(DB * L, D_MODEL)
    outs_p = []
    outs_s = []
    kt_p = vt_p = ssd_h_s = None
    for li in range(depth):
        w_in_b = w_in[li].astype(BF16)
        w_dt_b = jnp.pad(w_in[li][:, OFF_DT:], ((0, 0), (0, D_IN_PAD - D_IN))).astype(BF16)
        w_out_b = w_out[li].astype(BF16)
        pre_g = pre_norm_g[li].reshape(1, -1)
        post_g = post_norm_g[li].reshape(1, -1)
        wg = jnp.concatenate([_block_diag(lru_w_a[li]), _block_diag(lru_w_x[li])], axis=1).astype(BF16)
        bg = jnp.concatenate([lru_b_a[li], lru_b_x[li]]).reshape(1, -1)
        lcw, lcb, lam = lru_conv_w[li], lru_conv_b[li].reshape(1, -1), lru_lambda[li].reshape(1, -1)
        scw, scb = ssd_conv_w[li], ssd_conv_b[li].reshape(1, -1)
        dtb, alog = lane_pad(ssd_dt_bias[li]), lane_pad(ssd_a_log[li])
        dx = jnp.repeat(ssd_d[li], SSD_P).reshape(1, -1)
        ng = ssd_norm_g[li].reshape(1, -1)

        xc_lru, gate_lru, q4, k4, v4, gattn, gate_z, xc_ssd, dtr, kt_p, vt_p, tail_lru, tail_xbc = _proj(
            yp.reshape(B, S, D_MODEL), pre_g, w_in_b, w_dt_b, 512,
            cache=(li, depth, win, kt_p, vt_p, lcw, lcb, scw, scb))
        xlru_s, glru_s, q4s, k4s, v4s, gattn_s, z_s, xbc_s, dtr_s = [
            a.reshape(a.shape[:-3] + (DB * L, a.shape[-1]))
            for a in _proj(ys.reshape(1, DB * L, D_MODEL), pre_g, w_in_b, w_dt_b, min(256, DB * L))]

        y_lru, lru_h_p = _lru_prompt(xc_lru, gate_lru, wg, bg, lam, 512)
        o4 = _attn_prompt(q4, k4, v4)
        ssd_steps = B * (S // (SSD_PROMPT_CHUNKS * SSD_CHUNK))
        ride = L & (L - 1) == 0 and 8 % L == 0 and (DB * L) % 8 == 0 and (DB * L) // 8 == ssd_steps
        if ride:
            y_ssd, ssd_h_p, o_s = _ssd_prompt(xc_ssd, dtr, gate_z, dtb, alog, dx, ng, expand,
                                              sample_attn=(q4s, k4s, v4s, kc_t, vc_t, li, L))
        else:
            y_ssd, ssd_h_p = _ssd_prompt(xc_ssd, dtr, gate_z, dtb, alog, dx, ng, expand)
            o_s = _attn_sample(q4s, k4s, v4s, kc_t, vc_t, li, L)
        yp = _out_proj(yp, y_lru.reshape(B * S, D_LRU), o4.reshape(4, B * S, LANES), gattn.reshape(B * S, D_ATTN),
                       y_ssd.reshape(B * S, D_SSD), w_out_b, post_g, 512)
        outs_p.append((tail_lru[:, 8 - (CONV_W - 1):], lru_h_p.reshape(B, D_LRU), tail_xbc[:, 8 - (CONV_W - 1):],
                       ssd_h_p))

        xlru3 = xlru_s.reshape(DB, L, D_LRU)
        y_lru_tm, lru_h_s = _lru_sample(
            xlru3.transpose(1, 0, 2), glru_s.reshape(DB, L, D_LRU).transpose(1, 0, 2),
            state_lru_conv[li].transpose(1, 0, 2), state_lru_h[li], lcw, lcb, wg, bg, lam)
        y_lru = y_lru_tm.transpose(1, 0, 2).reshape(DB * L, D_LRU)

        xbc3 = xbc_s.reshape(DB, L, D_XBC)
        cs8 = jnp.pad(state_ssd_conv[li], ((0, 0), (8 - (CONV_W - 1), 0), (0, 0)))
        y_ssd8, ssd_h_s = _ssd_sample(_pad_rows(xbc3, 8), cs8, _pad_rows(dtr_s.reshape(DB, L, LANES), 8),
                                      _pad_rows(z_s.reshape(DB, L, D_SSD), 8), state_ssd_h, li, ssd_h_s, scw, scb,
                                      dtb, alog, dx, ng, expand, L)
        y_ssd = y_ssd8[:, :L].reshape(DB * L, D_SSD)
        ys = _out_proj(ys, y_lru, o_s, gattn_s, y_ssd, w_out_b, post_g, DB * L)
        outs_s.append((xlru3[:, L - (CONV_W - 1):], lru_h_s, _pairs_to_heads(k4s, DB, L),
                       _pairs_to_heads(v4s, DB, L), xbc3[:, L - (CONV_W - 1):]))

    lru_conv_p, lru_h_p, ssd_conv_p, ssd_h_p = [jnp.stack(c) for c in zip(*outs_p)]
    lru_conv_s, lru_h_s, k_s, v_s, ssd_conv_s = [jnp.stack(c) for c in zip(*outs_s)]
    k_p = kt_p.reshape(depth, B, N_HEADS, HEAD_DIM, win).transpose(0, 1, 4, 2, 3)
    v_p = vt_p.reshape(depth, B, N_HEADS, HEAD_DIM, win).transpose(0, 1, 4, 2, 3)
    return (yp.reshape(B, S, D_MODEL), ys.reshape(DB, L, D_MODEL), lru_conv_p, lru_conv_s, lru_h_p, lru_h_s,
            k_p, k_s, v_p, v_s, ssd_conv_p, ssd_conv_s, ssd_h_p, ssd_h_s)
```

```python
import functools

import numpy as np
import jax
import jax.numpy as jnp
from jax import lax
from jax.experimental import pallas as pl
from jax.experimental.pallas import tpu as pltpu

F32 = jnp.float32
BF16 = jnp.bfloat16

D_MODEL = 1024
CONV_W = 4
EPS = 1e-6
D_LRU = 512
LRU_BLOCKS = 8
LRU_C = 8.0
HEAD_DIM = 64
D_ATTN = 512
N_HEADS = 8
DILATIONS = ((128, 1), (512, 4), (2048, 16))
MAX_WINDOW = 2048
ATTN_BLOCK = 128
D_SSD = 1024
SSD_P = 64
SSD_HEADS = 16
SSD_GROUPS = 2
SSD_N = 128
SSD_CHUNK = 128
D_XBC = D_SSD + 2 * SSD_GROUPS * SSD_N
GROUP_W = D_SSD // SSD_GROUPS

OFF_XLRU, OFF_GLRU, OFF_Q, OFF_K, OFF_V, OFF_GATTN, OFF_Z, OFF_XBC, OFF_DT = (
    0, 512, 1024, 1536, 2048, 2560, 3072, 4096, 5632)
D_IN = 5648
LANES = 128
D_IN_PAD = OFF_DT + LANES

SPAN = 2048
NRES = SPAN // ATTN_BLOCK
SSD_PROMPT_CHUNKS = 2
SSD_SAMPLE_SEQS = 8
MASKED = 1e33
NEG = -1e30

VMEM_LIMIT = 56 * 1024 * 1024


def _cparams(sem):
    return pltpu.CompilerParams(dimension_semantics=sem, vmem_limit_bytes=VMEM_LIMIT)


def _silu(x):
    return x * jax.nn.sigmoid(x)


def _softplus(x):
    return jnp.maximum(x, 0.0) + jnp.log1p(jnp.exp(-jnp.abs(x)))


def _split3(x):
    h1 = x.astype(BF16)
    r1 = x - h1.astype(F32)
    h2 = r1.astype(BF16)
    h3 = (r1 - h2.astype(F32)).astype(BF16)
    return h1, h2, h3


def _dot3_right(x, m3_bf16):
    return jnp.dot(jnp.concatenate(_split3(x), axis=1), m3_bf16, preferred_element_type=F32)


def _dot3_left(m_bf16, x):
    return jnp.dot(jnp.concatenate([m_bf16] * 3, axis=1), jnp.concatenate(_split3(x), axis=0),
                   preferred_element_type=F32)


def _dot_nt(a, b):
    return lax.dot_general(a, b, (((1,), (1,)), ((), ())), preferred_element_type=F32)


def _dot_tn(a, b):
    return lax.dot_general(a, b, (((0,), (0,)), ((), ())), preferred_element_type=F32)


def _proj_kernel(x_ref, g_ref, w_ref, wdt_ref, *rest, first_win_tile, n_alias, cache_li, prompt):
    if prompt:
        lcw_ref, lcb_ref, scw_ref, scb_ref = rest[:4]
        rest = rest[4 + n_alias:]
        kt_ref, vt_ref, tail_lru_ref, tail_xbc_ref, xe_lru, xe_xbc = rest[9:]
    xlru_ref, glru_ref, q_ref, k_ref, v_ref, gattn_ref, z_ref, xbc_ref, dt_ref = rest[:9]
    n = x_ref.shape[0]
    x = x_ref[...]
    h = x * lax.rsqrt(jnp.mean(x * x, axis=-1, keepdims=True) + EPS) * g_ref[...]
    hb = h.astype(BF16)

    def mm(off, width):
        return jnp.dot(hb, w_ref[:, off:off + width], preferred_element_type=F32)

    def conv(raw, xe_ref, cw_ref, cb_ref, tail_ref, cols):
        ext = jnp.concatenate([xe_ref[:, cols], raw], axis=0)
        out = cb_ref[:, cols] + raw * cw_ref[CONV_W - 1:CONV_W, cols]
        for back in range(1, CONV_W):
            out = out + pltpu.roll(ext, back, axis=0)[8:, :] * cw_ref[CONV_W - 1 - back:CONV_W - back, cols]
        last = raw[n - 8:n, :]
        xe_ref[:, cols] = last
        tail_ref[:, cols] = last
        return out

    def pairs(dst, val):
        for c in range(D_ATTN // LANES):
            dst[c] = val[:, c * LANES:(c + 1) * LANES]

    dt_of = lambda: jnp.dot(hb, wdt_ref[...], preferred_element_type=F32)
    if prompt:
        @pl.when(pl.program_id(1) == 0)
        def _():
            xe_lru[0:8, :] = jnp.zeros((8, D_LRU), F32)
            xe_xbc[0:8, :] = jnp.zeros((8, D_XBC), F32)

        pw = 512
        cols = [slice(j * pw, (j + 1) * pw) for j in range(D_XBC // pw)]
        xbc_conv = lambda raw, c: _silu(conv(raw, xe_xbc, scw_ref, scb_ref, tail_xbc_ref, c))
        raw = mm(OFF_XBC, pw)
        pairs(q_ref, mm(OFF_Q, D_ATTN) * (HEAD_DIM ** -0.5))
        xbc_ref[:, cols[0]] = xbc_conv(raw, cols[0])
        raw = mm(OFF_XBC + pw, pw)
        k = mm(OFF_K, D_ATTN)
        xbc_ref[:, cols[1]] = xbc_conv(raw, cols[1])
        raw = mm(OFF_XBC + 2 * pw, pw)
        v = mm(OFF_V, D_ATTN)
        xbc_ref[:, cols[2]] = xbc_conv(raw, cols[2])
        pairs(k_ref, k)
        pairs(v_ref, v)
        raw = mm(OFF_XLRU, D_LRU)
        gattn_ref[...] = mm(OFF_GATTN, D_ATTN)
        xlru_ref[...] = conv(raw, xe_lru, lcw_ref, lcb_ref, tail_lru_ref, slice(None))
        z0 = mm(OFF_Z, pw)
        g_lru = mm(OFF_GLRU, D_LRU)
        z_ref[:, :pw] = _silu(z0)
        z1 = mm(OFF_Z + pw, D_SSD - pw)
        dt_ref[...] = dt_of()
        z_ref[:, pw:] = _silu(z1)
        glru_ref[...] = _silu(g_lru)
    else:
        xbc_ref[...] = mm(OFF_XBC, D_XBC)
        xlru_ref[...] = mm(OFF_XLRU, D_LRU)
        z_ref[...] = mm(OFF_Z, D_SSD)
        glru_ref[...] = mm(OFF_GLRU, D_LRU)
        pairs(q_ref, mm(OFF_Q, D_ATTN) * (HEAD_DIM ** -0.5))
        k = mm(OFF_K, D_ATTN)
        v = mm(OFF_V, D_ATTN)
        pairs(k_ref, k)
        pairs(v_ref, v)
        gattn_ref[...] = mm(OFF_GATTN, D_ATTN)
        dt_ref[...] = dt_of()
    if prompt:
        @pl.when(pl.program_id(1) >= first_win_tile)
        def _():
            for dst, val in ((kt_ref, k.T), (vt_ref, v.T)):
                if dst.ndim == 2:
                    dst[...] = val
                else:
                    for l in range(dst.shape[0]):
                        dst[l] = val if l == cache_li else jnp.zeros_like(val)


def _proj(x3, g, w_bf16, wdt_bf16, tm, cache=None):
    B, S, _ = x3.shape
    row = lambda width: pl.BlockSpec((None, tm, width), lambda b, t: (b, t, 0))
    const = lambda r, c: pl.BlockSpec((r, c), lambda b, t: (0, 0))
    pair = pl.BlockSpec((D_ATTN // LANES, None, tm, LANES), lambda b, t: (0, b, t, 0))
    pair_shape = (4, B, S, LANES)
    sds = lambda *s: jax.ShapeDtypeStruct(s, F32)
    weight = pl.BlockSpec((D_MODEL, D_IN), lambda b, t: (0, 0), pipeline_mode=pl.Buffered(1))
    in_specs = [row(D_MODEL), const(1, D_MODEL), weight, const(D_MODEL, LANES)]
    args = [x3, g, w_bf16, wdt_bf16]
    out_specs = [row(D_LRU), row(D_LRU), pair, pair, pair, row(D_ATTN), row(D_SSD), row(D_XBC), row(LANES)]
    out_shape = [sds(B, S, D_LRU), sds(B, S, D_LRU), sds(*pair_shape), sds(*pair_shape), sds(*pair_shape),
                 sds(B, S, D_ATTN), sds(B, S, D_SSD), sds(B, S, D_XBC), sds(B, S, LANES)]
    first_win_tile, n_alias, aliases, cache_li, scratch = 0, 0, {}, 0, []
    if cache is not None:
        li, depth, win, prev_kt, prev_vt, lcw, lcb, scw, scb = cache
        in_specs += [const(CONV_W, D_LRU), const(1, D_LRU), const(CONV_W, D_XBC), const(1, D_XBC)]
        args += [lcw, lcb, scw, scb]
        first_win_tile = (S - win) // tm
        cache_li = li
        win_tile = lambda t: jnp.maximum(t - first_win_tile, 0)
        if prev_kt is None:
            cspec = pl.BlockSpec((depth, None, D_ATTN, tm), lambda b, t: (0, b, 0, win_tile(t)))
        else:
            cspec = pl.BlockSpec((None, None, D_ATTN, tm), lambda b, t: (li, b, 0, win_tile(t)))
        tail = lambda width: pl.BlockSpec((None, 8, width), lambda b, t: (b, 0, 0))
        out_specs += [cspec, cspec, tail(D_LRU), tail(D_XBC)]
        out_shape += [sds(depth, B, D_ATTN, win), sds(depth, B, D_ATTN, win), sds(B, 8, D_LRU), sds(B, 8, D_XBC)]
        scratch = [pltpu.VMEM((8, D_LRU), F32), pltpu.VMEM((8, D_XBC), F32)]
        if prev_kt is not None:
            n_alias = 2
            in_specs += [pl.BlockSpec(memory_space=pl.ANY)] * 2
            args += [prev_kt, prev_vt]
            aliases = {len(args) - 2: 9, len(args) - 1: 10}
    return pl.pallas_call(
        functools.partial(_proj_kernel, first_win_tile=first_win_tile, n_alias=n_alias, cache_li=cache_li,
                          prompt=cache is not None),
        grid=(B, S // tm),
        in_specs=in_specs,
        out_specs=out_specs,
        out_shape=out_shape,
        scratch_shapes=scratch,
        input_output_aliases=aliases,
        compiler_params=_cparams(("arbitrary", "arbitrary")),
        name="in_proj",
    )(*args)


def _lru_gates(xc, wg_ref, bg_ref, lam_ref):
    pre = jnp.dot(xc.astype(BF16), wg_ref[...], preferred_element_type=F32) + bg_ref[...]
    r = jax.nn.sigmoid(pre[:, :D_LRU])
    i = jax.nn.sigmoid(pre[:, D_LRU:])
    log_a = (-LRU_C) * r * _softplus(-lam_ref[...])
    a = jnp.exp(log_a)
    t = jnp.tanh(log_a)
    return a, jnp.sqrt(-2.0 * t / (1.0 - t)) * (i * xc)


def _scan_rows(a, b, h0):
    n, width = a.shape
    a = a.reshape(n // 8, 8, width)
    b = b.reshape(n // 8, 8, width)
    row = lax.broadcasted_iota(jnp.int32, a.shape, 1)
    k = 1
    while k < 8:
        keep = row >= k
        a_sh = jnp.where(keep, pltpu.roll(a, k, axis=1), 1.0)
        b_sh = jnp.where(keep, pltpu.roll(b, k, axis=1), 0.0)
        b = a * b_sh + b
        a = a * a_sh
        k *= 2
    blocks = []
    h = h0
    for j in range(n // 8):
        blk = b[j] + a[j] * h
        blocks.append(blk)
        h = blk[7:8, :]
    return jnp.concatenate(blocks, axis=0), h


def _conv_rows(xe_ref, n, w_ref, b_ref, cols=slice(None)):
    y = b_ref[:, cols] + xe_ref[8:8 + n, cols] * w_ref[3:4, cols]
    for tap in range(CONV_W - 1):
        y = y + xe_ref[5 + tap:5 + tap + n, cols] * w_ref[tap:tap + 1, cols]
    return y


def _lru_out_kernel(xc_ref, gate_ref, wg_ref, bg_ref, lam_ref, x_ref, o_ref, gattn_ref, yssd_ref, w_ref, g_ref,
                    out_ref, hl_ref, h_ref):
    @pl.when(pl.program_id(1) == 0)
    def _():
        h_ref[...] = jnp.zeros_like(h_ref)

    a, b = _lru_gates(xc_ref[...], wg_ref, bg_ref, lam_ref)
    hs, h_last = _scan_rows(a, b, h_ref[0:1, :])
    h_ref[...] = jnp.broadcast_to(h_last, h_ref.shape)
    hl_ref[...] = h_last
    _out_tile(x_ref, hs * gate_ref[...], o_ref, gattn_ref, yssd_ref, w_ref, g_ref, out_ref)


def _lru_out_prompt(xc, gate, wg, bg, lam, x3, o4, g_attn, y_ssd, w_bf16, g, tt):
    B, S, _ = xc.shape
    tile = lambda w: pl.BlockSpec((None, tt, w), lambda b, t: (b, t, 0))
    const = lambda r, c: pl.BlockSpec((r, c), lambda b, t: (0, 0))
    return pl.pallas_call(
        _lru_out_kernel,
        grid=(B, S // tt),
        in_specs=[tile(D_LRU), tile(D_LRU), const(D_LRU, 2 * D_LRU), const(1, 2 * D_LRU), const(1, D_LRU),
                  tile(D_MODEL), pl.BlockSpec((D_ATTN // LANES, None, tt, LANES), lambda b, t: (0, b, t, 0)),
                  tile(D_ATTN), tile(D_SSD), const(D_LRU + D_ATTN + D_SSD, D_MODEL), const(1, D_MODEL)],
        out_specs=[tile(D_MODEL), pl.BlockSpec((None, 1, D_LRU), lambda b, t: (b, 0, 0))],
        out_shape=[jax.ShapeDtypeStruct((B, S, D_MODEL), F32), jax.ShapeDtypeStruct((B, 1, D_LRU), F32)],
        scratch_shapes=[pltpu.VMEM((8, D_LRU), F32)],
        compiler_params=_cparams(("arbitrary", "arbitrary")),
        name="lru_out_prompt",
    )(xc, gate, wg, bg, lam, x3, o4, g_attn, y_ssd, w_bf16, g)


def _lru_s_kernel(x_ref, g_ref, cs_ref, h0_ref, cw_ref, cb_ref, wg_ref, bg_ref, lam_ref, y_ref, hl_ref):
    L = x_ref.shape[0]
    db = x_ref.shape[1]
    xs = [cs_ref[i] for i in range(CONV_W - 1)] + [x_ref[i] for i in range(L)]
    xc = []
    for t in range(L):
        y = cb_ref[...] + xs[t] * cw_ref[0:1, :]
        for tap in range(1, CONV_W):
            y = y + xs[t + tap] * cw_ref[tap:tap + 1, :]
        xc.append(y)
    a, b = _lru_gates(jnp.concatenate(xc, axis=0), wg_ref, bg_ref, lam_ref)
    h = h0_ref[...]
    for t in range(L):
        h = a[t * db:(t + 1) * db] * h + b[t * db:(t + 1) * db]
        y_ref[t] = h * _silu(g_ref[t])
    hl_ref[...] = h


def _lru_sample(x_tm, g_tm, cs_tm, h0, cw, cb, wg, bg, lam):
    L, DB, _ = x_tm.shape
    return pl.pallas_call(
        _lru_s_kernel,
        out_shape=[jax.ShapeDtypeStruct((L, DB, D_LRU), F32), jax.ShapeDtypeStruct((DB, D_LRU), F32)],
        compiler_params=pltpu.CompilerParams(vmem_limit_bytes=VMEM_LIMIT),
        name="lru_sample",
    )(x_tm, g_tm, cs_tm, h0, cw, cb, wg, bg, lam)


def _block_order(rho, dil):
    if dil == 16:
        return rho
    if dil == 4:
        return 4 * (rho & 31) + (rho >> 5)
    return 16 * (rho & 7) + (rho >> 3)


def _chunks(dil):
    return {16: (1, 128), 4: (4, 32), 1: (16, 8)}[dil]


def _attn_p_kernel(q_ref, k_ref, v_ref, o_ref, qs, ks, vs, r_st, m_st, bias_ref, *, unroll):
    c = pl.program_id(1)
    s = pl.program_id(2)
    nres = SPAN // ATTN_BLOCK
    lane_q = lax.broadcasted_iota(jnp.int32, (ATTN_BLOCK, LANES), 1)
    first_half = lane_q < HEAD_DIM

    cur = s & 1
    prev = 1 - cur

    @pl.when(s == 0)
    def _():
        ks[prev] = jnp.zeros((SPAN, LANES), F32)
        vs[0, prev] = jnp.zeros((SPAN, LANES), F32)
        vs[1, prev] = jnp.zeros((SPAN, LANES), F32)

    pr = lax.broadcasted_iota(jnp.int32, (ATTN_BLOCK, ATTN_BLOCK), 0)
    pc = lax.broadcasted_iota(jnp.int32, (ATTN_BLOCK, ATTN_BLOCK), 1)
    perm = jnp.where(pc == nres * (pr & 7) + (pr >> 3), 1.0, 0.0).astype(BF16)
    for g in range(SPAN // ATTN_BLOCK):
        src = slice(g * ATTN_BLOCK, (g + 1) * ATTN_BLOCK)
        move = lambda ref: jnp.dot(perm, ref[src, :].astype(BF16), preferred_element_type=F32)
        q, k, v = move(q_ref), move(k_ref), move(v_ref)
        pieces = ((qs, (0,), jnp.where(first_half, q, 0.0)), (qs, (1,), jnp.where(first_half, 0.0, q)),
                  (ks, (cur,), k),
                  (vs, (0, cur), jnp.where(first_half, v, 1.0)), (vs, (1, cur), jnp.where(first_half, 1.0, v)))
        for r in range(nres):
            for ref, lead, val in pieces:
                ref[lead + (slice(r * ATTN_BLOCK + 8 * g, r * ATTN_BLOCK + 8 * g + 8), slice(None))] = (
                    val[8 * r:8 * r + 8, :])

    span0 = s == 0

    @pl.when(span0)
    def _():
        iq = lax.broadcasted_iota(jnp.int32, (ATTN_BLOCK, 2 * ATTN_BLOCK), 0)
        ik = lax.broadcasted_iota(jnp.int32, (ATTN_BLOCK, 2 * ATTN_BLOCK), 1)
        in_cur = ik >= ATTN_BLOCK
        for di, (window, dil) in enumerate(DILATIONS):
            j = ATTN_BLOCK + _block_order(iq, dil) - (
                jnp.where(in_cur, ATTN_BLOCK, 0) + _block_order(ik & (ATTN_BLOCK - 1), dil))
            valid = (j >= 0) & (j <= window // dil)
            dist = (j * dil).astype(F32)
            dist_all = jnp.where(valid, dist, MASKED)
            dist_cur = jnp.where(valid & in_cur, dist, MASKED)
            for hh in range(2):
                slope = jnp.exp2(-(jnp.zeros((1, 1), F32) + (2 * c + hh + 1).astype(F32)))
                bias_ref[di, hh, 0] = -slope * dist_all
                bias_ref[di, hh, 1] = -slope * dist_cur

    def block(di, dil, it):
        nchunk, crow = _chunks(dil)

        def gather(ref, lead, starts):
            parts = [ref[lead + (pl.ds(st, crow), slice(None))] for st in starts]
            return parts[0] if len(parts) == 1 else jnp.concatenate(parts, axis=0)

        if dil == 16:
            base, cb, last_off = it * ATTN_BLOCK, 0, 0
        elif dil == 4:
            base, cb, last_off = (it >> 2) * ATTN_BLOCK, it & 3, 3 * crow
        else:
            base, cb, last_off = 0, it, 15 * crow
        stride = {16: 0, 4: 4 * ATTN_BLOCK, 1: ATTN_BLOCK}[dil]
        cur_starts = [pl.multiple_of(base + a * stride + crow * cb, 8) for a in range(nchunk)]
        if dil == 16:
            prev_slot = prev
            prev_starts = cur_starts
            first_in_span = True
        else:
            has_prev = cb > 0
            prev_slot = jnp.where(has_prev, cur, prev)
            prev_off = jnp.where(has_prev, crow * (cb - 1), last_off)
            prev_starts = [pl.multiple_of(base + a * stride + prev_off, 8) for a in range(nchunk)]
            first_in_span = jnp.logical_not(has_prev)
        kb = jnp.concatenate([gather(ks, (prev_slot,), prev_starts), gather(ks, (cur,), cur_starts)],
                             axis=0).astype(BF16)
        variant = jnp.where(jnp.logical_and(span0, first_in_span), 1, 0)
        for hh in range(2):
            qh = gather(qs, (hh,), cur_starts).astype(BF16)
            vh = jnp.concatenate([gather(vs, (hh, prev_slot), prev_starts), gather(vs, (hh, cur), cur_starts)],
                                 axis=0).astype(BF16)
            sc = _dot_nt(qh, kb) + bias_ref[di, hh, variant]
            m = jnp.max(sc, axis=1, keepdims=True)
            p = jnp.exp(sc - m)
            r_new = jnp.dot(p.astype(BF16), vh, preferred_element_type=F32)
            m_new = jnp.broadcast_to(m, (ATTN_BLOCK, LANES))
            for a, st in enumerate(cur_starts):
                r_st[di, hh, pl.ds(st, crow), :] = r_new[a * crow:(a + 1) * crow, :]
                m_st[di, hh, pl.ds(st, crow), :] = m_new[a * crow:(a + 1) * crow, :]

    def body(it, carry):
        for di, (_, dil) in enumerate(DILATIONS):
            block(di, dil, it)
        return carry

    lax.fori_loop(0, nres, body, 0, unroll=unroll)

    ndil = len(DILATIONS)
    for r in range(nres):
        src = slice(r * ATTN_BLOCK, (r + 1) * ATTN_BLOCK)
        outs = []
        for hh in range(2):
            ms = [m_st[di, hh, src, :] for di in range(ndil)]
            m_tot = functools.reduce(jnp.maximum, ms)
            tot = sum(r_st[di, hh, src, :] * jnp.exp(ms[di] - m_tot) for di in range(ndil))
            outs.append(tot / pltpu.roll(tot, HEAD_DIM, axis=1))
        o_ref[pl.ds(r, ATTN_BLOCK, stride=nres), :] = jnp.where(first_half, outs[0], outs[1])


def _attn_prompt(q4, k4, v4, unroll=8):
    npair, B, S, _ = q4.shape
    ndil = len(DILATIONS)
    spec = pl.BlockSpec((None, None, SPAN, LANES), lambda b, c, s: (c, b, s, 0))
    return pl.pallas_call(
        functools.partial(_attn_p_kernel, unroll=unroll),
        grid=(B, npair, S // SPAN),
        in_specs=[spec, spec, spec],
        out_specs=spec,
        out_shape=jax.ShapeDtypeStruct(q4.shape, F32),
        scratch_shapes=[pltpu.VMEM((2, SPAN, LANES), F32), pltpu.VMEM((2, SPAN, LANES), F32),
                        pltpu.VMEM((2, 2, SPAN, LANES), F32),
                        pltpu.VMEM((ndil, 2, SPAN, LANES), F32), pltpu.VMEM((ndil, 2, SPAN, LANES), F32),
                        pltpu.VMEM((ndil, 2, 2, ATTN_BLOCK, 2 * ATTN_BLOCK), F32)],
        compiler_params=_cparams(("arbitrary", "arbitrary", "arbitrary")),
        name="attn_prompt",
    )(q4, k4, v4)


def _attn_s_kernel(*refs, n_new):
    _attn_s_body(*refs, n_new=n_new, first=pl.program_id(0) == 0)


def _attn_s_body(q_ref, kn_ref, vn_ref, kc_ref, vc_ref, o_ref, dist_ref, mult_ref, *, n_new, first):
    nseq, nheads, hd, W = kc_ref.shape
    rows = q_ref.shape[1]
    G = 4
    gr, gl = G * rows, G * hd
    shift = n_new.bit_length() - 1

    @pl.when(first)
    def _():
        i = lax.broadcasted_iota(jnp.int32, (gr, W), 0) & (n_new - 1)
        pos = lax.broadcasted_iota(jnp.int32, (gr, W), 1)
        d = W + i - pos
        cnt = jnp.zeros((gr, W), F32)
        for window, dil in DILATIONS:
            cnt = cnt + jnp.where(((d & (dil - 1)) == 0) & (d <= window), 1.0, 0.0)
        dist_ref[...] = d.astype(F32)
        mult_ref[...] = cnt

    r = lax.broadcasted_iota(jnp.int32, (gr, rows), 0) & (rows - 1)
    r2 = lax.broadcasted_iota(jnp.int32, (gr, rows), 1)
    dn = (r & (n_new - 1)) - (r2 & (n_new - 1))
    same_seq = (r >> shift) == (r2 >> shift)
    mult_n = jnp.zeros((gr, rows), F32)
    for window, dil in DILATIONS:
        ok = ((dn & (dil - 1)) == 0) & (dn >= 0) & (dn <= window) & same_seq
        mult_n = mult_n + jnp.where(ok, 1.0, 0.0)
    dist_n = dn.astype(F32)
    dist_c = dist_ref[...]
    mult_c = mult_ref[...]
    row_head = lax.broadcasted_iota(jnp.int32, (gr, gl), 0) >> 3
    hd_shift = hd.bit_length() - 1
    lane_head = lax.broadcasted_iota(jnp.int32, (gr, gl), 1) >> hd_shift
    diag = row_head == lane_head
    out_lane_head = lax.broadcasted_iota(jnp.int32, (rows, gl), 1) >> hd_shift
    out_row_seq = lax.broadcasted_iota(jnp.int32, (rows, gl), 0) >> shift
    head_col = lax.broadcasted_iota(jnp.int32, (gr, 1), 0) >> 3
    npair_g = gl // LANES
    for g in range(nheads // G):
        pairs = range(g * npair_g, (g + 1) * npair_g)
        cat = lambda ref: jnp.concatenate([ref[c] for c in pairs], axis=1)
        q_bd = jnp.where(diag, jnp.concatenate([cat(q_ref)] * G, axis=0), 0.0).astype(BF16)
        slope = jnp.exp2(-(head_col + (g * G + 1)).astype(F32))
        sn = _dot_nt(q_bd, cat(kn_ref).astype(BF16))
        sn = jnp.where(mult_n > 0, sn - slope * dist_n, NEG)
        mn = jnp.max(sn, axis=1, keepdims=True)
        vn = cat(vn_ref).astype(BF16)
        o_g = jnp.zeros((rows, gl), F32)
        for b in range(nseq):
            kst = kc_ref[b, g * G:(g + 1) * G].reshape(gl, W).astype(BF16)
            vst = vc_ref[b, g * G:(g + 1) * G].reshape(gl, W).astype(BF16)
            sc = jnp.dot(q_bd, kst, preferred_element_type=F32)
            sc = jnp.where(mult_c > 0, sc - slope * dist_c, NEG)
            m = jnp.maximum(jnp.max(sc, axis=1, keepdims=True), mn)
            pc = mult_c * jnp.exp(sc - m)
            pn = mult_n * jnp.exp(sn - m)
            l = jnp.sum(pc, axis=1, keepdims=True) + jnp.sum(pn, axis=1, keepdims=True)
            o = (_dot_nt(pc.astype(BF16), vst) + jnp.dot(pn.astype(BF16), vn, preferred_element_type=F32)) / l
            sel = sum(jnp.where(out_lane_head == h, o[h * rows:(h + 1) * rows, :], 0.0) for h in range(G))
            o_g = jnp.where(out_row_seq == b, sel, o_g)
        for j, c in enumerate(pairs):
            o_ref[c] = o_g[:, j * LANES:(j + 1) * LANES]


def _attn_sample(q4, k4, v4, kc_t, vc_t, li, n_new):
    small, cache, tables = _attn_s_specs(q4, kc_t, li, n_new, lambda b: b)
    return pl.pallas_call(
        functools.partial(_attn_s_kernel, n_new=n_new),
        grid=(q4.shape[1] // 8,),
        in_specs=[small, small, small, cache, cache],
        out_specs=small,
        out_shape=jax.ShapeDtypeStruct(q4.shape, F32),
        scratch_shapes=tables,
        compiler_params=_cparams(("arbitrary",)),
        name="attn_sample",
    )(q4, k4, v4, kc_t, vc_t)


def _attn_s_specs(q4, kc_t, li, n_new, group_of):
    npair, T, _ = q4.shape
    W = kc_t.shape[-1]
    rows = 8
    assert n_new & (n_new - 1) == 0 and rows % n_new == 0 and T % rows == 0
    small = pl.BlockSpec((npair, rows, LANES), lambda *g: (0, group_of(*g), 0))
    cache = pl.BlockSpec((None, rows // n_new, N_HEADS, HEAD_DIM, W), lambda *g: (li, group_of(*g), 0, 0, 0))
    return small, cache, [pltpu.VMEM((4 * rows, W), F32), pltpu.VMEM((4 * rows, W), F32)]


def _group_norm_out(y, z, ng_ref, gate_ready=False):
    yg = y * (z if gate_ready else _silu(z))
    outs = []
    for g in range(SSD_GROUPS):
        part = yg[:, g * GROUP_W:(g + 1) * GROUP_W]
        outs.append(part * lax.rsqrt(jnp.mean(part * part, axis=-1, keepdims=True) + EPS))
    return jnp.concatenate(outs, axis=1) * ng_ref[...]


def _ssd_p_kernel(*refs, n_new):
    fused = n_new is not None
    n_in = 13 if fused else 8
    y_ref, hl_ref = refs[n_in:n_in + 2]
    scratch = refs[n_in + (3 if fused else 2):]
    _ssd_p_tile(*refs[:8], y_ref, hl_ref, scratch[0])
    if fused:
        first = jnp.logical_and(pl.program_id(0) == 0, pl.program_id(1) == 0)
        _attn_s_body(*refs[8:13], refs[n_in + 2], scratch[1], scratch[2], n_new=n_new, first=first)


def _ssd_p_tile(xc_ref, dt_ref, gate_ref, dtb_ref, alog_ref, dx_ref, ng_ref, e_ref, y_ref, hl_ref, ht_ref):
    t = pl.program_id(1)
    Q = SSD_CHUNK
    n = xc_ref.shape[0]

    @pl.when(t == 0)
    def _():
        ht_ref[...] = jnp.zeros_like(ht_ref)

    xc_all = xc_ref[...]
    dt_all = _softplus(dt_ref[...] + dtb_ref[...])
    a_neg = -jnp.exp(alog_ref[...])
    ri = lax.broadcasted_iota(jnp.int32, (Q, Q), 0)
    ci = lax.broadcasted_iota(jnp.int32, (Q, Q), 1)
    causal = ri >= ci
    tri = jnp.where(causal, 1.0, 0.0).astype(BF16)
    lane = lax.broadcasted_iota(jnp.int32, (Q, LANES), 1)
    for ck in range(n // Q):
        rows = slice(ck * Q, (ck + 1) * Q)
        y = _ssd_chunk(xc_all[rows, :], dt_all[rows, :], a_neg, causal, tri, lane, dx_ref, e_ref, ht_ref)
        y_ref[rows, :] = _group_norm_out(y, gate_ref[rows, :], ng_ref, gate_ready=True).astype(y_ref.dtype)

    @pl.when(t == pl.num_programs(1) - 1)
    def _():
        for g in range(SSD_GROUPS):
            hl_ref[g * (SSD_HEADS // SSD_GROUPS):(g + 1) * (SSD_HEADS // SSD_GROUPS)] = (
                ht_ref[g].T.reshape(SSD_HEADS // SSD_GROUPS, SSD_P, SSD_N))


def _ssd_chunk(xc, dt, a_neg, causal, tri, lane, dx_ref, e_ref, ht_ref):
    Q = SSD_CHUNK
    xs = xc[:, :D_SSD]
    bm = xc[:, D_SSD:D_SSD + SSD_GROUPS * SSD_N]
    cm = xc[:, D_SSD + SSD_GROUPS * SSD_N:]
    cum = _dot3_left(tri, dt * a_neg)
    cum_t = cum.T
    both_x = _dot3_right(jnp.concatenate([cum, dt], axis=0), e_ref[...])
    cum_x = both_x[:Q]
    dt_x = both_x[Q:]
    xdt = xs * dt_x
    ecum = jnp.exp(cum_x)
    cl = cum_x[Q - 1:Q, :]
    w_end = xdt * jnp.exp(cl - cum_x)
    dec = jnp.exp(cl)

    y_parts = []
    for g in range(SSD_GROUPS):
        cg = cm[:, g * SSD_N:(g + 1) * SSD_N].astype(BF16)
        bg = bm[:, g * SSD_N:(g + 1) * SSD_N]
        cb = _dot_nt(cg, bg.astype(BF16))
        gl = slice(g * GROUP_W, (g + 1) * GROUP_W)
        h_old = ht_ref[g]
        y_off = jnp.dot(cg, h_old.astype(BF16), preferred_element_type=F32) * ecum[:, gl]
        ht_ref[g] = h_old * dec[:, gl] + jnp.dot(bg.T.astype(BF16), w_end[:, gl].astype(BF16),
                                                  preferred_element_type=F32)
        for pp in range(GROUP_W // LANES):
            pl_ = slice(g * GROUP_W + pp * LANES, g * GROUP_W + (pp + 1) * LANES)
            xdt_pair = xdt[:, pl_].astype(BF16)
            halves = []
            for hh in range(2):
                h = (g * GROUP_W + pp * LANES) // SSD_P + hh
                seg = jnp.broadcast_to(cum[:, h:h + 1], (Q, Q)) - jnp.broadcast_to(cum_t[h:h + 1, :], (Q, Q))
                scores = cb * jnp.exp(jnp.where(causal, seg, NEG))
                halves.append(jnp.dot(scores.astype(BF16), xdt_pair, preferred_element_type=F32))
            y_diag = jnp.where(lane < SSD_P, halves[0], halves[1])
            y_parts.append(y_diag + y_off[:, pp * LANES:(pp + 1) * LANES])
    return jnp.concatenate(y_parts, axis=1) + dx_ref[...] * xs


def _ssd_prompt(xc, dt, gate, dtb, alog, dx, ng, e, sample_attn=None):
    B, S, _ = xc.shape
    Q = SSD_PROMPT_CHUNKS * SSD_CHUNK
    nt = S // Q
    tile = lambda w: pl.BlockSpec((None, Q, w), lambda b, t: (b, t, 0))
    const = lambda r, c: pl.BlockSpec((r, c), lambda b, t: (0, 0))
    in_specs = [tile(D_XBC), tile(LANES), tile(D_SSD), const(1, LANES), const(1, LANES), const(1, D_SSD),
                const(1, D_SSD), const(3 * LANES, D_SSD)]
    args = [xc, dt, gate, dtb, alog, dx, ng, e]
    out_specs = [tile(D_SSD), pl.BlockSpec((None, SSD_HEADS, SSD_P, SSD_N), lambda b, t: (b, 0, 0, 0))]
    out_shape = [jax.ShapeDtypeStruct((B, S, D_SSD), BF16), jax.ShapeDtypeStruct((B, SSD_HEADS, SSD_P, SSD_N), F32)]
    scratch = [pltpu.VMEM((SSD_GROUPS, SSD_N, GROUP_W), F32)]
    n_new = None
    if sample_attn is not None:
        q4, k4, v4, kc_t, vc_t, li, n_new = sample_attn
        small, cache, tables = _attn_s_specs(q4, kc_t, li, n_new, lambda b, t: b * nt + t)
        assert q4.shape[1] // 8 == B * nt
        in_specs += [small, small, small, cache, cache]
        args += [q4, k4, v4, kc_t, vc_t]
        out_specs.append(small)
        out_shape.append(jax.ShapeDtypeStruct(q4.shape, F32))
        scratch += tables
    return pl.pallas_call(
        functools.partial(_ssd_p_kernel, n_new=n_new),
        grid=(B, nt),
        in_specs=in_specs,
        out_specs=out_specs,
        out_shape=out_shape,
        scratch_shapes=scratch,
        compiler_params=_cparams(("arbitrary", "arbitrary")),
        name="ssd_prompt" if sample_attn is None else "ssd_prompt_attn_sample",
    )(*args)


def _ssd_s_kernel(xbc_ref, cs_ref, dt_ref, z_ref, h_ref, cw_ref, cb_ref, dtb_ref, alog_ref, dx_ref, ng_ref, e_ref,
                  *rest, n_new, li):
    y_ref, hn_ref = rest[-2:]
    if hn_ref.ndim == 5:
        for l in range(hn_ref.shape[0]):
            if l != li:
                hn_ref[l] = jnp.zeros(hn_ref.shape[1:], F32)
        hn_ref = hn_ref.at[li]
    for i in range(xbc_ref.shape[0]):
        _ssd_s_one(xbc_ref.at[i], cs_ref.at[i], dt_ref.at[i], z_ref.at[i], h_ref.at[i], cw_ref, cb_ref, dtb_ref,
                   alog_ref, dx_ref, ng_ref, e_ref, y_ref.at[i], hn_ref.at[i], n_new)


def _ssd_s_one(xbc_ref, cs_ref, dt_ref, z_ref, h_ref, cw_ref, cb_ref, dtb_ref, alog_ref, dx_ref, ng_ref, e_ref,
               y_ref, hn_ref, n_new):
    R = 8
    row = lax.broadcasted_iota(jnp.int32, (R, LANES), 0)
    real = row < n_new
    xe = jnp.concatenate([cs_ref[...], xbc_ref[...]], axis=0)
    y = cb_ref[...] + xe[8:16, :] * cw_ref[3:4, :]
    for tap in range(CONV_W - 1):
        y = y + xe[5 + tap:13 + tap, :] * cw_ref[tap:tap + 1, :]
    xc = _silu(y)
    xs = xc[:, :D_SSD]
    bm = xc[:, D_SSD:D_SSD + SSD_GROUPS * SSD_N]
    cm = xc[:, D_SSD + SSD_GROUPS * SSD_N:]

    dt = jnp.where(real, _softplus(dt_ref[...] + dtb_ref[...]), 0.0)
    da = dt * (-jnp.exp(alog_ref[...]))
    cum = da
    k = 1
    while k < R:
        cum = cum + jnp.where(row >= k, pltpu.roll(cum, k, axis=0), 0.0)
        k *= 2
    e = e_ref[...]
    cum_x = _dot3_right(cum, e)
    dt_x = _dot3_right(dt, e)
    xdt = xs * dt_x
    ecum = jnp.exp(cum_x)
    cl = cum_x[R - 1:R, :]
    w_end = xdt * jnp.exp(cl - cum_x)
    dec3 = _split3(jnp.exp(cl))
    ones = jnp.ones((R, SSD_N), BF16)
    zrow = jnp.zeros((R - 3, GROUP_W), BF16)

    y_parts = []
    for g in range(SSD_GROUPS):
        gl = slice(g * GROUP_W, (g + 1) * GROUP_W)
        cg = cm[:, g * SSD_N:(g + 1) * SSD_N]
        bg = bm[:, g * SSD_N:(g + 1) * SSD_N]
        hs = slice(g * (SSD_HEADS // SSD_GROUPS), (g + 1) * (SSD_HEADS // SSD_GROUPS))
        h_old = h_ref[hs].reshape(GROUP_W, SSD_N)
        y_g = _dot_nt(cg.astype(BF16), h_old.astype(BF16)) * ecum[:, gl]
        for j in range(n_new):
            cbj = jnp.sum(cg * bg[j:j + 1, :], axis=1, keepdims=True)
            term = cbj * jnp.exp(cum_x[:, gl] - cum_x[j:j + 1, gl]) * xdt[j:j + 1, gl]
            y_g = y_g + jnp.where(row[:, 0:1] >= j, term, 0.0)
        y_parts.append(y_g)
        dec_col = _dot_tn(jnp.concatenate([p[:, gl] for p in dec3] + [zrow], axis=0), ones)
        upd = _dot_tn(w_end[:, gl].astype(BF16), bg.astype(BF16))
        hn_ref[hs] = (h_old * dec_col + upd).reshape(SSD_HEADS // SSD_GROUPS, SSD_P, SSD_N)
    y = jnp.concatenate(y_parts, axis=1) + dx_ref[...] * xs
    y_ref[...] = _group_norm_out(y, z_ref[...], ng_ref)


def _ssd_sample(xbc, cs, dt, z, h_all, li, h_new_prev, cw, cb, dtb, alog, dx, ng, e, n_new):
    DB = xbc.shape[0]
    bb = SSD_SAMPLE_SEQS if DB % SSD_SAMPLE_SEQS == 0 else 1
    tile = lambda w: pl.BlockSpec((bb, 8, w), lambda b: (b, 0, 0))
    const = lambda r, c: pl.BlockSpec((r, c), lambda b: (0, 0))
    state = pl.BlockSpec((None, bb, SSD_HEADS, SSD_P, SSD_N), lambda b: (li, b, 0, 0, 0))
    in_specs = [tile(D_XBC), tile(D_XBC), tile(LANES), tile(D_SSD), state, const(CONV_W, D_XBC), const(1, D_XBC),
                const(1, LANES), const(1, LANES), const(1, D_SSD), const(1, D_SSD), const(3 * LANES, D_SSD)]
    args = [xbc, cs, dt, z, h_all, cw, cb, dtb, alog, dx, ng, e]
    aliases = {}
    if h_new_prev is not None:
        in_specs.append(pl.BlockSpec(memory_space=pl.ANY))
        args.append(h_new_prev)
        aliases = {len(args) - 1: 1}
        state_out = state
    else:
        state_out = pl.BlockSpec((h_all.shape[0], bb, SSD_HEADS, SSD_P, SSD_N), lambda b: (0, b, 0, 0, 0))
    return pl.pallas_call(
        functools.partial(_ssd_s_kernel, n_new=n_new, li=li),
        grid=(DB // bb,),
        in_specs=in_specs,
        out_specs=[tile(D_SSD), state_out],
        out_shape=[jax.ShapeDtypeStruct((DB, 8, D_SSD), F32), jax.ShapeDtypeStruct(h_all.shape, F32)],
        input_output_aliases=aliases,
        compiler_params=_cparams(("arbitrary",)),
        name="ssd_sample",
    )(*args)


def _out_tile(x_ref, y_lru, o_ref, gattn_ref, yssd_ref, w_ref, g_ref, out_ref):
    o = jnp.concatenate([o_ref[c] for c in range(D_ATTN // LANES)], axis=1)
    y_attn = o * _silu(gattn_ref[...])
    mix_in = jnp.concatenate([y_lru.astype(BF16), y_attn.astype(BF16), yssd_ref[...].astype(BF16)], axis=1)
    mix = jnp.dot(mix_in, w_ref[...], preferred_element_type=F32)
    out_ref[...] = x_ref[...] + mix * lax.rsqrt(jnp.mean(mix * mix, axis=-1, keepdims=True) + EPS) * g_ref[...]


def _out_kernel(x_ref, ylru_ref, o_ref, gattn_ref, yssd_ref, w_ref, g_ref, out_ref):
    _out_tile(x_ref, ylru_ref[...], o_ref, gattn_ref, yssd_ref, w_ref, g_ref, out_ref)


def _out_proj(x2d, y_lru, o4, g_attn, y_ssd, w_bf16, g, tm):
    T = x2d.shape[0]
    row = lambda width: pl.BlockSpec((tm, width), lambda i: (i, 0))
    o_spec = pl.BlockSpec((D_ATTN // LANES, tm, LANES), lambda i: (0, i, 0))
    return pl.pallas_call(
        _out_kernel,
        grid=(T // tm,),
        in_specs=[row(D_MODEL), row(D_LRU), o_spec,
                  row(D_ATTN), row(D_SSD), pl.BlockSpec((D_LRU + D_ATTN + D_SSD, D_MODEL), lambda i: (0, 0)),
                  pl.BlockSpec((1, D_MODEL), lambda i: (0, 0))],
        out_specs=row(D_MODEL),
        out_shape=jax.ShapeDtypeStruct((T, D_MODEL), F32),
        compiler_params=_cparams(("arbitrary",)),
        name="out_proj",
    )(x2d, y_lru, o4, g_attn, y_ssd, w_bf16, g)


def _block_diag(w):
    k, n, _ = w.shape
    return jnp.einsum("kij,kl->kilj", w, jnp.eye(k, dtype=w.dtype)).reshape(k * n, k * n)


def _pad_rows(a, rows):
    return jnp.pad(a, ((0, 0), (0, rows - a.shape[1]), (0, 0)))


def _pairs_to_heads(a4, nb, L):
    return a4.reshape(4, nb, L, 2, HEAD_DIM).transpose(1, 2, 0, 3, 4).reshape(nb, L, N_HEADS, HEAD_DIM)


def kernel(x_prompt, x_sample, state_lru_conv, state_lru_h, cache_attn_k, cache_attn_v, state_ssd_conv, state_ssd_h, pre_norm_g, post_norm_g, w_in, lru_conv_w, lru_conv_b, lru_w_a, lru_b_a, lru_w_x, lru_b_x, lru_lambda, ssd_conv_w, ssd_conv_b, ssd_dt_bias, ssd_a_log, ssd_d, ssd_norm_g, w_out):
    depth = w_in.shape[0]
    B, S, _ = x_prompt.shape
    DB, L, _ = x_sample.shape
    assert S % SPAN == 0 and L <= 8 and L >= CONV_W - 1
    win = min(MAX_WINDOW, S)

    lane_pad = lambda a: jnp.pad(a.reshape(1, -1), ((0, 0), (0, LANES - a.shape[-1])))
    expand = jnp.asarray(np.tile(np.repeat(np.eye(LANES, SSD_HEADS, dtype=np.float32), SSD_P, axis=1), (3, 1)), BF16)
    kc_t = jnp.transpose(cache_attn_k, (0, 1, 3, 4, 2))
    vc_t = jnp.transpose(cache_attn_v, (0, 1, 3, 4, 2))

    yp = x_prompt.reshape(B * S, D_MODEL)
    ys = x_sample.reshape(DB * L, D_MODEL)
    outs_p = []
    outs_s = []
    kt_p = vt_p = ssd_h_s = None
    for li in range(depth):
        w_in_b = w_in[li].astype(BF16)
        w_dt_b = jnp.pad(w_in[li][:, OFF_DT:], ((0, 0), (0, D_IN_PAD - D_IN))).astype(BF16)
        w_out_b = w_out[li].astype(BF16)
        pre_g = pre_norm_g[li].reshape(1, -1)
        post_g = post_norm_g[li].reshape(1, -1)
        wg = jnp.concatenate([_block_diag(lru_w_a[li]), _block_diag(lru_w_x[li])], axis=1).astype(BF16)
        bg = jnp.concatenate([lru_b_a[li], lru_b_x[li]]).reshape(1, -1)
        lcw, lcb, lam = lru_conv_w[li], lru_conv_b[li].reshape(1, -1), lru_lambda[li].reshape(1, -1)
        scw, scb = ssd_conv_w[li], ssd_conv_b[li].reshape(1, -1)
        dtb, alog = lane_pad(ssd_dt_bias[li]), lane_pad(ssd_a_log[li])
        dx = jnp.repeat(ssd_d[li], SSD_P).reshape(1, -1)
        ng = ssd_norm_g[li].reshape(1, -1)

        xc_lru, gate_lru, q4, k4, v4, gattn, gate_z, xc_ssd, dtr, kt_p, vt_p, tail_lru, tail_xbc = _proj(
            yp.reshape(B, S, D_MODEL), pre_g, w_in_b, w_dt_b, 512,
            cache=(li, depth, win, kt_p, vt_p, lcw, lcb, scw, scb))
        xlru_s, glru_s, q4s, k4s, v4s, gattn_s, z_s, xbc_s, dtr_s = [
            a.reshape(a.shape[:-3] + (DB * L, a.shape[-1]))
            for a in _proj(ys.reshape(1, DB * L, D_MODEL), pre_g, w_in_b, w_dt_b, min(256, DB * L))]

        o4 = _attn_prompt(q4, k4, v4)
        ssd_steps = B * (S // (SSD_PROMPT_CHUNKS * SSD_CHUNK))
        ride = L & (L - 1) == 0 and 8 % L == 0 and (DB * L) % 8 == 0 and (DB * L) // 8 == ssd_steps
        if ride:
            y_ssd, ssd_h_p, o_s = _ssd_prompt(xc_ssd, dtr, gate_z, dtb, alog, dx, ng, expand,
                                              sample_attn=(q4s, k4s, v4s, kc_t, vc_t, li, L))
        else:
            y_ssd, ssd_h_p = _ssd_prompt(xc_ssd, dtr, gate_z, dtb, alog, dx, ng, expand)
            o_s = _attn_sample(q4s, k4s, v4s, kc_t, vc_t, li, L)
        yp3, lru_h_p = _lru_out_prompt(xc_lru, gate_lru, wg, bg, lam, yp.reshape(B, S, D_MODEL), o4, gattn, y_ssd,
                                       w_out_b, post_g, 512)
        yp = yp3.reshape(B * S, D_MODEL)
        outs_p.append((tail_lru[:, 8 - (CONV_W - 1):], lru_h_p.reshape(B, D_LRU), tail_xbc[:, 8 - (CONV_W - 1):],
                       ssd_h_p))

        xlru3 = xlru_s.reshape(DB, L, D_LRU)
        y_lru_tm, lru_h_s = _lru_sample(
            xlru3.transpose(1, 0, 2), glru_s.reshape(DB, L, D_LRU).transpose(1, 0, 2),
            state_lru_conv[li].transpose(1, 0, 2), state_lru_h[li], lcw, lcb, wg, bg, lam)
        y_lru = y_lru_tm.transpose(1, 0, 2).reshape(DB * L, D_LRU)

        xbc3 = xbc_s.reshape(DB, L, D_XBC)
        cs8 = jnp.pad(state_ssd_conv[li], ((0, 0), (8 - (CONV_W - 1), 0), (0, 0)))
        y_ssd8, ssd_h_s = _ssd_sample(_pad_rows(xbc3, 8), cs8, _pad_rows(dtr_s.reshape(DB, L, LANES), 8),
                                      _pad_rows(z_s.reshape(DB, L, D_SSD), 8), state_ssd_h, li, ssd_h_s, scw, scb,
                                      dtb, alog, dx, ng, expand, L)
        y_ssd = y_ssd8[:, :L].reshape(DB * L, D_SSD)
        ys = _out_proj(ys, y_lru, o_s, gattn_s, y_ssd, w_out_b, post_g, DB * L)
        outs_s.append((xlru3[:, L - (CONV_W - 1):], lru_h_s, _pairs_to_heads(k4s, DB, L),
                       _pairs_to_heads(v4s, DB, L), xbc3[:, L - (CONV_W - 1):]))

    lru_conv_p, lru_h_p, ssd_conv_p, ssd_h_p = [jnp.stack(c) for c in zip(*outs_p)]
    lru_conv_s, lru_h_s, k_s, v_s, ssd_conv_s = [jnp.stack(c) for c in zip(*outs_s)]
    k_p = kt_p.reshape(depth, B, N_HEADS, HEAD_DIM, win).transpose(0, 1, 4, 2, 3)
    v_p = vt_p.reshape(depth, B, N_HEADS, HEAD_DIM, win).transpose(0, 1, 4, 2, 3)
    return (yp.reshape(B, S, D_MODEL), ys.reshape(DB, L, D_MODEL), lru_conv_p, lru_conv_s, lru_h_p, lru_h_s,
            k_p, k_s, v_p, v_s, ssd_conv_p, ssd_conv_s, ssd_h_p, ssd_h_s)
```

```python
import functools

import numpy as np
import jax
import jax.numpy as jnp
from jax import lax
from jax.experimental import pallas as pl
from jax.experimental.pallas import tpu as pltpu

F32 = jnp.float32
BF16 = jnp.bfloat16

D_MODEL = 1024
CONV_W = 4
EPS = 1e-6
D_LRU = 512
LRU_BLOCKS = 8
LRU_C = 8.0
HEAD_DIM = 64
D_ATTN = 512
N_HEADS = 8
DILATIONS = ((128, 1), (512, 4), (2048, 16))
MAX_WINDOW = 2048
ATTN_BLOCK = 128
D_SSD = 1024
SSD_P = 64
SSD_HEADS = 16
SSD_GROUPS = 2
SSD_N = 128
SSD_CHUNK = 128
D_XBC = D_SSD + 2 * SSD_GROUPS * SSD_N
GROUP_W = D_SSD // SSD_GROUPS

OFF_XLRU, OFF_GLRU, OFF_Q, OFF_K, OFF_V, OFF_GATTN, OFF_Z, OFF_XBC, OFF_DT = (
    0, 512, 1024, 1536, 2048, 2560, 3072, 4096, 5632)
D_IN = 5648
LANES = 128
D_IN_PAD = OFF_DT + LANES

SPAN = 2048
NRES = SPAN // ATTN_BLOCK
SSD_PROMPT_CHUNKS = 2
SSD_SAMPLE_SEQS = 8
MASKED = 1e33
NEG = -1e30

VMEM_LIMIT = 56 * 1024 * 1024


def _cparams(sem):
    return pltpu.CompilerParams(dimension_semantics=sem, vmem_limit_bytes=VMEM_LIMIT)


def _silu(x):
    return x * jax.nn.sigmoid(x)


def _softplus(x):
    return jnp.maximum(x, 0.0) + jnp.log1p(jnp.exp(-jnp.abs(x)))


def _split3(x):
    h1 = x.astype(BF16)
    r1 = x - h1.astype(F32)
    h2 = r1.astype(BF16)
    h3 = (r1 - h2.astype(F32)).astype(BF16)
    return h1, h2, h3


def _dot3_right(x, m3_bf16):
    return jnp.dot(jnp.concatenate(_split3(x), axis=1), m3_bf16, preferred_element_type=F32)


def _dot3_left(m_bf16, x):
    return jnp.dot(jnp.concatenate([m_bf16] * 3, axis=1), jnp.concatenate(_split3(x), axis=0),
                   preferred_element_type=F32)


def _dot_nt(a, b):
    return lax.dot_general(a, b, (((1,), (1,)), ((), ())), preferred_element_type=F32)


def _dot_tn(a, b):
    return lax.dot_general(a, b, (((0,), (0,)), ((), ())), preferred_element_type=F32)


def _proj_kernel(x_ref, g_ref, w_ref, wdt_ref, *rest, first_win_tile, n_alias, cache_li, prompt):
    if prompt:
        lcw_ref, lcb_ref, scw_ref, scb_ref = rest[:4]
        rest = rest[4 + n_alias:]
        kt_ref, vt_ref, tail_lru_ref, tail_xbc_ref, xe_lru, xe_xbc = rest[9:]
    xlru_ref, glru_ref, q_ref, k_ref, v_ref, gattn_ref, z_ref, xbc_ref, dt_ref = rest[:9]
    n = x_ref.shape[0]
    x = x_ref[...]
    h = x * lax.rsqrt(jnp.mean(x * x, axis=-1, keepdims=True) + EPS) * g_ref[...]
    hb = h.astype(BF16)

    def mm(off, width):
        return jnp.dot(hb, w_ref[:, off:off + width], preferred_element_type=F32)

    def conv(raw, xe_ref, cw_ref, cb_ref, tail_ref, cols):
        ext = jnp.concatenate([xe_ref[:, cols], raw], axis=0)
        out = cb_ref[:, cols] + raw * cw_ref[CONV_W - 1:CONV_W, cols]
        for back in range(1, CONV_W):
            out = out + pltpu.roll(ext, back, axis=0)[8:, :] * cw_ref[CONV_W - 1 - back:CONV_W - back, cols]
        last = raw[n - 8:n, :]
        xe_ref[:, cols] = last
        tail_ref[:, cols] = last
        return out

    def pairs(dst, val):
        for c in range(D_ATTN // LANES):
            dst[c] = val[:, c * LANES:(c + 1) * LANES]

    dt_of = lambda: jnp.dot(hb, wdt_ref[...], preferred_element_type=F32)
    if prompt:
        @pl.when(pl.program_id(1) == 0)
        def _():
            xe_lru[0:8, :] = jnp.zeros((8, D_LRU), F32)
            xe_xbc[0:8, :] = jnp.zeros((8, D_XBC), F32)

        pw = 512
        cols = [slice(j * pw, (j + 1) * pw) for j in range(D_XBC // pw)]
        xbc_conv = lambda raw, c: _silu(conv(raw, xe_xbc, scw_ref, scb_ref, tail_xbc_ref, c))
        raw = mm(OFF_XBC, pw)
        pairs(q_ref, mm(OFF_Q, D_ATTN) * (HEAD_DIM ** -0.5))
        xbc_ref[:, cols[0]] = xbc_conv(raw, cols[0])
        raw = mm(OFF_XBC + pw, pw)
        k = mm(OFF_K, D_ATTN)
        xbc_ref[:, cols[1]] = xbc_conv(raw, cols[1])
        raw = mm(OFF_XBC + 2 * pw, pw)
        v = mm(OFF_V, D_ATTN)
        xbc_ref[:, cols[2]] = xbc_conv(raw, cols[2])
        pairs(k_ref, k)
        pairs(v_ref, v)
        raw = mm(OFF_XLRU, D_LRU)
        gattn_ref[...] = mm(OFF_GATTN, D_ATTN)
        xlru_ref[...] = conv(raw, xe_lru, lcw_ref, lcb_ref, tail_lru_ref, slice(None))
        z0 = mm(OFF_Z, pw)
        g_lru = mm(OFF_GLRU, D_LRU)
        z_ref[:, :pw] = _silu(z0)
        z1 = mm(OFF_Z + pw, D_SSD - pw)
        dt_ref[...] = dt_of()
        z_ref[:, pw:] = _silu(z1)
        glru_ref[...] = _silu(g_lru)
    else:
        xbc_ref[...] = mm(OFF_XBC, D_XBC)
        xlru_ref[...] = mm(OFF_XLRU, D_LRU)
        z_ref[...] = mm(OFF_Z, D_SSD)
        glru_ref[...] = mm(OFF_GLRU, D_LRU)
        pairs(q_ref, mm(OFF_Q, D_ATTN) * (HEAD_DIM ** -0.5))
        k = mm(OFF_K, D_ATTN)
        v = mm(OFF_V, D_ATTN)
        pairs(k_ref, k)
        pairs(v_ref, v)
        gattn_ref[...] = mm(OFF_GATTN, D_ATTN)
        dt_ref[...] = dt_of()
    if prompt:
        @pl.when(pl.program_id(1) >= first_win_tile)
        def _():
            for dst, val in ((kt_ref, k.T), (vt_ref, v.T)):
                if dst.ndim == 2:
                    dst[...] = val
                else:
                    for l in range(dst.shape[0]):
                        dst[l] = val if l == cache_li else jnp.zeros_like(val)


def _proj(x3, g, w_bf16, wdt_bf16, tm, cache=None):
    B, S, _ = x3.shape
    row = lambda width: pl.BlockSpec((None, tm, width), lambda b, t: (b, t, 0))
    const = lambda r, c: pl.BlockSpec((r, c), lambda b, t: (0, 0))
    pair = pl.BlockSpec((D_ATTN // LANES, None, tm, LANES), lambda b, t: (0, b, t, 0))
    pair_shape = (4, B, S, LANES)
    sds = lambda *s: jax.ShapeDtypeStruct(s, F32)
    weight = pl.BlockSpec((D_MODEL, D_IN), lambda b, t: (0, 0), pipeline_mode=pl.Buffered(1))
    in_specs = [row(D_MODEL), const(1, D_MODEL), weight, const(D_MODEL, LANES)]
    args = [x3, g, w_bf16, wdt_bf16]
    out_specs = [row(D_LRU), row(D_LRU), pair, pair, pair, row(D_ATTN), row(D_SSD), row(D_XBC), row(LANES)]
    out_shape = [sds(B, S, D_LRU), sds(B, S, D_LRU), sds(*pair_shape), sds(*pair_shape), sds(*pair_shape),
                 sds(B, S, D_ATTN), sds(B, S, D_SSD), sds(B, S, D_XBC), sds(B, S, LANES)]
    first_win_tile, n_alias, aliases, cache_li, scratch = 0, 0, {}, 0, []
    if cache is not None:
        li, depth, win, prev_kt, prev_vt, lcw, lcb, scw, scb = cache
        in_specs += [const(CONV_W, D_LRU), const(1, D_LRU), const(CONV_W, D_XBC), const(1, D_XBC)]
        args += [lcw, lcb, scw, scb]
        first_win_tile = (S - win) // tm
        cache_li = li
        win_tile = lambda t: jnp.maximum(t - first_win_tile, 0)
        if prev_kt is None:
            cspec = pl.BlockSpec((depth, None, D_ATTN, tm), lambda b, t: (0, b, 0, win_tile(t)))
        else:
            cspec = pl.BlockSpec((None, None, D_ATTN, tm), lambda b, t: (li, b, 0, win_tile(t)))
        tail = lambda width: pl.BlockSpec((None, 8, width), lambda b, t: (b, 0, 0))
        out_specs += [cspec, cspec, tail(D_LRU), tail(D_XBC)]
        out_shape += [sds(depth, B, D_ATTN, win), sds(depth, B, D_ATTN, win), sds(B, 8, D_LRU), sds(B, 8, D_XBC)]
        scratch = [pltpu.VMEM((8, D_LRU), F32), pltpu.VMEM((8, D_XBC), F32)]
        if prev_kt is not None:
            n_alias = 2
            in_specs += [pl.BlockSpec(memory_space=pl.ANY)] * 2
            args += [prev_kt, prev_vt]
            aliases = {len(args) - 2: 9, len(args) - 1: 10}
    return pl.pallas_call(
        functools.partial(_proj_kernel, first_win_tile=first_win_tile, n_alias=n_alias, cache_li=cache_li,
                          prompt=cache is not None),
        grid=(B, S // tm),
        in_specs=in_specs,
        out_specs=out_specs,
        out_shape=out_shape,
        scratch_shapes=scratch,
        input_output_aliases=aliases,
        compiler_params=_cparams(("arbitrary", "arbitrary")),
        name="in_proj",
    )(*args)


def _lru_gates(xc, wg_ref, bg_ref, lam_ref):
    pre = jnp.dot(xc.astype(BF16), wg_ref[...], preferred_element_type=F32) + bg_ref[...]
    r = jax.nn.sigmoid(pre[:, :D_LRU])
    i = jax.nn.sigmoid(pre[:, D_LRU:])
    log_a = (-LRU_C) * r * _softplus(-lam_ref[...])
    a = jnp.exp(log_a)
    t = jnp.tanh(log_a)
    return a, jnp.sqrt(-2.0 * t / (1.0 - t)) * (i * xc)


def _scan_rows(a, b, h0):
    n, width = a.shape
    a = a.reshape(n // 8, 8, width)
    b = b.reshape(n // 8, 8, width)
    row = lax.broadcasted_iota(jnp.int32, a.shape, 1)
    k = 1
    while k < 8:
        keep = row >= k
        a_sh = jnp.where(keep, pltpu.roll(a, k, axis=1), 1.0)
        b_sh = jnp.where(keep, pltpu.roll(b, k, axis=1), 0.0)
        b = a * b_sh + b
        a = a * a_sh
        k *= 2
    blocks = []
    h = h0
    for j in range(n // 8):
        blk = b[j] + a[j] * h
        blocks.append(blk)
        h = blk[7:8, :]
    return jnp.concatenate(blocks, axis=0), h


def _conv_rows(xe_ref, n, w_ref, b_ref, cols=slice(None)):
    y = b_ref[:, cols] + xe_ref[8:8 + n, cols] * w_ref[3:4, cols]
    for tap in range(CONV_W - 1):
        y = y + xe_ref[5 + tap:5 + tap + n, cols] * w_ref[tap:tap + 1, cols]
    return y


def _lru_out_kernel(xc_ref, gate_ref, wg_ref, bg_ref, lam_ref, x_ref, o_ref, gattn_ref, yssd_ref, w_ref, g_ref,
                    out_ref, hl_ref, h_ref):
    @pl.when(pl.program_id(1) == 0)
    def _():
        h_ref[...] = jnp.zeros_like(h_ref)

    rest = _out_partial(o_ref, gattn_ref, yssd_ref, w_ref)
    a, b = _lru_gates(xc_ref[...], wg_ref, bg_ref, lam_ref)
    hs, h_last = _scan_rows(a, b, h_ref[0:1, :])
    h_ref[...] = jnp.broadcast_to(h_last, h_ref.shape)
    hl_ref[...] = h_last
    _out_finish(x_ref, hs * gate_ref[...], rest, w_ref, g_ref, out_ref)


def _lru_out_prompt(xc, gate, wg, bg, lam, x3, o4, g_attn, y_ssd, w_bf16, g, tt):
    B, S, _ = xc.shape
    tile = lambda w: pl.BlockSpec((None, tt, w), lambda b, t: (b, t, 0))
    const = lambda r, c: pl.BlockSpec((r, c), lambda b, t: (0, 0))
    return pl.pallas_call(
        _lru_out_kernel,
        grid=(B, S // tt),
        in_specs=[tile(D_LRU), tile(D_LRU), const(D_LRU, 2 * D_LRU), const(1, 2 * D_LRU), const(1, D_LRU),
                  tile(D_MODEL), pl.BlockSpec((D_ATTN // LANES, None, tt, LANES), lambda b, t: (0, b, t, 0)),
                  tile(D_ATTN), tile(D_SSD), const(D_LRU + D_ATTN + D_SSD, D_MODEL), const(1, D_MODEL)],
        out_specs=[tile(D_MODEL), pl.BlockSpec((None, 1, D_LRU), lambda b, t: (b, 0, 0))],
        out_shape=[jax.ShapeDtypeStruct((B, S, D_MODEL), F32), jax.ShapeDtypeStruct((B, 1, D_LRU), F32)],
        scratch_shapes=[pltpu.VMEM((8, D_LRU), F32)],
        compiler_params=_cparams(("arbitrary", "arbitrary")),
        name="lru_out_prompt",
    )(xc, gate, wg, bg, lam, x3, o4, g_attn, y_ssd, w_bf16, g)


def _lru_s_kernel(x_ref, g_ref, cs_ref, h0_ref, cw_ref, cb_ref, wg_ref, bg_ref, lam_ref, y_ref, hl_ref):
    L = x_ref.shape[0]
    db = x_ref.shape[1]
    xs = [cs_ref[i] for i in range(CONV_W - 1)] + [x_ref[i] for i in range(L)]
    xc = []
    for t in range(L):
        y = cb_ref[...] + xs[t] * cw_ref[0:1, :]
        for tap in range(1, CONV_W):
            y = y + xs[t + tap] * cw_ref[tap:tap + 1, :]
        xc.append(y)
    a, b = _lru_gates(jnp.concatenate(xc, axis=0), wg_ref, bg_ref, lam_ref)
    h = h0_ref[...]
    for t in range(L):
        h = a[t * db:(t + 1) * db] * h + b[t * db:(t + 1) * db]
        y_ref[t] = h * _silu(g_ref[t])
    hl_ref[...] = h


def _lru_sample(x_tm, g_tm, cs_tm, h0, cw, cb, wg, bg, lam):
    L, DB, _ = x_tm.shape
    return pl.pallas_call(
        _lru_s_kernel,
        out_shape=[jax.ShapeDtypeStruct((L, DB, D_LRU), F32), jax.ShapeDtypeStruct((DB, D_LRU), F32)],
        compiler_params=pltpu.CompilerParams(vmem_limit_bytes=VMEM_LIMIT),
        name="lru_sample",
    )(x_tm, g_tm, cs_tm, h0, cw, cb, wg, bg, lam)


def _block_order(rho, dil):
    if dil == 16:
        return rho
    if dil == 4:
        return 4 * (rho & 31) + (rho >> 5)
    return 16 * (rho & 7) + (rho >> 3)


def _chunks(dil):
    return {16: (1, 128), 4: (4, 32), 1: (16, 8)}[dil]


def _attn_p_kernel(q_ref, k_ref, v_ref, o_ref, qs, ks, vs, r_st, m_st, bias_ref, *, unroll):
    c = pl.program_id(1)
    s = pl.program_id(2)
    nres = SPAN // ATTN_BLOCK
    lane_q = lax.broadcasted_iota(jnp.int32, (ATTN_BLOCK, LANES), 1)
    first_half = lane_q < HEAD_DIM

    cur = s & 1
    prev = 1 - cur

    @pl.when(s == 0)
    def _():
        ks[prev] = jnp.zeros((SPAN, LANES), F32)
        vs[0, prev] = jnp.zeros((SPAN, LANES), F32)
        vs[1, prev] = jnp.zeros((SPAN, LANES), F32)

    pr = lax.broadcasted_iota(jnp.int32, (ATTN_BLOCK, ATTN_BLOCK), 0)
    pc = lax.broadcasted_iota(jnp.int32, (ATTN_BLOCK, ATTN_BLOCK), 1)
    perm = jnp.where(pc == nres * (pr & 7) + (pr >> 3), 1.0, 0.0).astype(BF16)
    for g in range(SPAN // ATTN_BLOCK):
        src = slice(g * ATTN_BLOCK, (g + 1) * ATTN_BLOCK)
        move = lambda ref: jnp.dot(perm, ref[src, :].astype(BF16), preferred_element_type=F32)
        q, k, v = move(q_ref), move(k_ref), move(v_ref)
        pieces = ((qs, (0,), jnp.where(first_half, q, 0.0)), (qs, (1,), jnp.where(first_half, 0.0, q)),
                  (ks, (cur,), k),
                  (vs, (0, cur), jnp.where(first_half, v, 1.0)), (vs, (1, cur), jnp.where(first_half, 1.0, v)))
        for r in range(nres):
            for ref, lead, val in pieces:
                ref[lead + (slice(r * ATTN_BLOCK + 8 * g, r * ATTN_BLOCK + 8 * g + 8), slice(None))] = (
                    val[8 * r:8 * r + 8, :])

    span0 = s == 0

    @pl.when(span0)
    def _():
        iq = lax.broadcasted_iota(jnp.int32, (ATTN_BLOCK, 2 * ATTN_BLOCK), 0)
        ik = lax.broadcasted_iota(jnp.int32, (ATTN_BLOCK, 2 * ATTN_BLOCK), 1)
        in_cur = ik >= ATTN_BLOCK
        for di, (window, dil) in enumerate(DILATIONS):
            j = ATTN_BLOCK + _block_order(iq, dil) - (
                jnp.where(in_cur, ATTN_BLOCK, 0) + _block_order(ik & (ATTN_BLOCK - 1), dil))
            valid = (j >= 0) & (j <= window // dil)
            dist = (j * dil).astype(F32)
            dist_all = jnp.where(valid, dist, MASKED)
            dist_cur = jnp.where(valid & in_cur, dist, MASKED)
            for hh in range(2):
                slope = jnp.exp2(-(jnp.zeros((1, 1), F32) + (2 * c + hh + 1).astype(F32)))
                bias_ref[di, hh, 0] = -slope * dist_all
                bias_ref[di, hh, 1] = -slope * dist_cur

    def block(di, dil, it):
        nchunk, crow = _chunks(dil)

        def gather(ref, lead, starts):
            parts = [ref[lead + (pl.ds(st, crow), slice(None))] for st in starts]
            return parts[0] if len(parts) == 1 else jnp.concatenate(parts, axis=0)

        if dil == 16:
            base, cb, last_off = it * ATTN_BLOCK, 0, 0
        elif dil == 4:
            base, cb, last_off = (it >> 2) * ATTN_BLOCK, it & 3, 3 * crow
        else:
            base, cb, last_off = 0, it, 15 * crow
        stride = {16: 0, 4: 4 * ATTN_BLOCK, 1: ATTN_BLOCK}[dil]
        cur_starts = [pl.multiple_of(base + a * stride + crow * cb, 8) for a in range(nchunk)]
        if dil == 16:
            prev_slot = prev
            prev_starts = cur_starts
            first_in_span = True
        else:
            has_prev = cb > 0
            prev_slot = jnp.where(has_prev, cur, prev)
            prev_off = jnp.where(has_prev, crow * (cb - 1), last_off)
            prev_starts = [pl.multiple_of(base + a * stride + prev_off, 8) for a in range(nchunk)]
            first_in_span = jnp.logical_not(has_prev)
        kb = jnp.concatenate([gather(ks, (prev_slot,), prev_starts), gather(ks, (cur,), cur_starts)],
                             axis=0).astype(BF16)
        variant = jnp.where(jnp.logical_and(span0, first_in_span), 1, 0)
        for hh in range(2):
            qh = gather(qs, (hh,), cur_starts).astype(BF16)
            vh = jnp.concatenate([gather(vs, (hh, prev_slot), prev_starts), gather(vs, (hh, cur), cur_starts)],
                                 axis=0).astype(BF16)
            sc = _dot_nt(qh, kb) + bias_ref[di, hh, variant]
            m = jnp.max(sc, axis=1, keepdims=True)
            p = jnp.exp(sc - m)
            r_new = jnp.dot(p.astype(BF16), vh, preferred_element_type=F32)
            m_new = jnp.broadcast_to(m, (ATTN_BLOCK, LANES))
            for a, st in enumerate(cur_starts):
                r_st[di, hh, pl.ds(st, crow), :] = r_new[a * crow:(a + 1) * crow, :]
                m_st[di, hh, pl.ds(st, crow), :] = m_new[a * crow:(a + 1) * crow, :]

    def body(it, carry):
        for di, (_, dil) in enumerate(DILATIONS):
            block(di, dil, it)
        return carry

    lax.fori_loop(0, nres, body, 0, unroll=unroll)

    ndil = len(DILATIONS)
    for r in range(nres):
        src = slice(r * ATTN_BLOCK, (r + 1) * ATTN_BLOCK)
        outs = []
        for hh in range(2):
            ms = [m_st[di, hh, src, :] for di in range(ndil)]
            m_tot = functools.reduce(jnp.maximum, ms)
            tot = sum(r_st[di, hh, src, :] * jnp.exp(ms[di] - m_tot) for di in range(ndil))
            outs.append(tot / pltpu.roll(tot, HEAD_DIM, axis=1))
        o_ref[pl.ds(r, ATTN_BLOCK, stride=nres), :] = jnp.where(first_half, outs[0], outs[1])


def _attn_prompt(q4, k4, v4, unroll=8):
    npair, B, S, _ = q4.shape
    ndil = len(DILATIONS)
    spec = pl.BlockSpec((None, None, SPAN, LANES), lambda b, c, s: (c, b, s, 0))
    return pl.pallas_call(
        functools.partial(_attn_p_kernel, unroll=unroll),
        grid=(B, npair, S // SPAN),
        in_specs=[spec, spec, spec],
        out_specs=spec,
        out_shape=jax.ShapeDtypeStruct(q4.shape, F32),
        scratch_shapes=[pltpu.VMEM((2, SPAN, LANES), F32), pltpu.VMEM((2, SPAN, LANES), F32),
                        pltpu.VMEM((2, 2, SPAN, LANES), F32),
                        pltpu.VMEM((ndil, 2, SPAN, LANES), F32), pltpu.VMEM((ndil, 2, SPAN, LANES), F32),
                        pltpu.VMEM((ndil, 2, 2, ATTN_BLOCK, 2 * ATTN_BLOCK), F32)],
        compiler_params=_cparams(("arbitrary", "arbitrary", "arbitrary")),
        name="attn_prompt",
    )(q4, k4, v4)


def _attn_s_kernel(*refs, n_new):
    _attn_s_body(*refs, n_new=n_new, first=pl.program_id(0) == 0)


def _attn_s_body(q_ref, kn_ref, vn_ref, kc_ref, vc_ref, o_ref, dist_ref, mult_ref, *, n_new, first):
    nseq, nheads, hd, W = kc_ref.shape
    rows = q_ref.shape[1]
    G = 4
    gr, gl = G * rows, G * hd
    shift = n_new.bit_length() - 1

    @pl.when(first)
    def _():
        i = lax.broadcasted_iota(jnp.int32, (gr, W), 0) & (n_new - 1)
        pos = lax.broadcasted_iota(jnp.int32, (gr, W), 1)
        d = W + i - pos
        cnt = jnp.zeros((gr, W), F32)
        for window, dil in DILATIONS:
            cnt = cnt + jnp.where(((d & (dil - 1)) == 0) & (d <= window), 1.0, 0.0)
        dist_ref[...] = d.astype(F32)
        mult_ref[...] = cnt

    r = lax.broadcasted_iota(jnp.int32, (gr, rows), 0) & (rows - 1)
    r2 = lax.broadcasted_iota(jnp.int32, (gr, rows), 1)
    dn = (r & (n_new - 1)) - (r2 & (n_new - 1))
    same_seq = (r >> shift) == (r2 >> shift)
    mult_n = jnp.zeros((gr, rows), F32)
    for window, dil in DILATIONS:
        ok = ((dn & (dil - 1)) == 0) & (dn >= 0) & (dn <= window) & same_seq
        mult_n = mult_n + jnp.where(ok, 1.0, 0.0)
    dist_n = dn.astype(F32)
    dist_c = dist_ref[...]
    mult_c = mult_ref[...]
    row_head = lax.broadcasted_iota(jnp.int32, (gr, gl), 0) >> 3
    hd_shift = hd.bit_length() - 1
    lane_head = lax.broadcasted_iota(jnp.int32, (gr, gl), 1) >> hd_shift
    diag = row_head == lane_head
    out_lane_head = lax.broadcasted_iota(jnp.int32, (rows, gl), 1) >> hd_shift
    out_row_seq = lax.broadcasted_iota(jnp.int32, (rows, gl), 0) >> shift
    head_col = lax.broadcasted_iota(jnp.int32, (gr, 1), 0) >> 3
    npair_g = gl // LANES
    for g in range(nheads // G):
        pairs = range(g * npair_g, (g + 1) * npair_g)
        cat = lambda ref: jnp.concatenate([ref[c] for c in pairs], axis=1)
        q_bd = jnp.where(diag, jnp.concatenate([cat(q_ref)] * G, axis=0), 0.0).astype(BF16)
        slope = jnp.exp2(-(head_col + (g * G + 1)).astype(F32))
        sn = _dot_nt(q_bd, cat(kn_ref).astype(BF16))
        sn = jnp.where(mult_n > 0, sn - slope * dist_n, NEG)
        mn = jnp.max(sn, axis=1, keepdims=True)
        vn = cat(vn_ref).astype(BF16)
        o_g = jnp.zeros((rows, gl), F32)
        for b in range(nseq):
            kst = kc_ref[b, g * G:(g + 1) * G].reshape(gl, W).astype(BF16)
            vst = vc_ref[b, g * G:(g + 1) * G].reshape(gl, W).astype(BF16)
            sc = jnp.dot(q_bd, kst, preferred_element_type=F32)
            sc = jnp.where(mult_c > 0, sc - slope * dist_c, NEG)
            m = jnp.maximum(jnp.max(sc, axis=1, keepdims=True), mn)
            pc = mult_c * jnp.exp(sc - m)
            pn = mult_n * jnp.exp(sn - m)
            l = jnp.sum(pc, axis=1, keepdims=True) + jnp.sum(pn, axis=1, keepdims=True)
            o = (_dot_nt(pc.astype(BF16), vst) + jnp.dot(pn.astype(BF16), vn, preferred_element_type=F32)) / l
            sel = sum(jnp.where(out_lane_head == h, o[h * rows:(h + 1) * rows, :], 0.0) for h in range(G))
            o_g = jnp.where(out_row_seq == b, sel, o_g)
        for j, c in enumerate(pairs):
            o_ref[c] = o_g[:, j * LANES:(j + 1) * LANES]


def _attn_sample(q4, k4, v4, kc_t, vc_t, li, n_new):
    small, cache, tables = _attn_s_specs(q4, kc_t, li, n_new, lambda b: b)
    return pl.pallas_call(
        functools.partial(_attn_s_kernel, n_new=n_new),
        grid=(q4.shape[1] // 8,),
        in_specs=[small, small, small, cache, cache],
        out_specs=small,
        out_shape=jax.ShapeDtypeStruct(q4.shape, F32),
        scratch_shapes=tables,
        compiler_params=_cparams(("arbitrary",)),
        name="attn_sample",
    )(q4, k4, v4, kc_t, vc_t)


def _attn_s_specs(q4, kc_t, li, n_new, group_of):
    npair, T, _ = q4.shape
    W = kc_t.shape[-1]
    rows = 8
    assert n_new & (n_new - 1) == 0 and rows % n_new == 0 and T % rows == 0
    small = pl.BlockSpec((npair, rows, LANES), lambda *g: (0, group_of(*g), 0))
    cache = pl.BlockSpec((None, rows // n_new, N_HEADS, HEAD_DIM, W), lambda *g: (li, group_of(*g), 0, 0, 0))
    return small, cache, [pltpu.VMEM((4 * rows, W), F32), pltpu.VMEM((4 * rows, W), F32)]


def _group_norm_out(y, z, ng_ref, gate_ready=False):
    yg = y * (z if gate_ready else _silu(z))
    outs = []
    for g in range(SSD_GROUPS):
        part = yg[:, g * GROUP_W:(g + 1) * GROUP_W]
        outs.append(part * lax.rsqrt(jnp.mean(part * part, axis=-1, keepdims=True) + EPS))
    return jnp.concatenate(outs, axis=1) * ng_ref[...]


def _ssd_p_kernel(*refs, n_new):
    fused = n_new is not None
    n_in = 13 if fused else 8
    y_ref, hl_ref = refs[n_in:n_in + 2]
    scratch = refs[n_in + (3 if fused else 2):]
    _ssd_p_tile(*refs[:8], y_ref, hl_ref, scratch[0])
    if fused:
        first = jnp.logical_and(pl.program_id(0) == 0, pl.program_id(1) == 0)
        _attn_s_body(*refs[8:13], refs[n_in + 2], scratch[1], scratch[2], n_new=n_new, first=first)


def _ssd_p_tile(xc_ref, dt_ref, gate_ref, dtb_ref, alog_ref, dx_ref, ng_ref, e_ref, y_ref, hl_ref, ht_ref):
    t = pl.program_id(1)
    Q = SSD_CHUNK
    n = xc_ref.shape[0]

    @pl.when(t == 0)
    def _():
        ht_ref[...] = jnp.zeros_like(ht_ref)

    xc_all = xc_ref[...]
    dt_all = _softplus(dt_ref[...] + dtb_ref[...])
    a_neg = -jnp.exp(alog_ref[...])
    ri = lax.broadcasted_iota(jnp.int32, (Q, Q), 0)
    ci = lax.broadcasted_iota(jnp.int32, (Q, Q), 1)
    causal = ri >= ci
    tri = jnp.where(causal, 1.0, 0.0).astype(BF16)
    lane = lax.broadcasted_iota(jnp.int32, (Q, LANES), 1)
    for ck in range(n // Q):
        rows = slice(ck * Q, (ck + 1) * Q)
        y = _ssd_chunk(xc_all[rows, :], dt_all[rows, :], a_neg, causal, tri, lane, dx_ref, e_ref, ht_ref)
        y_ref[rows, :] = _group_norm_out(y, gate_ref[rows, :], ng_ref, gate_ready=True).astype(y_ref.dtype)

    @pl.when(t == pl.num_programs(1) - 1)
    def _():
        for g in range(SSD_GROUPS):
            hl_ref[g * (SSD_HEADS // SSD_GROUPS):(g + 1) * (SSD_HEADS // SSD_GROUPS)] = (
                ht_ref[g].T.reshape(SSD_HEADS // SSD_GROUPS, SSD_P, SSD_N))


def _ssd_chunk(xc, dt, a_neg, causal, tri, lane, dx_ref, e_ref, ht_ref):
    Q = SSD_CHUNK
    xs = xc[:, :D_SSD]
    bm = xc[:, D_SSD:D_SSD + SSD_GROUPS * SSD_N]
    cm = xc[:, D_SSD + SSD_GROUPS * SSD_N:]
    cum = _dot3_left(tri, dt * a_neg)
    cum_t = cum.T
    both_x = _dot3_right(jnp.concatenate([cum, dt], axis=0), e_ref[...])
    cum_x = both_x[:Q]
    dt_x = both_x[Q:]
    xdt = xs * dt_x
    ecum = jnp.exp(cum_x)
    cl = cum_x[Q - 1:Q, :]
    w_end = xdt * jnp.exp(cl - cum_x)
    dec = jnp.exp(cl)

    y_parts = []
    for g in range(SSD_GROUPS):
        cg = cm[:, g * SSD_N:(g + 1) * SSD_N].astype(BF16)
        bg = bm[:, g * SSD_N:(g + 1) * SSD_N]
        cb = _dot_nt(cg, bg.astype(BF16))
        gl = slice(g * GROUP_W, (g + 1) * GROUP_W)
        h_old = ht_ref[g]
        y_off = jnp.dot(cg, h_old.astype(BF16), preferred_element_type=F32) * ecum[:, gl]
        ht_ref[g] = h_old * dec[:, gl] + jnp.dot(bg.T.astype(BF16), w_end[:, gl].astype(BF16),
                                                  preferred_element_type=F32)
        for pp in range(GROUP_W // LANES):
            pl_ = slice(g * GROUP_W + pp * LANES, g * GROUP_W + (pp + 1) * LANES)
            xdt_pair = xdt[:, pl_].astype(BF16)
            halves = []
            for hh in range(2):
                h = (g * GROUP_W + pp * LANES) // SSD_P + hh
                seg = jnp.broadcast_to(cum[:, h:h + 1], (Q, Q)) - jnp.broadcast_to(cum_t[h:h + 1, :], (Q, Q))
                scores = cb * jnp.exp(jnp.where(causal, seg, NEG))
                halves.append(jnp.dot(scores.astype(BF16), xdt_pair, preferred_element_type=F32))
            y_diag = jnp.where(lane < SSD_P, halves[0], halves[1])
            y_parts.append(y_diag + y_off[:, pp * LANES:(pp + 1) * LANES])
    return jnp.concatenate(y_parts, axis=1) + dx_ref[...] * xs


def _ssd_prompt(xc, dt, gate, dtb, alog, dx, ng, e, sample_attn=None):
    B, S, _ = xc.shape
    Q = SSD_PROMPT_CHUNKS * SSD_CHUNK
    nt = S // Q
    tile = lambda w: pl.BlockSpec((None, Q, w), lambda b, t: (b, t, 0))
    const = lambda r, c: pl.BlockSpec((r, c), lambda b, t: (0, 0))
    in_specs = [tile(D_XBC), tile(LANES), tile(D_SSD), const(1, LANES), const(1, LANES), const(1, D_SSD),
                const(1, D_SSD), const(3 * LANES, D_SSD)]
    args = [xc, dt, gate, dtb, alog, dx, ng, e]
    out_specs = [tile(D_SSD), pl.BlockSpec((None, SSD_HEADS, SSD_P, SSD_N), lambda b, t: (b, 0, 0, 0))]
    out_shape = [jax.ShapeDtypeStruct((B, S, D_SSD), BF16), jax.ShapeDtypeStruct((B, SSD_HEADS, SSD_P, SSD_N), F32)]
    scratch = [pltpu.VMEM((SSD_GROUPS, SSD_N, GROUP_W), F32)]
    n_new = None
    if sample_attn is not None:
        q4, k4, v4, kc_t, vc_t, li, n_new = sample_attn
        small, cache, tables = _attn_s_specs(q4, kc_t, li, n_new, lambda b, t: b * nt + t)
        assert q4.shape[1] // 8 == B * nt
        in_specs += [small, small, small, cache, cache]
        args += [q4, k4, v4, kc_t, vc_t]
        out_specs.append(small)
        out_shape.append(jax.ShapeDtypeStruct(q4.shape, F32))
        scratch += tables
    return pl.pallas_call(
        functools.partial(_ssd_p_kernel, n_new=n_new),
        grid=(B, nt),
        in_specs=in_specs,
        out_specs=out_specs,
        out_shape=out_shape,
        scratch_shapes=scratch,
        compiler_params=_cparams(("arbitrary", "arbitrary")),
        name="ssd_prompt" if sample_attn is None else "ssd_prompt_attn_sample",
    )(*args)


def _ssd_s_kernel(xbc_ref, cs_ref, dt_ref, z_ref, h_ref, cw_ref, cb_ref, dtb_ref, alog_ref, dx_ref, ng_ref, e_ref,
                  *rest, n_new, li):
    y_ref, hn_ref = rest[-2:]
    if hn_ref.ndim == 5:
        for l in range(hn_ref.shape[0]):
            if l != li:
                hn_ref[l] = jnp.zeros(hn_ref.shape[1:], F32)
        hn_ref = hn_ref.at[li]
    nseq = xbc_ref.shape[0]
    pre = [_ssd_s_pre(xbc_ref.at[i], cs_ref.at[i], dt_ref.at[i], cw_ref, cb_ref, dtb_ref, alog_ref, n_new)
           for i in range(nseq)]
    both_x = _dot3_right(jnp.concatenate([p[3] for p in pre] + [p[4] for p in pre], axis=0), e_ref[...])
    for i in range(nseq):
        xs, bm, cm = pre[i][:3]
        _ssd_s_post(xs, bm, cm, both_x[8 * i:8 * i + 8], both_x[8 * (nseq + i):8 * (nseq + i) + 8], z_ref.at[i],
                    h_ref.at[i], dx_ref, ng_ref, y_ref.at[i], hn_ref.at[i], n_new)


def _ssd_s_pre(xbc_ref, cs_ref, dt_ref, cw_ref, cb_ref, dtb_ref, alog_ref, n_new):
    R = 8
    row = lax.broadcasted_iota(jnp.int32, (R, LANES), 0)
    real = row < n_new
    xe = jnp.concatenate([cs_ref[...], xbc_ref[...]], axis=0)
    y = cb_ref[...] + xe[8:16, :] * cw_ref[3:4, :]
    for tap in range(CONV_W - 1):
        y = y + xe[5 + tap:13 + tap, :] * cw_ref[tap:tap + 1, :]
    xc = _silu(y)
    xs = xc[:, :D_SSD]
    bm = xc[:, D_SSD:D_SSD + SSD_GROUPS * SSD_N]
    cm = xc[:, D_SSD + SSD_GROUPS * SSD_N:]

    dt = jnp.where(real, _softplus(dt_ref[...] + dtb_ref[...]), 0.0)
    da = dt * (-jnp.exp(alog_ref[...]))
    cum = da
    k = 1
    while k < R:
        cum = cum + jnp.where(row >= k, pltpu.roll(cum, k, axis=0), 0.0)
        k *= 2
    return xs, bm, cm, cum, dt


def _ssd_s_post(xs, bm, cm, cum_x, dt_x, z_ref, h_ref, dx_ref, ng_ref, y_ref, hn_ref, n_new):
    R = 8
    row = lax.broadcasted_iota(jnp.int32, (R, LANES), 0)
    xdt = xs * dt_x
    ecum = jnp.exp(cum_x)
    cl = cum_x[R - 1:R, :]
    w_end = xdt * jnp.exp(cl - cum_x)
    dec3 = _split3(jnp.exp(cl))
    ones = jnp.ones((R, SSD_N), BF16)
    zrow = jnp.zeros((R - 3, GROUP_W), BF16)

    y_parts = []
    for g in range(SSD_GROUPS):
        gl = slice(g * GROUP_W, (g + 1) * GROUP_W)
        cg = cm[:, g * SSD_N:(g + 1) * SSD_N]
        bg = bm[:, g * SSD_N:(g + 1) * SSD_N]
        hs = slice(g * (SSD_HEADS // SSD_GROUPS), (g + 1) * (SSD_HEADS // SSD_GROUPS))
        h_old = h_ref[hs].reshape(GROUP_W, SSD_N)
        y_g = _dot_nt(cg.astype(BF16), h_old.astype(BF16)) * ecum[:, gl]
        for j in range(n_new):
            cbj = jnp.sum(cg * bg[j:j + 1, :], axis=1, keepdims=True)
            term = cbj * jnp.exp(cum_x[:, gl] - cum_x[j:j + 1, gl]) * xdt[j:j + 1, gl]
            y_g = y_g + jnp.where(row[:, 0:1] >= j, term, 0.0)
        y_parts.append(y_g)
        dec_col = _dot_tn(jnp.concatenate([p[:, gl] for p in dec3] + [zrow], axis=0), ones)
        upd = _dot_tn(w_end[:, gl].astype(BF16), bg.astype(BF16))
        hn_ref[hs] = (h_old * dec_col + upd).reshape(SSD_HEADS // SSD_GROUPS, SSD_P, SSD_N)
    y = jnp.concatenate(y_parts, axis=1) + dx_ref[...] * xs
    y_ref[...] = _group_norm_out(y, z_ref[...], ng_ref)


def _ssd_sample(xbc, cs, dt, z, h_all, li, h_new_prev, cw, cb, dtb, alog, dx, ng, e, n_new):
    DB = xbc.shape[0]
    bb = SSD_SAMPLE_SEQS if DB % SSD_SAMPLE_SEQS == 0 else 1
    tile = lambda w: pl.BlockSpec((bb, 8, w), lambda b: (b, 0, 0))
    const = lambda r, c: pl.BlockSpec((r, c), lambda b: (0, 0))
    state = pl.BlockSpec((None, bb, SSD_HEADS, SSD_P, SSD_N), lambda b: (li, b, 0, 0, 0))
    in_specs = [tile(D_XBC), tile(D_XBC), tile(LANES), tile(D_SSD), state, const(CONV_W, D_XBC), const(1, D_XBC),
                const(1, LANES), const(1, LANES), const(1, D_SSD), const(1, D_SSD), const(3 * LANES, D_SSD)]
    args = [xbc, cs, dt, z, h_all, cw, cb, dtb, alog, dx, ng, e]
    aliases = {}
    if h_new_prev is not None:
        in_specs.append(pl.BlockSpec(memory_space=pl.ANY))
        args.append(h_new_prev)
        aliases = {len(args) - 1: 1}
        state_out = state
    else:
        state_out = pl.BlockSpec((h_all.shape[0], bb, SSD_HEADS, SSD_P, SSD_N), lambda b: (0, b, 0, 0, 0))
    return pl.pallas_call(
        functools.partial(_ssd_s_kernel, n_new=n_new, li=li),
        grid=(DB // bb,),
        in_specs=in_specs,
        out_specs=[tile(D_SSD), state_out],
        out_shape=[jax.ShapeDtypeStruct((DB, 8, D_SSD), F32), jax.ShapeDtypeStruct(h_all.shape, F32)],
        input_output_aliases=aliases,
        compiler_params=_cparams(("arbitrary",)),
        name="ssd_sample",
    )(*args)


def _out_partial(o_ref, gattn_ref, yssd_ref, w_ref):
    o = jnp.concatenate([o_ref[c] for c in range(D_ATTN // LANES)], axis=1)
    y_attn = o * _silu(gattn_ref[...])
    mix_in = jnp.concatenate([y_attn.astype(BF16), yssd_ref[...].astype(BF16)], axis=1)
    return jnp.dot(mix_in, w_ref[D_LRU:, :], preferred_element_type=F32)


def _out_finish(x_ref, y_lru, rest, w_ref, g_ref, out_ref):
    mix = rest + jnp.dot(y_lru.astype(BF16), w_ref[:D_LRU, :], preferred_element_type=F32)
    out_ref[...] = x_ref[...] + mix * lax.rsqrt(jnp.mean(mix * mix, axis=-1, keepdims=True) + EPS) * g_ref[...]


def _out_kernel(x_ref, ylru_ref, o_ref, gattn_ref, yssd_ref, w_ref, g_ref, out_ref):
    _out_finish(x_ref, ylru_ref[...], _out_partial(o_ref, gattn_ref, yssd_ref, w_ref), w_ref, g_ref, out_ref)


def _out_proj(x2d, y_lru, o4, g_attn, y_ssd, w_bf16, g, tm):
    T = x2d.shape[0]
    row = lambda width: pl.BlockSpec((tm, width), lambda i: (i, 0))
    o_spec = pl.BlockSpec((D_ATTN // LANES, tm, LANES), lambda i: (0, i, 0))
    return pl.pallas_call(
        _out_kernel,
        grid=(T // tm,),
        in_specs=[row(D_MODEL), row(D_LRU), o_spec,
                  row(D_ATTN), row(D_SSD), pl.BlockSpec((D_LRU + D_ATTN + D_SSD, D_MODEL), lambda i: (0, 0)),
                  pl.BlockSpec((1, D_MODEL), lambda i: (0, 0))],
        out_specs=row(D_MODEL),
        out_shape=jax.ShapeDtypeStruct((T, D_MODEL), F32),
        compiler_params=_cparams(("arbitrary",)),
        name="out_proj",
    )(x2d, y_lru, o4, g_attn, y_ssd, w_bf16, g)


def _block_diag(w):
    k, n, _ = w.shape
    return jnp.einsum("kij,kl->kilj", w, jnp.eye(k, dtype=w.dtype)).reshape(k * n, k * n)


def _pad_rows(a, rows):
    return jnp.pad(a, ((0, 0), (0, rows - a.shape[1]), (0, 0)))


def _pairs_to_heads(a4, nb, L):
    return a4.reshape(4, nb, L, 2, HEAD_DIM).transpose(1, 2, 0, 3, 4).reshape(nb, L, N_HEADS, HEAD_DIM)


def kernel(x_prompt, x_sample, state_lru_conv, state_lru_h, cache_attn_k, cache_attn_v, state_ssd_conv, state_ssd_h, pre_norm_g, post_norm_g, w_in, lru_conv_w, lru_conv_b, lru_w_a, lru_b_a, lru_w_x, lru_b_x, lru_lambda, ssd_conv_w, ssd_conv_b, ssd_dt_bias, ssd_a_log, ssd_d, ssd_norm_g, w_out):
    depth = w_in.shape[0]
    B, S, _ = x_prompt.shape
    DB, L, _ = x_sample.shape
    assert S % SPAN == 0 and L <= 8 and L >= CONV_W - 1
    win = min(MAX_WINDOW, S)

    lane_pad = lambda a: jnp.pad(a.reshape(1, -1), ((0, 0), (0, LANES - a.shape[-1])))
    expand = jnp.asarray(np.tile(np.repeat(np.eye(LANES, SSD_HEADS, dtype=np.float32), SSD_P, axis=1), (3, 1)), BF16)
    kc_t = jnp.transpose(cache_attn_k, (0, 1, 3, 4, 2))
    vc_t = jnp.transpose(cache_attn_v, (0, 1, 3, 4, 2))

    yp = x_prompt.reshape(B * S, D_MODEL)
    ys = x_sample.reshape(DB * L, D_MODEL)
    outs_p = []
    outs_s = []
    kt_p = vt_p = ssd_h_s = None
    for li in range(depth):
        w_in_b = w_in[li].astype(BF16)
        w_dt_b = jnp.pad(w_in[li][:, OFF_DT:], ((0, 0), (0, D_IN_PAD - D_IN))).astype(BF16)
        w_out_b = w_out[li].astype(BF16)
        pre_g = pre_norm_g[li].reshape(1, -1)
        post_g = post_norm_g[li].reshape(1, -1)
        wg = jnp.concatenate([_block_diag(lru_w_a[li]), _block_diag(lru_w_x[li])], axis=1).astype(BF16)
        bg = jnp.concatenate([lru_b_a[li], lru_b_x[li]]).reshape(1, -1)
        lcw, lcb, lam = lru_conv_w[li], lru_conv_b[li].reshape(1, -1), lru_lambda[li].reshape(1, -1)
        scw, scb = ssd_conv_w[li], ssd_conv_b[li].reshape(1, -1)
        dtb, alog = lane_pad(ssd_dt_bias[li]), lane_pad(ssd_a_log[li])
        dx = jnp.repeat(ssd_d[li], SSD_P).reshape(1, -1)
        ng = ssd_norm_g[li].reshape(1, -1)

        xc_lru, gate_lru, q4, k4, v4, gattn, gate_z, xc_ssd, dtr, kt_p, vt_p, tail_lru, tail_xbc = _proj(
            yp.reshape(B, S, D_MODEL), pre_g, w_in_b, w_dt_b, 512,
            cache=(li, depth, win, kt_p, vt_p, lcw, lcb, scw, scb))
        xlru_s, glru_s, q4s, k4s, v4s, gattn_s, z_s, xbc_s, dtr_s = [
            a.reshape(a.shape[:-3] + (DB * L, a.shape[-1]))
            for a in _proj(ys.reshape(1, DB * L, D_MODEL), pre_g, w_in_b, w_dt_b, min(256, DB * L))]

        o4 = _attn_prompt(q4, k4, v4)
        ssd_steps = B * (S // (SSD_PROMPT_CHUNKS * SSD_CHUNK))
        ride = L & (L - 1) == 0 and 8 % L == 0 and (DB * L) % 8 == 0 and (DB * L) // 8 == ssd_steps
        if ride:
            y_ssd, ssd_h_p, o_s = _ssd_prompt(xc_ssd, dtr, gate_z, dtb, alog, dx, ng, expand,
                                              sample_attn=(q4s, k4s, v4s, kc_t, vc_t, li, L))
        else:
            y_ssd, ssd_h_p = _ssd_prompt(xc_ssd, dtr, gate_z, dtb, alog, dx, ng, expand)
            o_s = _attn_sample(q4s, k4s, v4s, kc_t, vc_t, li, L)
        yp3, lru_h_p = _lru_out_prompt(xc_lru, gate_lru, wg, bg, lam, yp.reshape(B, S, D_MODEL), o4, gattn, y_ssd,
                                       w_out_b, post_g, 512)
        yp = yp3.reshape(B * S, D_MODEL)
        outs_p.append((tail_lru[:, 8 - (CONV_W - 1):], lru_h_p.reshape(B, D_LRU), tail_xbc[:, 8 - (CONV_W - 1):],
                       ssd_h_p))

        xlru3 = xlru_s.reshape(DB, L, D_LRU)
        y_lru_tm, lru_h_s = _lru_sample(
            xlru3.transpose(1, 0, 2), glru_s.reshape(DB, L, D_LRU).transpose(1, 0, 2),
            state_lru_conv[li].transpose(1, 0, 2), state_lru_h[li], lcw, lcb, wg, bg, lam)
        y_lru = y_lru_tm.transpose(1, 0, 2).reshape(DB * L, D_LRU)

        xbc3 = xbc_s.reshape(DB, L, D_XBC)
        cs8 = jnp.pad(state_ssd_conv[li], ((0, 0), (8 - (CONV_W - 1), 0), (0, 0)))
        y_ssd8, ssd_h_s = _ssd_sample(_pad_rows(xbc3, 8), cs8, _pad_rows(dtr_s.reshape(DB, L, LANES), 8),
                                      _pad_rows(z_s.reshape(DB, L, D_SSD), 8), state_ssd_h, li, ssd_h_s, scw, scb,
                                      dtb, alog, dx, ng, expand, L)
        y_ssd = y_ssd8[:, :L].reshape(DB * L, D_SSD)
        ys = _out_proj(ys, y_lru, o_s, gattn_s, y_ssd, w_out_b, post_g, DB * L)
        outs_s.append((xlru3[:, L - (CONV_W - 1):], lru_h_s, _pairs_to_heads(k4s, DB, L),
                       _pairs_to_heads(v4s, DB, L), xbc3[:, L - (CONV_W - 1):]))

    lru_conv_p, lru_h_p, ssd_conv_p, ssd_h_p = [jnp.stack(c) for c in zip(*outs_p)]
    lru_conv_s, lru_h_s, k_s, v_s, ssd_conv_s = [jnp.stack(c) for c in zip(*outs_s)]
    k_p = kt_p.reshape(depth, B, N_HEADS, HEAD_DIM, win).transpose(0, 1, 4, 2, 3)
    v_p = vt_p.reshape(depth, B, N_HEADS, HEAD_DIM, win).transpose(0, 1, 4, 2, 3)
    return (yp.reshape(B, S, D_MODEL), ys.reshape(DB, L, D_MODEL), lru_conv_p, lru_conv_s, lru_h_p, lru_h_s,
            k_p, k_s, v_p, v_s, ssd_conv_p, ssd_conv_s, ssd_h_p, ssd_h_s)
```

```python
import functools

import numpy as np
import jax
import jax.numpy as jnp
from jax import lax
from jax.experimental import pallas as pl
from jax.experimental.pallas import tpu as pltpu

F32 = jnp.float32
BF16 = jnp.bfloat16

D_MODEL = 1024
CONV_W = 4
EPS = 1e-6
D_LRU = 512
LRU_BLOCKS = 8
LRU_C = 8.0
HEAD_DIM = 64
D_ATTN = 512
N_HEADS = 8
DILATIONS = ((128, 1), (512, 4), (2048, 16))
MAX_WINDOW = 2048
ATTN_BLOCK = 128
D_SSD = 1024
SSD_P = 64
SSD_HEADS = 16
SSD_GROUPS = 2
SSD_N = 128
SSD_CHUNK = 128
D_XBC = D_SSD + 2 * SSD_GROUPS * SSD_N
GROUP_W = D_SSD // SSD_GROUPS

OFF_XLRU, OFF_GLRU, OFF_Q, OFF_K, OFF_V, OFF_GATTN, OFF_Z, OFF_XBC, OFF_DT = (
    0, 512, 1024, 1536, 2048, 2560, 3072, 4096, 5632)
D_IN = 5648
LANES = 128
D_IN_PAD = OFF_DT + LANES

SPAN = 2048
NRES = SPAN // ATTN_BLOCK
SSD_PROMPT_CHUNKS = 2
SSD_SAMPLE_SEQS = 8
MASKED = 1e33
NEG = -1e30

VMEM_LIMIT = 56 * 1024 * 1024


def _cparams(sem):
    return pltpu.CompilerParams(dimension_semantics=sem, vmem_limit_bytes=VMEM_LIMIT)


def _silu(x):
    return x * jax.nn.sigmoid(x)


def _softplus(x):
    return jnp.maximum(x, 0.0) + jnp.log1p(jnp.exp(-jnp.abs(x)))


def _split3(x):
    h1 = x.astype(BF16)
    r1 = x - h1.astype(F32)
    h2 = r1.astype(BF16)
    h3 = (r1 - h2.astype(F32)).astype(BF16)
    return h1, h2, h3


def _dot3_right(x, m3_bf16):
    return jnp.dot(jnp.concatenate(_split3(x), axis=1), m3_bf16, preferred_element_type=F32)


def _dot3_left(m_bf16, x):
    return jnp.dot(jnp.concatenate([m_bf16] * 3, axis=1), jnp.concatenate(_split3(x), axis=0),
                   preferred_element_type=F32)


def _dot_nt(a, b):
    return lax.dot_general(a, b, (((1,), (1,)), ((), ())), preferred_element_type=F32)


def _dot_tn(a, b):
    return lax.dot_general(a, b, (((0,), (0,)), ((), ())), preferred_element_type=F32)


def _proj_kernel(x_ref, g_ref, w_ref, wdt_ref, *rest, first_win_tile, n_alias, cache_li, prompt):
    if prompt:
        lcw_ref, lcb_ref, scw_ref, scb_ref = rest[:4]
        rest = rest[4 + n_alias:]
        kt_ref, vt_ref, tail_lru_ref, tail_xbc_ref, xe_lru, xe_xbc = rest[9:]
    xlru_ref, glru_ref, q_ref, k_ref, v_ref, gattn_ref, z_ref, xbc_ref, dt_ref = rest[:9]
    n = x_ref.shape[0]
    x = x_ref[...]
    h = x * lax.rsqrt(jnp.mean(x * x, axis=-1, keepdims=True) + EPS) * g_ref[...]
    hb = h.astype(BF16)

    def mm(off, width):
        return jnp.dot(hb, w_ref[:, off:off + width], preferred_element_type=F32)

    def conv(raw, xe_ref, cw_ref, cb_ref, tail_ref, cols):
        ext = jnp.concatenate([xe_ref[:, cols], raw], axis=0)
        out = cb_ref[:, cols] + raw * cw_ref[CONV_W - 1:CONV_W, cols]
        for back in range(1, CONV_W):
            out = out + pltpu.roll(ext, back, axis=0)[8:, :] * cw_ref[CONV_W - 1 - back:CONV_W - back, cols]
        last = raw[n - 8:n, :]
        xe_ref[:, cols] = last
        tail_ref[:, cols] = last
        return out

    def pairs(dst, val):
        for c in range(D_ATTN // LANES):
            dst[c] = val[:, c * LANES:(c + 1) * LANES]

    dt_of = lambda: jnp.dot(hb, wdt_ref[...], preferred_element_type=F32)
    if prompt:
        @pl.when(pl.program_id(1) == 0)
        def _():
            xe_lru[0:8, :] = jnp.zeros((8, D_LRU), F32)
            xe_xbc[0:8, :] = jnp.zeros((8, D_XBC), F32)

        pw = 512
        cols = [slice(j * pw, (j + 1) * pw) for j in range(D_XBC // pw)]
        xbc_conv = lambda raw, c: _silu(conv(raw, xe_xbc, scw_ref, scb_ref, tail_xbc_ref, c))
        raw = mm(OFF_XBC, pw)
        pairs(q_ref, mm(OFF_Q, D_ATTN) * (HEAD_DIM ** -0.5))
        xbc_ref[:, cols[0]] = xbc_conv(raw, cols[0])
        raw = mm(OFF_XBC + pw, pw)
        k = mm(OFF_K, D_ATTN)
        xbc_ref[:, cols[1]] = xbc_conv(raw, cols[1])
        raw = mm(OFF_XBC + 2 * pw, pw)
        v = mm(OFF_V, D_ATTN)
        xbc_ref[:, cols[2]] = xbc_conv(raw, cols[2])
        pairs(k_ref, k)
        pairs(v_ref, v)
        raw = mm(OFF_XLRU, D_LRU)
        gattn_ref[...] = mm(OFF_GATTN, D_ATTN)
        xlru_ref[...] = conv(raw, xe_lru, lcw_ref, lcb_ref, tail_lru_ref, slice(None))
        z0 = mm(OFF_Z, pw)
        g_lru = mm(OFF_GLRU, D_LRU)
        z_ref[:, :pw] = _silu(z0)
        z1 = mm(OFF_Z + pw, D_SSD - pw)
        dt_ref[...] = dt_of()
        z_ref[:, pw:] = _silu(z1)
        glru_ref[...] = _silu(g_lru)
    else:
        xbc_ref[...] = mm(OFF_XBC, D_XBC)
        xlru_ref[...] = mm(OFF_XLRU, D_LRU)
        z_ref[...] = mm(OFF_Z, D_SSD)
        glru_ref[...] = mm(OFF_GLRU, D_LRU)
        pairs(q_ref, mm(OFF_Q, D_ATTN) * (HEAD_DIM ** -0.5))
        k = mm(OFF_K, D_ATTN)
        v = mm(OFF_V, D_ATTN)
        pairs(k_ref, k)
        pairs(v_ref, v)
        gattn_ref[...] = mm(OFF_GATTN, D_ATTN)
        dt_ref[...] = dt_of()
    if prompt:
        @pl.when(pl.program_id(1) >= first_win_tile)
        def _():
            for dst, val in ((kt_ref, k.T), (vt_ref, v.T)):
                if dst.ndim == 2:
                    dst[...] = val
                else:
                    for l in range(dst.shape[0]):
                        dst[l] = val if l == cache_li else jnp.zeros_like(val)


def _proj(x3, g, w_bf16, wdt_bf16, tm, cache=None):
    B, S, _ = x3.shape
    row = lambda width: pl.BlockSpec((None, tm, width), lambda b, t: (b, t, 0))
    const = lambda r, c: pl.BlockSpec((r, c), lambda b, t: (0, 0))
    pair = pl.BlockSpec((D_ATTN // LANES, None, tm, LANES), lambda b, t: (0, b, t, 0))
    pair_shape = (4, B, S, LANES)
    sds = lambda *s: jax.ShapeDtypeStruct(s, F32)
    weight = pl.BlockSpec((D_MODEL, D_IN), lambda b, t: (0, 0), pipeline_mode=pl.Buffered(1))
    in_specs = [row(D_MODEL), const(1, D_MODEL), weight, const(D_MODEL, LANES)]
    args = [x3, g, w_bf16, wdt_bf16]
    out_specs = [row(D_LRU), row(D_LRU), pair, pair, pair, row(D_ATTN), row(D_SSD), row(D_XBC), row(LANES)]
    out_shape = [sds(B, S, D_LRU), sds(B, S, D_LRU), sds(*pair_shape), sds(*pair_shape), sds(*pair_shape),
                 sds(B, S, D_ATTN), sds(B, S, D_SSD), sds(B, S, D_XBC), sds(B, S, LANES)]
    first_win_tile, n_alias, aliases, cache_li, scratch = 0, 0, {}, 0, []
    if cache is not None:
        li, depth, win, prev_kt, prev_vt, lcw, lcb, scw, scb = cache
        in_specs += [const(CONV_W, D_LRU), const(1, D_LRU), const(CONV_W, D_XBC), const(1, D_XBC)]
        args += [lcw, lcb, scw, scb]
        first_win_tile = (S - win) // tm
        cache_li = li
        win_tile = lambda t: jnp.maximum(t - first_win_tile, 0)
        if prev_kt is None:
            cspec = pl.BlockSpec((depth, None, D_ATTN, tm), lambda b, t: (0, b, 0, win_tile(t)))
        else:
            cspec = pl.BlockSpec((None, None, D_ATTN, tm), lambda b, t: (li, b, 0, win_tile(t)))
        tail = lambda width: pl.BlockSpec((None, 8, width), lambda b, t: (b, 0, 0))
        out_specs += [cspec, cspec, tail(D_LRU), tail(D_XBC)]
        out_shape += [sds(depth, B, D_ATTN, win), sds(depth, B, D_ATTN, win), sds(B, 8, D_LRU), sds(B, 8, D_XBC)]
        scratch = [pltpu.VMEM((8, D_LRU), F32), pltpu.VMEM((8, D_XBC), F32)]
        if prev_kt is not None:
            n_alias = 2
            in_specs += [pl.BlockSpec(memory_space=pl.ANY)] * 2
            args += [prev_kt, prev_vt]
            aliases = {len(args) - 2: 9, len(args) - 1: 10}
    return pl.pallas_call(
        functools.partial(_proj_kernel, first_win_tile=first_win_tile, n_alias=n_alias, cache_li=cache_li,
                          prompt=cache is not None),
        grid=(B, S // tm),
        in_specs=in_specs,
        out_specs=out_specs,
        out_shape=out_shape,
        scratch_shapes=scratch,
        input_output_aliases=aliases,
        compiler_params=_cparams(("arbitrary", "arbitrary")),
        name="in_proj",
    )(*args)


def _lru_gates(xc, wg_ref, bg_ref, lam_ref):
    pre = jnp.dot(xc.astype(BF16), wg_ref[...], preferred_element_type=F32) + bg_ref[...]
    r = jax.nn.sigmoid(pre[:, :D_LRU])
    i = jax.nn.sigmoid(pre[:, D_LRU:])
    log_a = (-LRU_C) * r * _softplus(-lam_ref[...])
    a = jnp.exp(log_a)
    t = jnp.tanh(log_a)
    return a, jnp.sqrt(-2.0 * t / (1.0 - t)) * (i * xc)


def _scan_rows(a, b, h0):
    n, width = a.shape
    a = a.reshape(n // 8, 8, width)
    b = b.reshape(n // 8, 8, width)
    row = lax.broadcasted_iota(jnp.int32, a.shape, 1)
    k = 1
    while k < 8:
        keep = row >= k
        a_sh = jnp.where(keep, pltpu.roll(a, k, axis=1), 1.0)
        b_sh = jnp.where(keep, pltpu.roll(b, k, axis=1), 0.0)
        b = a * b_sh + b
        a = a * a_sh
        k *= 2
    blocks = []
    h = h0
    for j in range(n // 8):
        blk = b[j] + a[j] * h
        blocks.append(blk)
        h = blk[7:8, :]
    return jnp.concatenate(blocks, axis=0), h


def _lru_out_kernel(xc_ref, gate_ref, wg_ref, bg_ref, lam_ref, x_ref, o_ref, gattn_ref, yssd_ref, w_ref, g_ref,
                    out_ref, hl_ref, h_ref):
    @pl.when(pl.program_id(1) == 0)
    def _():
        h_ref[...] = jnp.zeros_like(h_ref)

    rest = _out_partial(o_ref, gattn_ref, yssd_ref, w_ref)
    a, b = _lru_gates(xc_ref[...], wg_ref, bg_ref, lam_ref)
    hs, h_last = _scan_rows(a, b, h_ref[0:1, :])
    h_ref[...] = jnp.broadcast_to(h_last, h_ref.shape)
    hl_ref[...] = h_last
    _out_finish(x_ref, hs * gate_ref[...], rest, w_ref, g_ref, out_ref)


def _lru_out_prompt(xc, gate, wg, bg, lam, x3, o4, g_attn, y_ssd, w_bf16, g, tt):
    B, S, _ = xc.shape
    tile = lambda w: pl.BlockSpec((None, tt, w), lambda b, t: (b, t, 0))
    const = lambda r, c: pl.BlockSpec((r, c), lambda b, t: (0, 0))
    return pl.pallas_call(
        _lru_out_kernel,
        grid=(B, S // tt),
        in_specs=[tile(D_LRU), tile(D_LRU), const(D_LRU, 2 * D_LRU), const(1, 2 * D_LRU), const(1, D_LRU),
                  tile(D_MODEL), pl.BlockSpec((D_ATTN // LANES, None, tt, LANES), lambda b, t: (0, b, t, 0)),
                  tile(D_ATTN), tile(D_SSD), const(D_LRU + D_ATTN + D_SSD, D_MODEL), const(1, D_MODEL)],
        out_specs=[tile(D_MODEL), pl.BlockSpec((None, 1, D_LRU), lambda b, t: (b, 0, 0))],
        out_shape=[jax.ShapeDtypeStruct((B, S, D_MODEL), F32), jax.ShapeDtypeStruct((B, 1, D_LRU), F32)],
        scratch_shapes=[pltpu.VMEM((8, D_LRU), F32)],
        compiler_params=_cparams(("arbitrary", "arbitrary")),
        name="lru_out_prompt",
    )(xc, gate, wg, bg, lam, x3, o4, g_attn, y_ssd, w_bf16, g)


def _lru_s_kernel(x_ref, g_ref, cs_ref, h0_ref, cw_ref, cb_ref, wg_ref, bg_ref, lam_ref, y_ref, hl_ref):
    L = x_ref.shape[0]
    db = x_ref.shape[1]
    xs = [cs_ref[i] for i in range(CONV_W - 1)] + [x_ref[i] for i in range(L)]
    xc = []
    for t in range(L):
        y = cb_ref[...] + xs[t] * cw_ref[0:1, :]
        for tap in range(1, CONV_W):
            y = y + xs[t + tap] * cw_ref[tap:tap + 1, :]
        xc.append(y)
    a, b = _lru_gates(jnp.concatenate(xc, axis=0), wg_ref, bg_ref, lam_ref)
    h = h0_ref[...]
    for t in range(L):
        h = a[t * db:(t + 1) * db] * h + b[t * db:(t + 1) * db]
        y_ref[t] = h * _silu(g_ref[t])
    hl_ref[...] = h


def _lru_sample(x_tm, g_tm, cs_tm, h0, cw, cb, wg, bg, lam):
    L, DB, _ = x_tm.shape
    return pl.pallas_call(
        _lru_s_kernel,
        out_shape=[jax.ShapeDtypeStruct((L, DB, D_LRU), F32), jax.ShapeDtypeStruct((DB, D_LRU), F32)],
        compiler_params=pltpu.CompilerParams(vmem_limit_bytes=VMEM_LIMIT),
        name="lru_sample",
    )(x_tm, g_tm, cs_tm, h0, cw, cb, wg, bg, lam)


def _block_order(rho, dil):
    if dil == 16:
        return rho
    if dil == 4:
        return 4 * (rho & 31) + (rho >> 5)
    return 16 * (rho & 7) + (rho >> 3)


def _chunks(dil):
    return {16: (1, 128), 4: (4, 32), 1: (16, 8)}[dil]


def _attn_p_kernel(q_ref, k_ref, v_ref, o_ref, qs, ks, vs, r_st, m_st, bias_ref, *, unroll):
    c = pl.program_id(1)
    s = pl.program_id(2)
    nres = NRES
    lane_q = lax.broadcasted_iota(jnp.int32, (ATTN_BLOCK, LANES), 1)
    first_half = lane_q < HEAD_DIM

    cur = s & 1
    prev = 1 - cur

    @pl.when(s == 0)
    def _():
        ks[prev] = jnp.zeros((SPAN, LANES), F32)
        vs[0, prev] = jnp.zeros((SPAN, LANES), F32)
        vs[1, prev] = jnp.zeros((SPAN, LANES), F32)

    pr = lax.broadcasted_iota(jnp.int32, (ATTN_BLOCK, ATTN_BLOCK), 0)
    pc = lax.broadcasted_iota(jnp.int32, (ATTN_BLOCK, ATTN_BLOCK), 1)
    perm = jnp.where(pc == nres * (pr & 7) + (pr >> 3), 1.0, 0.0).astype(BF16)
    for g in range(SPAN // ATTN_BLOCK):
        src = slice(g * ATTN_BLOCK, (g + 1) * ATTN_BLOCK)
        qkv = jnp.concatenate([ref[src, :].astype(BF16) for ref in (q_ref, k_ref, v_ref)], axis=1)
        moved = jnp.dot(perm, qkv, preferred_element_type=F32)
        q, k, v = (moved[:, i * LANES:(i + 1) * LANES] for i in range(3))
        pieces = ((qs, (0,), jnp.where(first_half, q, 0.0)), (qs, (1,), jnp.where(first_half, 0.0, q)),
                  (ks, (cur,), k),
                  (vs, (0, cur), jnp.where(first_half, v, 1.0)), (vs, (1, cur), jnp.where(first_half, 1.0, v)))
        for r in range(nres):
            for ref, lead, val in pieces:
                ref[lead + (slice(r * ATTN_BLOCK + 8 * g, r * ATTN_BLOCK + 8 * g + 8), slice(None))] = (
                    val[8 * r:8 * r + 8, :])

    span0 = s == 0

    @pl.when(span0)
    def _():
        iq = lax.broadcasted_iota(jnp.int32, (ATTN_BLOCK, 2 * ATTN_BLOCK), 0)
        ik = lax.broadcasted_iota(jnp.int32, (ATTN_BLOCK, 2 * ATTN_BLOCK), 1)
        in_cur = ik >= ATTN_BLOCK
        for di, (window, dil) in enumerate(DILATIONS):
            j = ATTN_BLOCK + _block_order(iq, dil) - (
                jnp.where(in_cur, ATTN_BLOCK, 0) + _block_order(ik & (ATTN_BLOCK - 1), dil))
            valid = (j >= 0) & (j <= window // dil)
            dist = (j * dil).astype(F32)
            dist_all = jnp.where(valid, dist, MASKED)
            dist_cur = jnp.where(valid & in_cur, dist, MASKED)
            for hh in range(2):
                slope = jnp.exp2(-(jnp.zeros((1, 1), F32) + (2 * c + hh + 1).astype(F32)))
                bias_ref[di, hh, 0] = -slope * dist_all
                bias_ref[di, hh, 1] = -slope * dist_cur

    def block(di, dil, it):
        nchunk, crow = _chunks(dil)

        def gather(ref, lead, starts):
            parts = [ref[lead + (pl.ds(st, crow), slice(None))] for st in starts]
            return parts[0] if len(parts) == 1 else jnp.concatenate(parts, axis=0)

        if dil == 16:
            base, cb, last_off = it * ATTN_BLOCK, 0, 0
        elif dil == 4:
            base, cb, last_off = (it >> 2) * ATTN_BLOCK, it & 3, 3 * crow
        else:
            base, cb, last_off = 0, it, 15 * crow
        stride = {16: 0, 4: 4 * ATTN_BLOCK, 1: ATTN_BLOCK}[dil]
        cur_starts = [pl.multiple_of(base + a * stride + crow * cb, 8) for a in range(nchunk)]
        if dil == 16:
            prev_slot = prev
            prev_starts = cur_starts
            first_in_span = True
        else:
            has_prev = cb > 0
            prev_slot = jnp.where(has_prev, cur, prev)
            prev_off = jnp.where(has_prev, crow * (cb - 1), last_off)
            prev_starts = [pl.multiple_of(base + a * stride + prev_off, 8) for a in range(nchunk)]
            first_in_span = jnp.logical_not(has_prev)
        kb = jnp.concatenate([gather(ks, (prev_slot,), prev_starts), gather(ks, (cur,), cur_starts)],
                             axis=0).astype(BF16)
        variant = jnp.where(jnp.logical_and(span0, first_in_span), 1, 0)
        for hh in range(2):
            qh = gather(qs, (hh,), cur_starts).astype(BF16)
            vh = jnp.concatenate([gather(vs, (hh, prev_slot), prev_starts), gather(vs, (hh, cur), cur_starts)],
                                 axis=0).astype(BF16)
            sc = _dot_nt(qh, kb) + bias_ref[di, hh, variant]
            m = jnp.max(sc, axis=1, keepdims=True)
            p = jnp.exp(sc - m)
            r_new = jnp.dot(p.astype(BF16), vh, preferred_element_type=F32)
            m_new = jnp.broadcast_to(m, (ATTN_BLOCK, LANES))
            for a, st in enumerate(cur_starts):
                r_st[di, hh, pl.ds(st, crow), :] = r_new[a * crow:(a + 1) * crow, :]
                m_st[di, hh, pl.ds(st, crow), :] = m_new[a * crow:(a + 1) * crow, :]

    def body(it, carry):
        for di, (_, dil) in enumerate(DILATIONS):
            block(di, dil, it)
        return carry

    lax.fori_loop(0, nres, body, 0, unroll=unroll)

    ndil = len(DILATIONS)
    for r in range(nres):
        src = slice(r * ATTN_BLOCK, (r + 1) * ATTN_BLOCK)
        outs = []
        for hh in range(2):
            ms = [m_st[di, hh, src, :] for di in range(ndil)]
            m_tot = functools.reduce(jnp.maximum, ms)
            tot = sum(r_st[di, hh, src, :] * jnp.exp(ms[di] - m_tot) for di in range(ndil))
            outs.append(tot / pltpu.roll(tot, HEAD_DIM, axis=1))
        o_ref[pl.ds(r, ATTN_BLOCK, stride=nres), :] = jnp.where(first_half, outs[0], outs[1])


def _attn_prompt(q4, k4, v4, unroll=8):
    npair, B, S, _ = q4.shape
    ndil = len(DILATIONS)
    spec = pl.BlockSpec((None, None, SPAN, LANES), lambda b, c, s: (c, b, s, 0))
    return pl.pallas_call(
        functools.partial(_attn_p_kernel, unroll=unroll),
        grid=(B, npair, S // SPAN),
        in_specs=[spec, spec, spec],
        out_specs=spec,
        out_shape=jax.ShapeDtypeStruct(q4.shape, F32),
        scratch_shapes=[pltpu.VMEM((2, SPAN, LANES), F32), pltpu.VMEM((2, SPAN, LANES), F32),
                        pltpu.VMEM((2, 2, SPAN, LANES), F32),
                        pltpu.VMEM((ndil, 2, SPAN, LANES), F32), pltpu.VMEM((ndil, 2, SPAN, LANES), F32),
                        pltpu.VMEM((ndil, 2, 2, ATTN_BLOCK, 2 * ATTN_BLOCK), F32)],
        compiler_params=_cparams(("arbitrary", "arbitrary", "arbitrary")),
        name="attn_prompt",
    )(q4, k4, v4)


def _attn_s_kernel(*refs, n_new):
    _attn_s_body(*refs, n_new=n_new, first=pl.program_id(0) == 0)


def _attn_s_body(q_ref, kn_ref, vn_ref, kc_ref, vc_ref, o_ref, dist_ref, mult_ref, *, n_new, first):
    nseq, nheads, hd, W = kc_ref.shape
    rows = q_ref.shape[1]
    G = 4
    gr, gl = G * rows, G * hd
    shift = n_new.bit_length() - 1

    @pl.when(first)
    def _():
        i = lax.broadcasted_iota(jnp.int32, (gr, W), 0) & (n_new - 1)
        pos = lax.broadcasted_iota(jnp.int32, (gr, W), 1)
        d = W + i - pos
        cnt = jnp.zeros((gr, W), F32)
        for window, dil in DILATIONS:
            cnt = cnt + jnp.where(((d & (dil - 1)) == 0) & (d <= window), 1.0, 0.0)
        dist_ref[...] = d.astype(F32)
        mult_ref[...] = cnt

    r = lax.broadcasted_iota(jnp.int32, (gr, rows), 0) & (rows - 1)
    r2 = lax.broadcasted_iota(jnp.int32, (gr, rows), 1)
    dn = (r & (n_new - 1)) - (r2 & (n_new - 1))
    same_seq = (r >> shift) == (r2 >> shift)
    mult_n = jnp.zeros((gr, rows), F32)
    for window, dil in DILATIONS:
        ok = ((dn & (dil - 1)) == 0) & (dn >= 0) & (dn <= window) & same_seq
        mult_n = mult_n + jnp.where(ok, 1.0, 0.0)
    dist_n = dn.astype(F32)
    dist_c = dist_ref[...]
    mult_c = mult_ref[...]
    row_head = lax.broadcasted_iota(jnp.int32, (gr, gl), 0) >> 3
    hd_shift = hd.bit_length() - 1
    lane_head = lax.broadcasted_iota(jnp.int32, (gr, gl), 1) >> hd_shift
    diag = row_head == lane_head
    out_lane_head = lax.broadcasted_iota(jnp.int32, (rows, gl), 1) >> hd_shift
    out_row_seq = lax.broadcasted_iota(jnp.int32, (rows, gl), 0) >> shift
    head_col = lax.broadcasted_iota(jnp.int32, (gr, 1), 0) >> 3
    npair_g = gl // LANES
    for g in range(nheads // G):
        pairs = range(g * npair_g, (g + 1) * npair_g)
        cat = lambda ref: jnp.concatenate([ref[c] for c in pairs], axis=1)
        q_bd = jnp.where(diag, jnp.concatenate([cat(q_ref)] * G, axis=0), 0.0).astype(BF16)
        slope = jnp.exp2(-(head_col + (g * G + 1)).astype(F32))
        sn = _dot_nt(q_bd, cat(kn_ref).astype(BF16))
        sn = jnp.where(mult_n > 0, sn - slope * dist_n, NEG)
        mn = jnp.max(sn, axis=1, keepdims=True)
        vn = cat(vn_ref).astype(BF16)
        o_g = jnp.zeros((rows, gl), F32)
        for b in range(nseq):
            kst = kc_ref[b, g * G:(g + 1) * G].reshape(gl, W).astype(BF16)
            vst = vc_ref[b, g * G:(g + 1) * G].reshape(gl, W).astype(BF16)
            sc = jnp.dot(q_bd, kst, preferred_element_type=F32)
            sc = jnp.where(mult_c > 0, sc - slope * dist_c, NEG)
            m = jnp.maximum(jnp.max(sc, axis=1, keepdims=True), mn)
            pc = mult_c * jnp.exp(sc - m)
            pn = mult_n * jnp.exp(sn - m)
            l = jnp.sum(pc, axis=1, keepdims=True) + jnp.sum(pn, axis=1, keepdims=True)
            o = (_dot_nt(pc.astype(BF16), vst) + jnp.dot(pn.astype(BF16), vn, preferred_element_type=F32)) / l
            sel = sum(jnp.where(out_lane_head == h, o[h * rows:(h + 1) * rows, :], 0.0) for h in range(G))
            o_g = jnp.where(out_row_seq == b, sel, o_g)
        for j, c in enumerate(pairs):
            o_ref[c] = o_g[:, j * LANES:(j + 1) * LANES]


def _attn_sample(q4, k4, v4, kc_t, vc_t, li, n_new):
    small, cache, tables = _attn_s_specs(q4, kc_t, li, n_new, lambda b: b)
    return pl.pallas_call(
        functools.partial(_attn_s_kernel, n_new=n_new),
        grid=(q4.shape[1] // 8,),
        in_specs=[small, small, small, cache, cache],
        out_specs=small,
        out_shape=jax.ShapeDtypeStruct(q4.shape, F32),
        scratch_shapes=tables,
        compiler_params=_cparams(("arbitrary",)),
        name="attn_sample",
    )(q4, k4, v4, kc_t, vc_t)


def _attn_s_specs(q4, kc_t, li, n_new, group_of):
    npair, T, _ = q4.shape
    W = kc_t.shape[-1]
    rows = 8
    assert n_new & (n_new - 1) == 0 and rows % n_new == 0 and T % rows == 0
    small = pl.BlockSpec((npair, rows, LANES), lambda *g: (0, group_of(*g), 0))
    cache = pl.BlockSpec((None, rows // n_new, N_HEADS, HEAD_DIM, W), lambda *g: (li, group_of(*g), 0, 0, 0))
    return small, cache, [pltpu.VMEM((4 * rows, W), F32), pltpu.VMEM((4 * rows, W), F32)]


def _group_norm_out(y, z, ng_ref, gate_ready=False):
    yg = y * (z if gate_ready else _silu(z))
    outs = []
    for g in range(SSD_GROUPS):
        part = yg[:, g * GROUP_W:(g + 1) * GROUP_W]
        outs.append(part * lax.rsqrt(jnp.mean(part * part, axis=-1, keepdims=True) + EPS))
    return jnp.concatenate(outs, axis=1) * ng_ref[...]


def _ssd_p_kernel(*refs, n_new):
    fused = n_new is not None
    n_in = 13 if fused else 8
    y_ref, hl_ref = refs[n_in:n_in + 2]
    scratch = refs[n_in + (3 if fused else 2):]
    _ssd_p_tile(*refs[:8], y_ref, hl_ref, scratch[0])
    if fused:
        first = jnp.logical_and(pl.program_id(0) == 0, pl.program_id(1) == 0)
        _attn_s_body(*refs[8:13], refs[n_in + 2], scratch[1], scratch[2], n_new=n_new, first=first)


def _ssd_p_tile(xc_ref, dt_ref, gate_ref, dtb_ref, alog_ref, dx_ref, ng_ref, e_ref, y_ref, hl_ref, ht_ref):
    t = pl.program_id(1)
    Q = SSD_CHUNK
    n = xc_ref.shape[0]

    @pl.when(t == 0)
    def _():
        ht_ref[...] = jnp.zeros_like(ht_ref)

    xc_all = xc_ref[...]
    dt_all = _softplus(dt_ref[...] + dtb_ref[...])
    a_neg = -jnp.exp(alog_ref[...])
    ri = lax.broadcasted_iota(jnp.int32, (Q, Q), 0)
    ci = lax.broadcasted_iota(jnp.int32, (Q, Q), 1)
    causal = ri >= ci
    tri = jnp.where(causal, 1.0, 0.0).astype(BF16)
    lane = lax.broadcasted_iota(jnp.int32, (Q, LANES), 1)
    for ck in range(n // Q):
        rows = slice(ck * Q, (ck + 1) * Q)
        y = _ssd_chunk(xc_all[rows, :], dt_all[rows, :], a_neg, causal, tri, lane, dx_ref, e_ref, ht_ref)
        y_ref[rows, :] = _group_norm_out(y, gate_ref[rows, :], ng_ref, gate_ready=True).astype(y_ref.dtype)

    @pl.when(t == pl.num_programs(1) - 1)
    def _():
        for g in range(SSD_GROUPS):
            hl_ref[g * (SSD_HEADS // SSD_GROUPS):(g + 1) * (SSD_HEADS // SSD_GROUPS)] = (
                ht_ref[g].T.reshape(SSD_HEADS // SSD_GROUPS, SSD_P, SSD_N))


def _ssd_chunk(xc, dt, a_neg, causal, tri, lane, dx_ref, e_ref, ht_ref):
    Q = SSD_CHUNK
    xs = xc[:, :D_SSD]
    bm = xc[:, D_SSD:D_SSD + SSD_GROUPS * SSD_N]
    cm = xc[:, D_SSD + SSD_GROUPS * SSD_N:]
    cum = _dot3_left(tri, dt * a_neg)
    cum_t = cum.T
    both_x = _dot3_right(jnp.concatenate([cum, dt], axis=0), e_ref[...])
    cum_x = both_x[:Q]
    dt_x = both_x[Q:]
    xdt = xs * dt_x
    ecum = jnp.exp(cum_x)
    cl = cum_x[Q - 1:Q, :]
    w_end = xdt * jnp.exp(cl - cum_x)
    dec = jnp.exp(cl)

    y_parts = []
    for g in range(SSD_GROUPS):
        cg = cm[:, g * SSD_N:(g + 1) * SSD_N].astype(BF16)
        bg = bm[:, g * SSD_N:(g + 1) * SSD_N]
        cb = _dot_nt(cg, bg.astype(BF16))
        gl = slice(g * GROUP_W, (g + 1) * GROUP_W)
        h_old = ht_ref[g]
        y_off = jnp.dot(cg, h_old.astype(BF16), preferred_element_type=F32) * ecum[:, gl]
        ht_ref[g] = h_old * dec[:, gl] + jnp.dot(bg.T.astype(BF16), w_end[:, gl].astype(BF16),
                                                  preferred_element_type=F32)
        for pp in range(GROUP_W // LANES):
            pl_ = slice(g * GROUP_W + pp * LANES, g * GROUP_W + (pp + 1) * LANES)
            xdt_pair = xdt[:, pl_].astype(BF16)
            halves = []
            for hh in range(2):
                h = (g * GROUP_W + pp * LANES) // SSD_P + hh
                seg = jnp.broadcast_to(cum[:, h:h + 1], (Q, Q)) - jnp.broadcast_to(cum_t[h:h + 1, :], (Q, Q))
                scores = cb * jnp.exp(jnp.where(causal, seg, NEG))
                halves.append(jnp.dot(scores.astype(BF16), xdt_pair, preferred_element_type=F32))
            y_diag = jnp.where(lane < SSD_P, halves[0], halves[1])
            y_parts.append(y_diag + y_off[:, pp * LANES:(pp + 1) * LANES])
    return jnp.concatenate(y_parts, axis=1) + dx_ref[...] * xs


def _ssd_prompt(xc, dt, gate, dtb, alog, dx, ng, e, sample_attn=None):
    B, S, _ = xc.shape
    Q = SSD_PROMPT_CHUNKS * SSD_CHUNK
    nt = S // Q
    tile = lambda w: pl.BlockSpec((None, Q, w), lambda b, t: (b, t, 0))
    const = lambda r, c: pl.BlockSpec((r, c), lambda b, t: (0, 0))
    in_specs = [tile(D_XBC), tile(LANES), tile(D_SSD), const(1, LANES), const(1, LANES), const(1, D_SSD),
                const(1, D_SSD), const(3 * LANES, D_SSD)]
    args = [xc, dt, gate, dtb, alog, dx, ng, e]
    out_specs = [tile(D_SSD), pl.BlockSpec((None, SSD_HEADS, SSD_P, SSD_N), lambda b, t: (b, 0, 0, 0))]
    out_shape = [jax.ShapeDtypeStruct((B, S, D_SSD), BF16), jax.ShapeDtypeStruct((B, SSD_HEADS, SSD_P, SSD_N), F32)]
    scratch = [pltpu.VMEM((SSD_GROUPS, SSD_N, GROUP_W), F32)]
    n_new = None
    if sample_attn is not None:
        q4, k4, v4, kc_t, vc_t, li, n_new = sample_attn
        small, cache, tables = _attn_s_specs(q4, kc_t, li, n_new, lambda b, t: b * nt + t)
        assert q4.shape[1] // 8 == B * nt
        in_specs += [small, small, small, cache, cache]
        args += [q4, k4, v4, kc_t, vc_t]
        out_specs.append(small)
        out_shape.append(jax.ShapeDtypeStruct(q4.shape, F32))
        scratch += tables
    return pl.pallas_call(
        functools.partial(_ssd_p_kernel, n_new=n_new),
        grid=(B, nt),
        in_specs=in_specs,
        out_specs=out_specs,
        out_shape=out_shape,
        scratch_shapes=scratch,
        compiler_params=_cparams(("arbitrary", "arbitrary")),
        name="ssd_prompt" if sample_attn is None else "ssd_prompt_attn_sample",
    )(*args)


def _ssd_s_kernel(xbc_ref, cs_ref, dt_ref, z_ref, h_ref, cw_ref, cb_ref, dtb_ref, alog_ref, dx_ref, ng_ref, e_ref,
                  *rest, n_new, li):
    y_ref, hn_ref = rest[-2:]
    if hn_ref.ndim == 5:
        for l in range(hn_ref.shape[0]):
            if l != li:
                hn_ref[l] = jnp.zeros(hn_ref.shape[1:], F32)
        hn_ref = hn_ref.at[li]
    nseq = xbc_ref.shape[0]
    pre = [_ssd_s_pre(xbc_ref.at[i], cs_ref.at[i], dt_ref.at[i], cw_ref, cb_ref, dtb_ref, alog_ref, n_new)
           for i in range(nseq)]
    both_x = _dot3_right(jnp.concatenate([p[3] for p in pre] + [p[4] for p in pre], axis=0), e_ref[...])
    for i in range(nseq):
        xs, bm, cm = pre[i][:3]
        _ssd_s_post(xs, bm, cm, both_x[8 * i:8 * i + 8], both_x[8 * (nseq + i):8 * (nseq + i) + 8], z_ref.at[i],
                    h_ref.at[i], dx_ref, ng_ref, y_ref.at[i], hn_ref.at[i], n_new)


def _ssd_s_pre(xbc_ref, cs_ref, dt_ref, cw_ref, cb_ref, dtb_ref, alog_ref, n_new):
    R = 8
    row = lax.broadcasted_iota(jnp.int32, (R, LANES), 0)
    real = row < n_new
    xe = jnp.concatenate([cs_ref[...], xbc_ref[...]], axis=0)
    y = cb_ref[...] + xe[8:16, :] * cw_ref[3:4, :]
    for tap in range(CONV_W - 1):
        y = y + xe[5 + tap:13 + tap, :] * cw_ref[tap:tap + 1, :]
    xc = _silu(y)
    xs = xc[:, :D_SSD]
    bm = xc[:, D_SSD:D_SSD + SSD_GROUPS * SSD_N]
    cm = xc[:, D_SSD + SSD_GROUPS * SSD_N:]

    dt = jnp.where(real, _softplus(dt_ref[...] + dtb_ref[...]), 0.0)
    da = dt * (-jnp.exp(alog_ref[...]))
    cum = da
    k = 1
    while k < R:
        cum = cum + jnp.where(row >= k, pltpu.roll(cum, k, axis=0), 0.0)
        k *= 2
    return xs, bm, cm, cum, dt


def _ssd_s_post(xs, bm, cm, cum_x, dt_x, z_ref, h_ref, dx_ref, ng_ref, y_ref, hn_ref, n_new):
    R = 8
    row = lax.broadcasted_iota(jnp.int32, (R, LANES), 0)
    xdt = xs * dt_x
    ecum = jnp.exp(cum_x)
    cl = cum_x[R - 1:R, :]
    w_end = xdt * jnp.exp(cl - cum_x)
    dec3 = _split3(jnp.exp(cl))
    ones = jnp.ones((R, SSD_N), BF16)
    zrow = jnp.zeros((R - 3, GROUP_W), BF16)

    y_parts = []
    for g in range(SSD_GROUPS):
        gl = slice(g * GROUP_W, (g + 1) * GROUP_W)
        cg = cm[:, g * SSD_N:(g + 1) * SSD_N]
        bg = bm[:, g * SSD_N:(g + 1) * SSD_N]
        hs = slice(g * (SSD_HEADS // SSD_GROUPS), (g + 1) * (SSD_HEADS // SSD_GROUPS))
        h_old = h_ref[hs].reshape(GROUP_W, SSD_N)
        y_g = _dot_nt(cg.astype(BF16), h_old.astype(BF16)) * ecum[:, gl]
        for j in range(n_new):
            cbj = jnp.sum(cg * bg[j:j + 1, :], axis=1, keepdims=True)
            term = cbj * jnp.exp(cum_x[:, gl] - cum_x[j:j + 1, gl]) * xdt[j:j + 1, gl]
            y_g = y_g + jnp.where(row[:, 0:1] >= j, term, 0.0)
        y_parts.append(y_g)
        dec_col = _dot_tn(jnp.concatenate([p[:, gl] for p in dec3] + [zrow], axis=0), ones)
        upd = _dot_tn(w_end[:, gl].astype(BF16), bg.astype(BF16))
        hn_ref[hs] = (h_old * dec_col + upd).reshape(SSD_HEADS // SSD_GROUPS, SSD_P, SSD_N)
    y = jnp.concatenate(y_parts, axis=1) + dx_ref[...] * xs
    y_ref[...] = _group_norm_out(y, z_ref[...], ng_ref)


def _ssd_sample(xbc, cs, dt, z, h_all, li, h_new_prev, cw, cb, dtb, alog, dx, ng, e, n_new):
    DB = xbc.shape[0]
    bb = SSD_SAMPLE_SEQS if DB % SSD_SAMPLE_SEQS == 0 else 1
    tile = lambda w: pl.BlockSpec((bb, 8, w), lambda b: (b, 0, 0))
    const = lambda r, c: pl.BlockSpec((r, c), lambda b: (0, 0))
    state = pl.BlockSpec((None, bb, SSD_HEADS, SSD_P, SSD_N), lambda b: (li, b, 0, 0, 0))
    in_specs = [tile(D_XBC), tile(D_XBC), tile(LANES), tile(D_SSD), state, const(CONV_W, D_XBC), const(1, D_XBC),
                const(1, LANES), const(1, LANES), const(1, D_SSD), const(1, D_SSD), const(3 * LANES, D_SSD)]
    args = [xbc, cs, dt, z, h_all, cw, cb, dtb, alog, dx, ng, e]
    aliases = {}
    if h_new_prev is not None:
        in_specs.append(pl.BlockSpec(memory_space=pl.ANY))
        args.append(h_new_prev)
        aliases = {len(args) - 1: 1}
        state_out = state
    else:
        state_out = pl.BlockSpec((h_all.shape[0], bb, SSD_HEADS, SSD_P, SSD_N), lambda b: (0, b, 0, 0, 0))
    return pl.pallas_call(
        functools.partial(_ssd_s_kernel, n_new=n_new, li=li),
        grid=(DB // bb,),
        in_specs=in_specs,
        out_specs=[tile(D_SSD), state_out],
        out_shape=[jax.ShapeDtypeStruct((DB, 8, D_SSD), F32), jax.ShapeDtypeStruct(h_all.shape, F32)],
        input_output_aliases=aliases,
        compiler_params=_cparams(("arbitrary",)),
        name="ssd_sample",
    )(*args)


def _out_partial(o_ref, gattn_ref, yssd_ref, w_ref):
    o = jnp.concatenate([o_ref[c] for c in range(D_ATTN // LANES)], axis=1)
    y_attn = o * _silu(gattn_ref[...])
    mix_in = jnp.concatenate([y_attn.astype(BF16), yssd_ref[...].astype(BF16)], axis=1)
    return jnp.dot(mix_in, w_ref[D_LRU:, :], preferred_element_type=F32)


def _out_finish(x_ref, y_lru, rest, w_ref, g_ref, out_ref):
    mix = rest + jnp.dot(y_lru.astype(BF16), w_ref[:D_LRU, :], preferred_element_type=F32)
    out_ref[...] = x_ref[...] + mix * lax.rsqrt(jnp.mean(mix * mix, axis=-1, keepdims=True) + EPS) * g_ref[...]


def _out_kernel(x_ref, ylru_ref, o_ref, gattn_ref, yssd_ref, w_ref, g_ref, out_ref):
    _out_finish(x_ref, ylru_ref[...], _out_partial(o_ref, gattn_ref, yssd_ref, w_ref), w_ref, g_ref, out_ref)


def _out_proj(x2d, y_lru, o4, g_attn, y_ssd, w_bf16, g, tm):
    T = x2d.shape[0]
    row = lambda width: pl.BlockSpec((tm, width), lambda i: (i, 0))
    o_spec = pl.BlockSpec((D_ATTN // LANES, tm, LANES), lambda i: (0, i, 0))
    return pl.pallas_call(
        _out_kernel,
        grid=(T // tm,),
        in_specs=[row(D_MODEL), row(D_LRU), o_spec,
                  row(D_ATTN), row(D_SSD), pl.BlockSpec((D_LRU + D_ATTN + D_SSD, D_MODEL), lambda i: (0, 0)),
                  pl.BlockSpec((1, D_MODEL), lambda i: (0, 0))],
        out_specs=row(D_MODEL),
        out_shape=jax.ShapeDtypeStruct((T, D_MODEL), F32),
        compiler_params=_cparams(("arbitrary",)),
        name="out_proj",
    )(x2d, y_lru, o4, g_attn, y_ssd, w_bf16, g)


def _block_diag(w):
    k, n, _ = w.shape
    return jnp.einsum("kij,kl->kilj", w, jnp.eye(k, dtype=w.dtype)).reshape(k * n, k * n)


def _pad_rows(a, rows):
    return jnp.pad(a, ((0, 0), (0, rows - a.shape[1]), (0, 0)))


def _pairs_to_heads(a4, nb, L):
    return a4.reshape(4, nb, L, 2, HEAD_DIM).transpose(1, 2, 0, 3, 4).reshape(nb, L, N_HEADS, HEAD_DIM)


def kernel(x_prompt, x_sample, state_lru_conv, state_lru_h, cache_attn_k, cache_attn_v, state_ssd_conv, state_ssd_h, pre_norm_g, post_norm_g, w_in, lru_conv_w, lru_conv_b, lru_w_a, lru_b_a, lru_w_x, lru_b_x, lru_lambda, ssd_conv_w, ssd_conv_b, ssd_dt_bias, ssd_a_log, ssd_d, ssd_norm_g, w_out):
    depth = w_in.shape[0]
    B, S, _ = x_prompt.shape
    DB, L, _ = x_sample.shape
    assert S % SPAN == 0 and L <= 8 and L >= CONV_W - 1
    win = min(MAX_WINDOW, S)

    lane_pad = lambda a: jnp.pad(a.reshape(1, -1), ((0, 0), (0, LANES - a.shape[-1])))
    expand = jnp.asarray(np.tile(np.repeat(np.eye(LANES, SSD_HEADS, dtype=np.float32), SSD_P, axis=1), (3, 1)), BF16)
    kc_t = jnp.transpose(cache_attn_k, (0, 1, 3, 4, 2))
    vc_t = jnp.transpose(cache_attn_v, (0, 1, 3, 4, 2))

    yp = x_prompt.reshape(B * S, D_MODEL)
    ys = x_sample.reshape(DB * L, D_MODEL)
    outs_p = []
    outs_s = []
    kt_p = vt_p = ssd_h_s = None
    for li in range(depth):
        w_in_b = w_in[li].astype(BF16)
        w_dt_b = jnp.pad(w_in[li][:, OFF_DT:], ((0, 0), (0, D_IN_PAD - D_IN))).astype(BF16)
        w_out_b = w_out[li].astype(BF16)
        pre_g = pre_norm_g[li].reshape(1, -1)
        post_g = post_norm_g[li].reshape(1, -1)
        wg = jnp.concatenate([_block_diag(lru_w_a[li]), _block_diag(lru_w_x[li])], axis=1).astype(BF16)
        bg = jnp.concatenate([lru_b_a[li], lru_b_x[li]]).reshape(1, -1)
        lcw, lcb, lam = lru_conv_w[li], lru_conv_b[li].reshape(1, -1), lru_lambda[li].reshape(1, -1)
        scw, scb = ssd_conv_w[li], ssd_conv_b[li].reshape(1, -1)
        dtb, alog = lane_pad(ssd_dt_bias[li]), lane_pad(ssd_a_log[li])
        dx = jnp.repeat(ssd_d[li], SSD_P).reshape(1, -1)
        ng = ssd_norm_g[li].reshape(1, -1)

        xc_lru, gate_lru, q4, k4, v4, gattn, gate_z, xc_ssd, dtr, kt_p, vt_p, tail_lru, tail_xbc = _proj(
            yp.reshape(B, S, D_MODEL), pre_g, w_in_b, w_dt_b, 512,
            cache=(li, depth, win, kt_p, vt_p, lcw, lcb, scw, scb))
        xlru_s, glru_s, q4s, k4s, v4s, gattn_s, z_s, xbc_s, dtr_s = [
            a.reshape(a.shape[:-3] + (DB * L, a.shape[-1]))
            for a in _proj(ys.reshape(1, DB * L, D_MODEL), pre_g, w_in_b, w_dt_b, min(256, DB * L))]

        o4 = _attn_prompt(q4, k4, v4)
        ssd_steps = B * (S // (SSD_PROMPT_CHUNKS * SSD_CHUNK))
        ride = L & (L - 1) == 0 and 8 % L == 0 and (DB * L) % 8 == 0 and (DB * L) // 8 == ssd_steps
        if ride:
            y_ssd, ssd_h_p, o_s = _ssd_prompt(xc_ssd, dtr, gate_z, dtb, alog, dx, ng, expand,
                                              sample_attn=(q4s, k4s, v4s, kc_t, vc_t, li, L))
        else:
            y_ssd, ssd_h_p = _ssd_prompt(xc_ssd, dtr, gate_z, dtb, alog, dx, ng, expand)
            o_s = _attn_sample(q4s, k4s, v4s, kc_t, vc_t, li, L)
        yp3, lru_h_p = _lru_out_prompt(xc_lru, gate_lru, wg, bg, lam, yp.reshape(B, S, D_MODEL), o4, gattn, y_ssd,
                                       w_out_b, post_g, 512)
        yp = yp3.reshape(B * S, D_MODEL)
        outs_p.append((tail_lru[:, 8 - (CONV_W - 1):], lru_h_p.reshape(B, D_LRU), tail_xbc[:, 8 - (CONV_W - 1):],
                       ssd_h_p))

        xlru3 = xlru_s.reshape(DB, L, D_LRU)
        y_lru_tm, lru_h_s = _lru_sample(
            xlru3.transpose(1, 0, 2), glru_s.reshape(DB, L, D_LRU).transpose(1, 0, 2),
            state_lru_conv[li].transpose(1, 0, 2), state_lru_h[li], lcw, lcb, wg, bg, lam)
        y_lru = y_lru_tm.transpose(1, 0, 2).reshape(DB * L, D_LRU)

        xbc3 = xbc_s.reshape(DB, L, D_XBC)
        cs8 = jnp.pad(state_ssd_conv[li], ((0, 0), (8 - (CONV_W - 1), 0), (0, 0)))
        y_ssd8, ssd_h_s = _ssd_sample(_pad_rows(xbc3, 8), cs8, _pad_rows(dtr_s.reshape(DB, L, LANES), 8),
                                      _pad_rows(z_s.reshape(DB, L, D_SSD), 8), state_ssd_h, li, ssd_h_s, scw, scb,
                                      dtb, alog, dx, ng, expand, L)
        y_ssd = y_ssd8[:, :L].reshape(DB * L, D_SSD)
        ys = _out_proj(ys, y_lru, o_s, gattn_s, y_ssd, w_out_b, post_g, DB * L)
        outs_s.append((xlru3[:, L - (CONV_W - 1):], lru_h_s, _pairs_to_heads(k4s, DB, L),
                       _pairs_to_heads(v4s, DB, L), xbc3[:, L - (CONV_W - 1):]))

    lru_conv_p, lru_h_p, ssd_conv_p, ssd_h_p = [jnp.stack(c) for c in zip(*outs_p)]
    lru_conv_s, lru_h_s, k_s, v_s, ssd_conv_s = [jnp.stack(c) for c in zip(*outs_s)]
    k_p = kt_p.reshape(depth, B, N_HEADS, HEAD_DIM, win).transpose(0, 1, 4, 2, 3)
    v_p = vt_p.reshape(depth, B, N_HEADS, HEAD_DIM, win).transpose(0, 1, 4, 2, 3)
    return (yp.reshape(B, S, D_MODEL), ys.reshape(DB, L, D_MODEL), lru_conv_p, lru_conv_s, lru_h_p, lru_h_s,
            k_p, k_s, v_p, v_s, ssd_conv_p, ssd_conv_s, ssd_h_p, ssd_h_s)
```
